```python
import math
import jax, jax.numpy as jnp
from jax import lax
import numpy as np

D_MODEL = 2048
BATCH = 4
SEQ = 2048
DEPTH = 4
DEC_BATCH = 128
DEC_SEQ = 1
PAST_LEN = 16384
PAGE_SIZE = 128

GLA_HEADS = 4
GLA_DK = D_MODEL // 16
GLA_DV = D_MODEL // 8
GLA_KW = GLA_HEADS * GLA_DK
GLA_VW = GLA_HEADS * GLA_DV
GLA_RANK = 16
GLA_TAU = 16.0
S5_WIDTH = D_MODEL // 2
S5_CH = 16
S5_GROUPS = S5_WIDTH // S5_CH
S5_STATE = 64
GDN_HEADS = 8
GDN_DK = D_MODEL // 16
GDN_DV = D_MODEL // 16
GDN_KW = GDN_HEADS * GDN_DK
GDN_VW = GDN_HEADS * GDN_DV
GDN_CONV = 4
GDN_CONV_CH = 2 * GDN_KW + GDN_VW
N_BRANCH = 3
IN_SIZES = (GLA_KW, GLA_KW, GLA_VW, GLA_VW, GLA_RANK, S5_WIDTH, GDN_KW, GDN_KW, GDN_VW, GDN_VW, GDN_HEADS, GDN_HEADS, N_BRANCH * D_MODEL)
IN_TOTAL = sum(IN_SIZES)
CHUNK = 64
N_EXPERTS = 64
TOP_K = 8
N_GROUPS = 8
TOPK_GROUPS = 4
D_EXPERT = D_MODEL // 4
ROUTED_SCALE = 2.5
MOE_BLOCK = 128
PLE_DIM = 256
LN_EPS = 1e-5
NORM_EPS = 1e-6
DEEPNORM_ALPHA = (2 * DEPTH) ** 0.25
DEEPNORM_BETA = (8 * DEPTH) ** -0.25

kernel_name = 'hybrid_gla_s5_gdn_moe_decode_step'


def layer_norm(x, g, b):
    xf = x.astype(jnp.float32)
    mu = jnp.mean(xf, -1, keepdims=True)
    xc = xf - mu
    var = jnp.mean(xc * xc, -1, keepdims=True)
    return (xc * lax.rsqrt(var + LN_EPS) * g.astype(jnp.float32) + b.astype(jnp.float32)).astype(x.dtype)


def rms_norm(x, w):
    xf = x.astype(jnp.float32)
    return xf * lax.rsqrt(jnp.mean(xf * xf, -1, keepdims=True) + NORM_EPS) * w.astype(jnp.float32)


def l2_norm(x):
    return x * lax.rsqrt(jnp.sum(x * x, -1, keepdims=True) + NORM_EPS)


def split_cols(z, sizes):
    parts, off = [], 0
    for s in sizes:
        parts.append(z[..., off:off + s])
        off += s
    return parts


def to_chunks(a, c):
    b, l = a.shape[:2]
    n = -(-l // c)
    a = jnp.pad(a, [(0, 0), (0, n * c - l)] + [(0, 0)] * (a.ndim - 2))
    a = a.reshape((b, n, c) + a.shape[2:])
    return jnp.moveaxis(jnp.moveaxis(a, 1, 0), 2, 3)


def from_chunks(a, l):
    n, b, h, c = a.shape[:4]
    a = jnp.moveaxis(jnp.moveaxis(a, 3, 2), 0, 1)
    return a.reshape((b, n * c) + a.shape[3:])[:, :l]


def gla_recurrence(q, k, v, log_a, s0):
    l = q.shape[1]
    c = min(CHUNK, l)
    qc, kc, vc, gc = (to_chunks(t, c) for t in (q, k, v, log_a))
    incl = jnp.tril(jnp.ones((c, c), bool))[:, :, None]

    def step(s, blk):
        qb, kb, vb, gb = blk
        cum = jnp.cumsum(gb, axis=2)
        rel = cum[:, :, :, None, :] - cum[:, :, None, :, :]
        dec = jnp.where(incl, jnp.exp(jnp.where(incl, rel, 0.0)), 0.0)
        scores = jnp.einsum('bhtd,bhsd,bhtsd->bhts', qb, kb, dec)
        o = jnp.einsum('bhtd,bhde->bhte', qb * jnp.exp(cum), s) + jnp.einsum('bhts,bhse->bhte', scores, vb)
        last = cum[:, :, -1:, :]
        s = jnp.exp(last[:, :, 0, :, None]) * s + jnp.einsum('bhsd,bhse->bhde', kb * jnp.exp(last - cum), vb)
        return s, o

    s, o = lax.scan(step, s0, (qc, kc, vc, gc))
    return from_chunks(o, l), s


def gdn_recurrence(q, k, v, g, beta, s0):
    l = q.shape[1]
    c = min(CHUNK, l)
    qc, kc, vc, gc, bc = (to_chunks(t, c) for t in (q, k, v, g, beta))
    incl = jnp.tril(jnp.ones((c, c), bool))
    strict = jnp.tril(jnp.ones((c, c), bool), -1)
    eye = jnp.eye(c, dtype=jnp.float32)

    def step(s, blk):
        qb, kb, vb, gb, bb = blk
        cum = jnp.cumsum(gb, axis=-1)
        rel = cum[..., :, None] - cum[..., None, :]
        dec = jnp.where(incl, jnp.exp(jnp.where(incl, rel, 0.0)), 0.0)
        kbeta = kb * bb[..., None]
        a = jnp.where(strict, jnp.einsum('bhtd,bhsd->bhts', kbeta, kb) * dec, 0.0)
        t_inv = lax.linalg.triangular_solve(eye + a, jnp.broadcast_to(eye, a.shape), left_side=True, lower=True, unit_diagonal=True)
        w_dec = jnp.einsum('bhts,bhsd->bhtd', t_inv, kbeta * jnp.exp(cum)[..., None])
        u = jnp.einsum('bhts,bhse->bhte', t_inv, vb * bb[..., None]) - jnp.einsum('bhtd,bhde->bhte', w_dec, s)
        qk = jnp.einsum('bhtd,bhsd->bhts', qb, kb) * dec
        o = jnp.einsum('bhtd,bhde->bhte', qb * jnp.exp(cum)[..., None], s) + jnp.einsum('bhts,bhse->bhte', qk, u)
        last = cum[..., -1:]
        s = jnp.exp(last)[..., None] * s + jnp.einsum('bhsd,bhse->bhde', kb * jnp.exp(last - cum)[..., None], u)
        return s, o

    s, o = lax.scan(step, s0, (qc, kc, vc, gc, bc))
    return from_chunks(o, l), s


def short_conv(x, buf, w):
    l = x.shape[1]
    xp = jnp.concatenate([buf.astype(x.dtype), x], axis=1)
    acc = xp[:, 0:l] * w[0]
    for i in range(1, GDN_CONV):
        acc = acc + xp[:, i:i + l] * w[i]
    return jax.nn.silu(acc), xp[:, l:]


def complex_affine_combine(e1, e2):
    a1r, a1i, b1r, b1i = e1
    a2r, a2i, b2r, b2i = e2
    return (a1r * a2r - a1i * a2i, a1r * a2i + a1i * a2r,
            a2r * b1r - a2i * b1i + b2r, a2r * b1i + a2i * b1r + b2i)


def s5_scan(u, lam_re, lam_im, log_dt, b_re, b_im, c_re, c_im, d_skip, h0_re, h0_im):
    f32 = jnp.float32
    bsz, l, _ = u.shape
    uf = u.astype(f32)
    lam_re, lam_im = lam_re.astype(f32), lam_im.astype(f32)
    dt = jnp.exp(log_dt.astype(f32))[:, None]
    mag = jnp.exp(lam_re * dt)
    ab_re, ab_im = mag * jnp.cos(lam_im * dt), mag * jnp.sin(lam_im * dt)
    den = lam_re * lam_re + lam_im * lam_im
    nr = ab_re - 1.0
    co_re = (nr * lam_re + ab_im * lam_im) / den
    co_im = (ab_im * lam_re - nr * lam_im) / den
    b_re, b_im = b_re.astype(f32), b_im.astype(f32)
    bb_re = co_re[..., None] * b_re - co_im[..., None] * b_im
    bb_im = co_re[..., None] * b_im + co_im[..., None] * b_re
    ug = uf.reshape(bsz, l, S5_GROUPS, S5_CH)
    bu_re = jnp.einsum('blgc,gpc->blgp', ug, bb_re)
    bu_im = jnp.einsum('blgc,gpc->blgp', ug, bb_im)
    shp = bu_re.shape
    elems = (jnp.broadcast_to(ab_re, shp), jnp.broadcast_to(ab_im, shp), bu_re, bu_im)
    a_re, a_im, x_re, x_im = lax.associative_scan(complex_affine_combine, elems, axis=1)
    h0r = h0_re.astype(f32)[:, None]
    h0i = h0_im.astype(f32)[:, None]
    h_re = x_re + a_re * h0r - a_im * h0i
    h_im = x_im + a_re * h0i + a_im * h0r
    y = jnp.einsum('blgp,gcp->blgc', h_re, c_re.astype(f32)) - jnp.einsum('blgp,gcp->blgc', h_im, c_im.astype(f32))
    y = y.reshape(bsz, l, S5_WIDTH) + d_skip.astype(f32) * uf
    return y, h_re[:, -1], h_im[:, -1]


def token_mixers(h, gla_s, s5_re, s5_im, gdn_s, conv_buf, lw):
    f32 = jnp.float32
    bsz, l, _ = h.shape
    z = h @ lw['w_in']
    a_q, a_k, a_v, a_r, a_lr, s_u, c_q, c_k, c_v, c_z, c_a, c_b, gates = split_cols(z, IN_SIZES)
    q = a_q.reshape(bsz, l, GLA_HEADS, GLA_DK).astype(f32) * GLA_DK ** -0.5
    k = a_k.reshape(bsz, l, GLA_HEADS, GLA_DK).astype(f32)
    v = a_v.reshape(bsz, l, GLA_HEADS, GLA_DV).astype(f32)
    log_a = jax.nn.log_sigmoid((a_lr @ lw['gla_w_gate'] + lw['gla_b_gate']).astype(f32)) / GLA_TAU
    o_a, gla_new = gla_recurrence(q, k, v, log_a.reshape(bsz, l, GLA_HEADS, GLA_DK), gla_s.astype(f32))
    branch_a = (rms_norm(o_a, lw['gla_norm']).reshape(bsz, l, GLA_VW) * jax.nn.silu(a_r.astype(f32))).astype(h.dtype)
    y_s, s5_re_new, s5_im_new = s5_scan(s_u, lw['s5_lam_re'], lw['s5_lam_im'], lw['s5_log_dt'], lw['s5_b_re'], lw['s5_b_im'],
                                        lw['s5_c_re'], lw['s5_c_im'], lw['s5_d'], s5_re, s5_im)
    y_s = jax.nn.gelu(y_s).astype(h.dtype)
    branch_s = y_s * jax.nn.sigmoid(y_s @ lw['s5_w_glu'] + lw['s5_b_glu'])
    qkv, conv_new = short_conv(jnp.concatenate([c_q, c_k, c_v], axis=-1), conv_buf, lw['gdn_conv_w'])
    gq, gk, gv = split_cols(qkv.astype(f32), (GDN_KW, GDN_KW, GDN_VW))
    gq = l2_norm(gq.reshape(bsz, l, GDN_HEADS, GDN_DK)) * GDN_DK ** -0.5
    gk = l2_norm(gk.reshape(bsz, l, GDN_HEADS, GDN_DK))
    gv = gv.reshape(bsz, l, GDN_HEADS, GDN_DV)
    g = -jnp.exp(lw['gdn_a_log'].astype(f32)) * jax.nn.softplus(c_a.astype(f32) + lw['gdn_dt_bias'].astype(f32))
    beta = jax.nn.sigmoid(c_b.astype(f32))
    o_c, gdn_new = gdn_recurrence(gq, gk, gv, g, beta, gdn_s.astype(f32))
    zg = jax.nn.silu(c_z.astype(f32).reshape(bsz, l, GDN_HEADS, GDN_DV))
    branch_c = (rms_norm(o_c, lw['gdn_norm']) * zg).reshape(bsz, l, GDN_VW).astype(h.dtype)
    gate = jax.nn.sigmoid(gates.astype(f32)).reshape(bsz, l, N_BRANCH, D_MODEL)
    merged = (gate[:, :, 0] * (branch_a @ lw['w_branch_a']) + gate[:, :, 1] * (branch_s @ lw['w_branch_s'])
              + gate[:, :, 2] * (branch_c @ lw['w_branch_c']))
    out = merged.astype(h.dtype) @ lw['w_out']
    new_states = (gla_new.astype(gla_s.dtype), s5_re_new.astype(s5_re.dtype), s5_im_new.astype(s5_im.dtype),
                  gdn_new.astype(gdn_s.dtype), conv_new)
    return out, new_states


def swiglu(x, w1, w3, w2):
    return (jax.nn.silu(x @ w1) * (x @ w3)) @ w2


def routed_experts(xf, idx, wts, w1, w3, w2):
    n_pair = xf.shape[0] * TOP_K
    e_flat = idx.reshape(n_pair)
    tok_flat = jnp.arange(n_pair, dtype=jnp.int32) // TOP_K
    w_flat = wts.reshape(n_pair).astype(jnp.float32)
    order = jnp.argsort(e_flat)
    e_sorted = e_flat[order]
    counts = jnp.bincount(e_flat, length=N_EXPERTS)
    padded = (counts + MOE_BLOCK - 1) // MOE_BLOCK * MOE_BLOCK
    start = jnp.cumsum(counts) - counts
    pad_end = jnp.cumsum(padded)
    pad_start = pad_end - padded
    dest = pad_start[e_sorted] + jnp.arange(n_pair, dtype=jnp.int32) - start[e_sorted]
    n_blocks = -(-(n_pair + N_EXPERTS * (MOE_BLOCK - 1)) // MOE_BLOCK)
    n_rows = n_blocks * MOE_BLOCK
    row_tok = jnp.zeros((n_rows,), jnp.int32).at[dest].set(tok_flat[order])
    row_w = jnp.zeros((n_rows,), jnp.float32).at[dest].set(w_flat[order])
    block_e = jnp.minimum(jnp.searchsorted(pad_end, jnp.arange(n_blocks, dtype=jnp.int32) * MOE_BLOCK, side='right'), N_EXPERTS - 1)

    def step(acc, blk):
        toks, wb, e = blk
        yb = swiglu(xf[toks], w1[e], w3[e], w2[e])
        return acc.at[toks].add(yb.astype(jnp.float32) * wb[:, None]), None

    acc, _ = lax.scan(step, jnp.zeros(xf.shape, jnp.float32),
                      (row_tok.reshape(n_blocks, MOE_BLOCK), row_w.reshape(n_blocks, MOE_BLOCK), block_e))
    return acc


def moe_ffn(x, w_router, b_router, w1, w3, w2, ws1, ws3, ws2):
    bsz, l, d = x.shape
    xf = x.reshape(bsz * l, d)
    scores = jax.nn.sigmoid((xf @ w_router).astype(jnp.float32))
    choice = (scores + b_router.astype(jnp.float32)).reshape(-1, N_GROUPS, N_EXPERTS // N_GROUPS)
    group_score = jnp.sum(lax.top_k(choice, 2)[0], axis=-1)
    _, top_groups = lax.top_k(group_score, TOPK_GROUPS)
    keep = jnp.any(top_groups[:, :, None] == jnp.arange(N_GROUPS)[None, None, :], axis=1)
    choice = jnp.where(keep[:, :, None], choice, -jnp.inf).reshape(-1, N_EXPERTS)
    _, idx = lax.top_k(choice, TOP_K)
    wts = jnp.take_along_axis(scores, idx, axis=1)
    wts = wts / jnp.sum(wts, axis=-1, keepdims=True) * ROUTED_SCALE
    y = routed_experts(xf, idx, wts, w1, w3, w2) + swiglu(xf, ws1, ws3, ws2).astype(jnp.float32)
    return y.reshape(bsz, l, d).astype(x.dtype)


def trunk_layer(x, pe, states, lw):
    mix, new_states = token_mixers(x, *states, lw)
    x = layer_norm(DEEPNORM_ALPHA * x + mix, lw['ln1_g'], lw['ln1_b'])
    ffn = moe_ffn(x, lw['moe_w_router'], lw['moe_b_router'], lw['moe_w1'], lw['moe_w3'], lw['moe_w2'],
                  lw['moe_ws1'], lw['moe_ws3'], lw['moe_ws2'])
    x = layer_norm(DEEPNORM_ALPHA * x + ffn, lw['ln2_g'], lw['ln2_b'])
    x = x + jax.nn.sigmoid(x @ lw['ple_w_gate']) * (pe.astype(x.dtype) @ lw['ple_w_proj'])
    return x, new_states


def setup_inputs(seed: int = 0) -> dict:
    key = jax.random.key(seed)
    ks = jax.random.split(key, 45)
    f32 = jnp.float32
    D = D_MODEL

    def nrm(i, shape, scale):
        return jax.random.normal(ks[i], shape, f32) * scale

    def unif(i, shape, lo, hi):
        return jax.random.uniform(ks[i], shape, f32, lo, hi)

    dt_gdn = jnp.exp(unif(25, (DEPTH, GDN_HEADS), math.log(0.001), math.log(0.1)))
    return {
        'x_prompt': nrm(0, (BATCH, SEQ, D), 1.0),
        'x_sample': nrm(1, (DEC_BATCH, DEC_SEQ, D), 1.0),
        'p_prompt': nrm(2, (DEPTH, BATCH, SEQ, PLE_DIM), 1.0),
        'p_sample': nrm(3, (DEPTH, DEC_BATCH, DEC_SEQ, PLE_DIM), 1.0),
        'state_gla': nrm(4, (DEPTH, DEC_BATCH, GLA_HEADS, GLA_DK, GLA_DV), 0.5),
        'state_s5_re': nrm(5, (DEPTH, DEC_BATCH, S5_GROUPS, S5_STATE), 0.5),
        'state_s5_im': nrm(6, (DEPTH, DEC_BATCH, S5_GROUPS, S5_STATE), 0.5),
        'state_gdn': nrm(7, (DEPTH, DEC_BATCH, GDN_HEADS, GDN_DK, GDN_DV), 0.1),
        'state_gdn_conv': nrm(8, (DEPTH, DEC_BATCH, GDN_CONV - 1, GDN_CONV_CH), 1.0),
        'w_in': nrm(9, (DEPTH, D, IN_TOTAL), D ** -0.5),
        'gla_w_gate': nrm(10, (DEPTH, GLA_RANK, GLA_KW), GLA_RANK ** -0.5),
        'gla_b_gate': nrm(11, (DEPTH, GLA_KW), 0.1),
        'gla_norm': 1.0 + nrm(12, (DEPTH, GLA_DV), 0.02),
        's5_lam_re': -0.5 + nrm(13, (DEPTH, S5_GROUPS, S5_STATE), 0.01),
        's5_lam_im': jnp.pi * jnp.arange(S5_STATE, dtype=f32) + nrm(14, (DEPTH, S5_GROUPS, S5_STATE), 0.01),
        's5_log_dt': unif(15, (DEPTH, S5_GROUPS), math.log(0.001), math.log(0.1)),
        's5_b_re': nrm(16, (DEPTH, S5_GROUPS, S5_STATE, S5_CH), (2 * S5_CH) ** -0.5),
        's5_b_im': nrm(17, (DEPTH, S5_GROUPS, S5_STATE, S5_CH), (2 * S5_CH) ** -0.5),
        's5_c_re': nrm(18, (DEPTH, S5_GROUPS, S5_CH, S5_STATE), S5_STATE ** -0.5),
        's5_c_im': nrm(19, (DEPTH, S5_GROUPS, S5_CH, S5_STATE), S5_STATE ** -0.5),
        's5_d': nrm(20, (DEPTH, S5_WIDTH), 1.0),
        's5_w_glu': nrm(21, (DEPTH, S5_WIDTH, S5_WIDTH), S5_WIDTH ** -0.5),
        's5_b_glu': nrm(22, (DEPTH, S5_WIDTH), 0.02),
        'gdn_conv_w': nrm(23, (DEPTH, GDN_CONV, GDN_CONV_CH), GDN_CONV ** -0.5),
        'gdn_a_log': jnp.log(unif(24, (DEPTH, GDN_HEADS), 1.0, 16.0)),
        'gdn_dt_bias': dt_gdn + jnp.log(-jnp.expm1(-dt_gdn)),
        'gdn_norm': 1.0 + nrm(26, (DEPTH, GDN_DV), 0.02),
        'w_branch_a': nrm(27, (DEPTH, GLA_VW, D), GLA_VW ** -0.5 * DEEPNORM_BETA),
        'w_branch_s': nrm(28, (DEPTH, S5_WIDTH, D), S5_WIDTH ** -0.5 * DEEPNORM_BETA),
        'w_branch_c': nrm(29, (DEPTH, GDN_VW, D), GDN_VW ** -0.5 * DEEPNORM_BETA),
        'w_out': nrm(30, (DEPTH, D, D), D ** -0.5 * DEEPNORM_BETA),
        'ln1_g': 1.0 + nrm(31, (DEPTH, D), 0.02),
        'ln1_b': nrm(32, (DEPTH, D), 0.02),
        'ln2_g': 1.0 + nrm(33, (DEPTH, D), 0.02),
        'ln2_b': nrm(34, (DEPTH, D), 0.02),
        'moe_w_router': nrm(35, (DEPTH, D, N_EXPERTS), D ** -0.5),
        'moe_b_router': nrm(36, (DEPTH, N_EXPERTS), 0.01),
        'moe_w1': nrm(37, (DEPTH, N_EXPERTS, D, D_EXPERT), D ** -0.5),
        'moe_w3': nrm(38, (DEPTH, N_EXPERTS, D, D_EXPERT), D ** -0.5),
        'moe_w2': nrm(39, (DEPTH, N_EXPERTS, D_EXPERT, D), D_EXPERT ** -0.5 * DEEPNORM_BETA),
        'moe_ws1': nrm(40, (DEPTH, D, D_EXPERT), D ** -0.5),
        'moe_ws3': nrm(41, (DEPTH, D, D_EXPERT), D ** -0.5),
        'moe_ws2': nrm(42, (DEPTH, D_EXPERT, D), D_EXPERT ** -0.5 * DEEPNORM_BETA),
        'ple_w_proj': nrm(43, (DEPTH, PLE_DIM, D), PLE_DIM ** -0.5),
        'ple_w_gate': nrm(44, (DEPTH, D, D), D ** -0.5),
    }


def reference(x_prompt, x_sample, p_prompt, p_sample, state_gla, state_s5_re, state_s5_im, state_gdn, state_gdn_conv,
              w_in, gla_w_gate, gla_b_gate, gla_norm, s5_lam_re, s5_lam_im, s5_log_dt, s5_b_re, s5_b_im, s5_c_re, s5_c_im,
              s5_d, s5_w_glu, s5_b_glu, gdn_conv_w, gdn_a_log, gdn_dt_bias, gdn_norm, w_branch_a, w_branch_s, w_branch_c,
              w_out, ln1_g, ln1_b, ln2_g, ln2_b, moe_w_router, moe_b_router, moe_w1, moe_w3, moe_w2, moe_ws1, moe_ws3,
              moe_ws2, ple_w_proj, ple_w_gate):
    bp = x_prompt.shape[0]
    init_prompt = (
        jnp.zeros((bp, GLA_HEADS, GLA_DK, GLA_DV), state_gla.dtype),
        jnp.zeros((bp, S5_GROUPS, S5_STATE), state_s5_re.dtype),
        jnp.zeros((bp, S5_GROUPS, S5_STATE), state_s5_im.dtype),
        jnp.zeros((bp, GDN_HEADS, GDN_DK, GDN_DV), state_gdn.dtype),
        jnp.zeros((bp, GDN_CONV - 1, GDN_CONV_CH), x_prompt.dtype),
    )
    yp, ys = x_prompt, x_sample
    new_p, new_s = [], []
    for i in range(DEPTH):
        lw = dict(
            w_in=w_in[i], gla_w_gate=gla_w_gate[i], gla_b_gate=gla_b_gate[i], gla_norm=gla_norm[i],
            s5_lam_re=s5_lam_re[i], s5_lam_im=s5_lam_im[i], s5_log_dt=s5_log_dt[i], s5_b_re=s5_b_re[i],
            s5_b_im=s5_b_im[i], s5_c_re=s5_c_re[i], s5_c_im=s5_c_im[i], s5_d=s5_d[i], s5_w_glu=s5_w_glu[i],
            s5_b_glu=s5_b_glu[i], gdn_conv_w=gdn_conv_w[i], gdn_a_log=gdn_a_log[i], gdn_dt_bias=gdn_dt_bias[i],
            gdn_norm=gdn_norm[i], w_branch_a=w_branch_a[i], w_branch_s=w_branch_s[i], w_branch_c=w_branch_c[i],
            w_out=w_out[i], ln1_g=ln1_g[i], ln1_b=ln1_b[i], ln2_g=ln2_g[i], ln2_b=ln2_b[i],
            moe_w_router=moe_w_router[i], moe_b_router=moe_b_router[i], moe_w1=moe_w1[i], moe_w3=moe_w3[i],
            moe_w2=moe_w2[i], moe_ws1=moe_ws1[i], moe_ws3=moe_ws3[i], moe_ws2=moe_ws2[i],
            ple_w_proj=ple_w_proj[i], ple_w_gate=ple_w_gate[i],
        )
        yp, st_p = trunk_layer(yp, p_prompt[i], init_prompt, lw)
        ys, st_s = trunk_layer(ys, p_sample[i], (state_gla[i], state_s5_re[i], state_s5_im[i], state_gdn[i], state_gdn_conv[i]), lw)
        new_p.append(st_p)
        new_s.append(st_s)
    gla_p, s5r_p, s5i_p, gdn_p, conv_p = (jnp.stack(f) for f in zip(*new_p))
    gla_s, s5r_s, s5i_s, gdn_s, conv_s = (jnp.stack(f) for f in zip(*new_s))
    return (yp, ys, gla_p, gla_s, s5r_p, s5r_s, s5i_p, s5i_s, gdn_p, gdn_s, conv_p, conv_s)
```

```python
import functools
import math

import jax
import jax.numpy as jnp
from jax import lax
from jax.experimental import pallas as pl
from jax.experimental.pallas import tpu as pltpu

F32 = jnp.float32
BF16 = jnp.bfloat16

D_MODEL = 2048
DEPTH = 4
GLA_HEADS, GLA_DK, GLA_DV = 4, 128, 256
GLA_KW, GLA_VW, GLA_RANK, GLA_TAU = 512, 1024, 16, 16.0
S5_WIDTH, S5_CH, S5_GROUPS, S5_STATE = 1024, 16, 64, 64
GDN_HEADS, GDN_DK, GDN_DV = 8, 128, 128
GDN_KW, GDN_VW, GDN_CONV = 1024, 1024, 4
N_BRANCH = 3
IN_SIZES = (GLA_KW, GLA_KW, GLA_VW, GLA_VW, GLA_RANK, S5_WIDTH, GDN_KW, GDN_KW, GDN_VW, GDN_VW, GDN_HEADS,
            GDN_HEADS, N_BRANCH * D_MODEL)
CHUNK = 64
N_EXPERTS, TOP_K, N_GROUPS, TOPK_GROUPS = 64, 8, 8, 4
D_EXPERT = 512
ROUTED_SCALE = 2.5
MOE_BLOCK = 128
LN_EPS = 1e-5
NORM_EPS = 1e-6
DEEPNORM_ALPHA = (2 * DEPTH) ** 0.25

VMEM_LIMIT_BYTES = 56 * 1024 * 1024


def _mm_kernel(x_ref, w_ref, o_ref):
    o_ref[...] = jnp.dot(x_ref[...].astype(BF16), w_ref[...].astype(BF16),
                         preferred_element_type=F32).astype(o_ref.dtype)


def _pick_tile(n, candidates):
    for c in candidates:
        if n % c == 0:
            return c
    return n


def mm(x, w, out_dtype=F32):
    m, k = x.shape
    _, n = w.shape
    tm = _pick_tile(m, (1024, 832, 640, 512, 256, 128))
    tn = _pick_tile(n, (512, 256, 128))
    return pl.pallas_call(
        _mm_kernel,
        grid=(m // tm, n // tn),
        in_specs=[pl.BlockSpec((tm, k), lambda i, j: (i, 0)),
                  pl.BlockSpec((k, tn), lambda i, j: (0, j))],
        out_specs=pl.BlockSpec((tm, tn), lambda i, j: (i, j)),
        out_shape=jax.ShapeDtypeStruct((m, n), out_dtype),
        compiler_params=pltpu.CompilerParams(dimension_semantics=("parallel", "parallel"),
                                             vmem_limit_bytes=VMEM_LIMIT_BYTES),
        name="mm",
    )(x, w)


def mm3(x, w):
    b, l, d = x.shape
    return mm(x.reshape(b * l, d), w).reshape(b, l, w.shape[1])


def layer_norm(x, g, b):
    mu = jnp.mean(x, -1, keepdims=True)
    xc = x - mu
    var = jnp.mean(xc * xc, -1, keepdims=True)
    return xc * lax.rsqrt(var + LN_EPS) * g + b


def rms_norm(x, w):
    return x * lax.rsqrt(jnp.mean(x * x, -1, keepdims=True) + NORM_EPS) * w


def l2_norm(x):
    return x * lax.rsqrt(jnp.sum(x * x, -1, keepdims=True) + NORM_EPS)


def split_cols(z, sizes):
    parts, off = [], 0
    for s in sizes:
        parts.append(z[..., off:off + s])
        off += s
    return parts


def to_chunks(a, c):
    b, l = a.shape[:2]
    n = -(-l // c)
    a = jnp.pad(a, [(0, 0), (0, n * c - l)] + [(0, 0)] * (a.ndim - 2))
    a = a.reshape((b, n, c) + a.shape[2:])
    return jnp.moveaxis(jnp.moveaxis(a, 1, 0), 2, 3)


def from_chunks(a, l):
    n, b, h, c = a.shape[:4]
    a = jnp.moveaxis(jnp.moveaxis(a, 3, 2), 0, 1)
    return a.reshape((b, n * c) + a.shape[3:])[:, :l]


def gla_recurrence(q, k, v, log_a, s0):
    l = q.shape[1]
    c = min(CHUNK, l)
    qc, kc, vc, gc = (to_chunks(t, c) for t in (q, k, v, log_a))
    incl = jnp.tril(jnp.ones((c, c), bool))[:, :, None]

    def step(s, blk):
        qb, kb, vb, gb = blk
        cum = jnp.cumsum(gb, axis=2)
        rel = cum[:, :, :, None, :] - cum[:, :, None, :, :]
        dec = jnp.where(incl, jnp.exp(jnp.where(incl, rel, 0.0)), 0.0)
        scores = jnp.einsum('bhtd,bhsd,bhtsd->bhts', qb, kb, dec)
        o = jnp.einsum('bhtd,bhde->bhte', qb * jnp.exp(cum), s) + jnp.einsum('bhts,bhse->bhte', scores, vb)
        last = cum[:, :, -1:, :]
        s = jnp.exp(last[:, :, 0, :, None]) * s + jnp.einsum('bhsd,bhse->bhde', kb * jnp.exp(last - cum), vb)
        return s, o

    s, o = lax.scan(step, s0, (qc, kc, vc, gc))
    return from_chunks(o, l), s


def gdn_recurrence(q, k, v, g, beta, s0):
    l = q.shape[1]
    c = min(CHUNK, l)
    qc, kc, vc, gc, bc = (to_chunks(t, c) for t in (q, k, v, g, beta))
    incl = jnp.tril(jnp.ones((c, c), bool))
    strict = jnp.tril(jnp.ones((c, c), bool), -1)
    eye = jnp.eye(c, dtype=jnp.float32)

    def step(s, blk):
        qb, kb, vb, gb, bb = blk
        cum = jnp.cumsum(gb, axis=-1)
        rel = cum[..., :, None] - cum[..., None, :]
        dec = jnp.where(incl, jnp.exp(jnp.where(incl, rel, 0.0)), 0.0)
        kbeta = kb * bb[..., None]
        a = jnp.where(strict, jnp.einsum('bhtd,bhsd->bhts', kbeta, kb) * dec, 0.0)
        t_inv = lax.linalg.triangular_solve(eye + a, jnp.broadcast_to(eye, a.shape), left_side=True, lower=True,
                                            unit_diagonal=True)
        w_dec = jnp.einsum('bhts,bhsd->bhtd', t_inv, kbeta * jnp.exp(cum)[..., None])
        u = jnp.einsum('bhts,bhse->bhte', t_inv, vb * bb[..., None]) - jnp.einsum('bhtd,bhde->bhte', w_dec, s)
        qk = jnp.einsum('bhtd,bhsd->bhts', qb, kb) * dec
        o = jnp.einsum('bhtd,bhde->bhte', qb * jnp.exp(cum)[..., None], s) + jnp.einsum('bhts,bhse->bhte', qk, u)
        last = cum[..., -1:]
        s = jnp.exp(last)[..., None] * s + jnp.einsum('bhsd,bhse->bhde', kb * jnp.exp(last - cum)[..., None], u)
        return s, o

    s, o = lax.scan(step, s0, (qc, kc, vc, gc, bc))
    return from_chunks(o, l), s


def short_conv(x, buf, w):
    l = x.shape[1]
    xp = jnp.concatenate([buf.astype(x.dtype), x], axis=1)
    acc = xp[:, 0:l] * w[0]
    for i in range(1, GDN_CONV):
        acc = acc + xp[:, i:i + l] * w[i]
    return jax.nn.silu(acc), xp[:, l:]


def complex_affine_combine(e1, e2):
    a1r, a1i, b1r, b1i = e1
    a2r, a2i, b2r, b2i = e2
    return (a1r * a2r - a1i * a2i, a1r * a2i + a1i * a2r,
            a2r * b1r - a2i * b1i + b2r, a2r * b1i + a2i * b1r + b2i)


def s5_scan(u, lam_re, lam_im, log_dt, b_re, b_im, c_re, c_im, d_skip, h0_re, h0_im):
    bsz, l, _ = u.shape
    dt = jnp.exp(log_dt)[:, None]
    mag = jnp.exp(lam_re * dt)
    ab_re, ab_im = mag * jnp.cos(lam_im * dt), mag * jnp.sin(lam_im * dt)
    den = lam_re * lam_re + lam_im * lam_im
    nr = ab_re - 1.0
    co_re = (nr * lam_re + ab_im * lam_im) / den
    co_im = (ab_im * lam_re - nr * lam_im) / den
    bb_re = co_re[..., None] * b_re - co_im[..., None] * b_im
    bb_im = co_re[..., None] * b_im + co_im[..., None] * b_re
    ug = u.reshape(bsz, l, S5_GROUPS, S5_CH)
    bu_re = jnp.einsum('blgc,gpc->blgp', ug, bb_re)
    bu_im = jnp.einsum('blgc,gpc->blgp', ug, bb_im)
    shp = bu_re.shape
    elems = (jnp.broadcast_to(ab_re, shp), jnp.broadcast_to(ab_im, shp), bu_re, bu_im)
    a_re, a_im, x_re, x_im = lax.associative_scan(complex_affine_combine, elems, axis=1)
    h0r = h0_re[:, None]
    h0i = h0_im[:, None]
    h_re = x_re + a_re * h0r - a_im * h0i
    h_im = x_im + a_re * h0i + a_im * h0r
    y = jnp.einsum('blgp,gcp->blgc', h_re, c_re) - jnp.einsum('blgp,gcp->blgc', h_im, c_im)
    y = y.reshape(bsz, l, S5_WIDTH) + d_skip * u
    return y, h_re[:, -1], h_im[:, -1]


def token_mixers(h, gla_s, s5_re, s5_im, gdn_s, conv_buf, lw):
    bsz, l, _ = h.shape
    w_in = lw['w_in']
    z1 = mm3(h, w_in[:, 0:3072])
    a_lr = mm3(h, w_in[:, 3072:3088])
    z3 = mm3(h, w_in[:, 3088:8208])
    z4 = mm3(h, w_in[:, 8208:8224])
    gates = mm3(h, w_in[:, 8224:])
    a_q, a_k, a_v, a_r = split_cols(z1, IN_SIZES[0:4])
    s_u, c_q, c_k, c_v, c_z = split_cols(z3, IN_SIZES[5:10])
    c_a, c_b = split_cols(z4, IN_SIZES[10:12])
    q = a_q.reshape(bsz, l, GLA_HEADS, GLA_DK) * GLA_DK ** -0.5
    k = a_k.reshape(bsz, l, GLA_HEADS, GLA_DK)
    v = a_v.reshape(bsz, l, GLA_HEADS, GLA_DV)
    log_a = jax.nn.log_sigmoid(a_lr @ lw['gla_w_gate'] + lw['gla_b_gate']) / GLA_TAU
    o_a, gla_new = gla_recurrence(q, k, v, log_a.reshape(bsz, l, GLA_HEADS, GLA_DK), gla_s)
    branch_a = rms_norm(o_a, lw['gla_norm']).reshape(bsz, l, GLA_VW) * jax.nn.silu(a_r)
    y_s, s5_re_new, s5_im_new = s5_scan(s_u, lw['s5_lam_re'], lw['s5_lam_im'], lw['s5_log_dt'], lw['s5_b_re'],
                                        lw['s5_b_im'], lw['s5_c_re'], lw['s5_c_im'], lw['s5_d'], s5_re, s5_im)
    y_s = jax.nn.gelu(y_s)
    branch_s = y_s * jax.nn.sigmoid(mm3(y_s, lw['s5_w_glu']) + lw['s5_b_glu'])
    qkv, conv_new = short_conv(jnp.concatenate([c_q, c_k, c_v], axis=-1), conv_buf, lw['gdn_conv_w'])
    gq, gk, gv = split_cols(qkv, (GDN_KW, GDN_KW, GDN_VW))
    gq = l2_norm(gq.reshape(bsz, l, GDN_HEADS, GDN_DK)) * GDN_DK ** -0.5
    gk = l2_norm(gk.reshape(bsz, l, GDN_HEADS, GDN_DK))
    gv = gv.reshape(bsz, l, GDN_HEADS, GDN_DV)
    g = -jnp.exp(lw['gdn_a_log']) * jax.nn.softplus(c_a + lw['gdn_dt_bias'])
    beta = jax.nn.sigmoid(c_b)
    o_c, gdn_new = gdn_recurrence(gq, gk, gv, g, beta, gdn_s)
    zg = jax.nn.silu(c_z.reshape(bsz, l, GDN_HEADS, GDN_DV))
    branch_c = (rms_norm(o_c, lw['gdn_norm']) * zg).reshape(bsz, l, GDN_VW)
    gate = jax.nn.sigmoid(gates).reshape(bsz, l, N_BRANCH, D_MODEL)
    merged = (gate[:, :, 0] * mm3(branch_a, lw['w_branch_a']) + gate[:, :, 1] * mm3(branch_s, lw['w_branch_s'])
              + gate[:, :, 2] * mm3(branch_c, lw['w_branch_c']))
    out = mm3(merged, lw['w_out'])
    return out, (gla_new, s5_re_new, s5_im_new, gdn_new, conv_new)


def swiglu(x, w1, w3, w2):
    return mm(jax.nn.silu(mm(x, w1)) * mm(x, w3), w2)


def swiglu_plain(x, w1, w3, w2):
    return (jax.nn.silu(x @ w1) * (x @ w3)) @ w2


def routed_experts(xf, idx, wts, w1, w3, w2):
    n_pair = xf.shape[0] * TOP_K
    e_flat = idx.reshape(n_pair)
    tok_flat = jnp.arange(n_pair, dtype=jnp.int32) // TOP_K
    w_flat = wts.reshape(n_pair).astype(jnp.float32)
    order = jnp.argsort(e_flat)
    e_sorted = e_flat[order]
    counts = jnp.bincount(e_flat, length=N_EXPERTS)
    padded = (counts + MOE_BLOCK - 1) // MOE_BLOCK * MOE_BLOCK
    start = jnp.cumsum(counts) - counts
    pad_end = jnp.cumsum(padded)
    pad_start = pad_end - padded
    dest = pad_start[e_sorted] + jnp.arange(n_pair, dtype=jnp.int32) - start[e_sorted]
    n_blocks = -(-(n_pair + N_EXPERTS * (MOE_BLOCK - 1)) // MOE_BLOCK)
    n_rows = n_blocks * MOE_BLOCK
    row_tok = jnp.zeros((n_rows,), jnp.int32).at[dest].set(tok_flat[order])
    row_w = jnp.zeros((n_rows,), jnp.float32).at[dest].set(w_flat[order])
    block_e = jnp.minimum(jnp.searchsorted(pad_end, jnp.arange(n_blocks, dtype=jnp.int32) * MOE_BLOCK, side='right'),
                          N_EXPERTS - 1)

    def step(acc, blk):
        toks, wb, e = blk
        yb = swiglu_plain(xf[toks], w1[e], w3[e], w2[e])
        return acc.at[toks].add(yb.astype(jnp.float32) * wb[:, None]), None

    acc, _ = lax.scan(step, jnp.zeros(xf.shape, jnp.float32),
                      (row_tok.reshape(n_blocks, MOE_BLOCK), row_w.reshape(n_blocks, MOE_BLOCK), block_e))
    return acc


def moe_ffn(x, w_router, b_router, w1, w3, w2, ws1, ws3, ws2):
    bsz, l, d = x.shape
    xf = x.reshape(bsz * l, d)
    scores = jax.nn.sigmoid(xf @ w_router)
    choice = (scores + b_router).reshape(-1, N_GROUPS, N_EXPERTS // N_GROUPS)
    group_score = jnp.sum(lax.top_k(choice, 2)[0], axis=-1)
    _, top_groups = lax.top_k(group_score, TOPK_GROUPS)
    keep = jnp.any(top_groups[:, :, None] == jnp.arange(N_GROUPS)[None, None, :], axis=1)
    choice = jnp.where(keep[:, :, None], choice, -jnp.inf).reshape(-1, N_EXPERTS)
    _, idx = lax.top_k(choice, TOP_K)
    wts = jnp.take_along_axis(scores, idx, axis=1)
    wts = wts / jnp.sum(wts, axis=-1, keepdims=True) * ROUTED_SCALE
    y = routed_experts(xf, idx, wts, w1, w3, w2) + swiglu(xf, ws1, ws3, ws2)
    return y.reshape(bsz, l, d)


def trunk_layer(x, pe, states, lw):
    mix, new_states = token_mixers(x, *states, lw)
    x = layer_norm(DEEPNORM_ALPHA * x + mix, lw['ln1_g'], lw['ln1_b'])
    ffn = moe_ffn(x, lw['moe_w_router'], lw['moe_b_router'], lw['moe_w1'], lw['moe_w3'], lw['moe_w2'],
                  lw['moe_ws1'], lw['moe_ws3'], lw['moe_ws2'])
    x = layer_norm(DEEPNORM_ALPHA * x + ffn, lw['ln2_g'], lw['ln2_b'])
    x = x + jax.nn.sigmoid(mm3(x, lw['ple_w_gate'])) * mm3(pe, lw['ple_w_proj'])
    return x, new_states


_NAMES = ('w_in', 'gla_w_gate', 'gla_b_gate', 'gla_norm', 's5_lam_re', 's5_lam_im', 's5_log_dt', 's5_b_re',
          's5_b_im', 's5_c_re', 's5_c_im', 's5_d', 's5_w_glu', 's5_b_glu', 'gdn_conv_w', 'gdn_a_log',
          'gdn_dt_bias', 'gdn_norm', 'w_branch_a', 'w_branch_s', 'w_branch_c', 'w_out', 'ln1_g', 'ln1_b',
          'ln2_g', 'ln2_b', 'moe_w_router', 'moe_b_router', 'moe_w1', 'moe_w3', 'moe_w2', 'moe_ws1', 'moe_ws3',
          'moe_ws2', 'ple_w_proj', 'ple_w_gate')


def kernel(x_prompt, x_sample, p_prompt, p_sample, state_gla, state_s5_re, state_s5_im, state_gdn, state_gdn_conv,
           w_in, gla_w_gate, gla_b_gate, gla_norm, s5_lam_re, s5_lam_im, s5_log_dt, s5_b_re, s5_b_im, s5_c_re,
           s5_c_im, s5_d, s5_w_glu, s5_b_glu, gdn_conv_w, gdn_a_log, gdn_dt_bias, gdn_norm, w_branch_a,
           w_branch_s, w_branch_c, w_out, ln1_g, ln1_b, ln2_g, ln2_b, moe_w_router, moe_b_router, moe_w1, moe_w3,
           moe_w2, moe_ws1, moe_ws3, moe_ws2, ple_w_proj, ple_w_gate):
    weights = (w_in, gla_w_gate, gla_b_gate, gla_norm, s5_lam_re, s5_lam_im, s5_log_dt, s5_b_re, s5_b_im, s5_c_re,
               s5_c_im, s5_d, s5_w_glu, s5_b_glu, gdn_conv_w, gdn_a_log, gdn_dt_bias, gdn_norm, w_branch_a,
               w_branch_s, w_branch_c, w_out, ln1_g, ln1_b, ln2_g, ln2_b, moe_w_router, moe_b_router, moe_w1,
               moe_w3, moe_w2, moe_ws1, moe_ws3, moe_ws2, ple_w_proj, ple_w_gate)
    bp = x_prompt.shape[0]
    init_prompt = (
        jnp.zeros((bp, GLA_HEADS, GLA_DK, GLA_DV), F32),
        jnp.zeros((bp, S5_GROUPS, S5_STATE), F32),
        jnp.zeros((bp, S5_GROUPS, S5_STATE), F32),
        jnp.zeros((bp, GDN_HEADS, GDN_DK, GDN_DV), F32),
        jnp.zeros((bp, GDN_CONV - 1, 2 * GDN_KW + GDN_VW), F32),
    )
    yp, ys = x_prompt, x_sample
    new_p, new_s = [], []
    for i in range(DEPTH):
        lw = {n: w[i] for n, w in zip(_NAMES, weights)}
        yp, st_p = trunk_layer(yp, p_prompt[i], init_prompt, lw)
        ys, st_s = trunk_layer(ys, p_sample[i], (state_gla[i], state_s5_re[i], state_s5_im[i], state_gdn[i],
                                                 state_gdn_conv[i]), lw)
        new_p.append(st_p)
        new_s.append(st_s)
    gla_p, s5r_p, s5i_p, gdn_p, conv_p = (jnp.stack(f) for f in zip(*new_p))
    gla_s, s5r_s, s5i_s, gdn_s, conv_s = (jnp.stack(f) for f in zip(*new_s))
    return (yp, ys, gla_p, gla_s, s5r_p, s5r_s, s5i_p, s5i_s, gdn_p, gdn_s, conv_p, conv_s)
```

```python
import functools
import math

import jax
import jax.numpy as jnp
from jax import lax
from jax.experimental import pallas as pl
from jax.experimental.pallas import tpu as pltpu

F32 = jnp.float32
BF16 = jnp.bfloat16
I32 = jnp.int32

D_MODEL = 2048
DEPTH = 4
GLA_HEADS, GLA_DK, GLA_DV = 4, 128, 256
GLA_KW, GLA_VW, GLA_RANK, GLA_TAU = 512, 1024, 16, 16.0
S5_WIDTH, S5_CH, S5_GROUPS, S5_STATE = 1024, 16, 64, 64
GDN_HEADS, GDN_DK, GDN_DV = 8, 128, 128
GDN_KW, GDN_VW, GDN_CONV = 1024, 1024, 4
N_BRANCH = 3
IN_SIZES = (GLA_KW, GLA_KW, GLA_VW, GLA_VW, GLA_RANK, S5_WIDTH, GDN_KW, GDN_KW, GDN_VW, GDN_VW, GDN_HEADS,
            GDN_HEADS, N_BRANCH * D_MODEL)
CHUNK = 64
N_EXPERTS, TOP_K, N_GROUPS, TOPK_GROUPS = 64, 8, 8, 4
D_EXPERT = 512
ROUTED_SCALE = 2.5
LN_EPS = 1e-5
NORM_EPS = 1e-6
DEEPNORM_ALPHA = (2 * DEPTH) ** 0.25

LANES = 128
VMEM_LIMIT_BYTES = 56 * 1024 * 1024
S5_CS = 16
S5_TILES = S5_WIDTH // LANES
S5_TSTATE = (LANES // S5_CH) * S5_STATE
MOE_ROWS = 256
MOE_TOK_TILE = 128


def _cparams(sem):
    return pltpu.CompilerParams(dimension_semantics=sem, vmem_limit_bytes=VMEM_LIMIT_BYTES)


def _pick_tile(n, candidates):
    for c in candidates:
        if n % c == 0:
            return c
    return n


def _sigmoid(x):
    return 1.0 / (1.0 + jnp.exp(-x))


def _silu(x):
    return x * _sigmoid(x)


def _gelu_tanh(x):
    return 0.5 * x * (1.0 + jnp.tanh(math.sqrt(2.0 / math.pi) * (x + 0.044715 * (x * x * x))))


def _log_sigmoid(x):
    return jnp.minimum(x, 0.0) - jnp.log1p(jnp.exp(-jnp.abs(x)))


def _bdot(a, b):
    return jnp.dot(a.astype(BF16), b.astype(BF16), preferred_element_type=F32)


def dense(x, w, *, bias=None, act=None, out_dtype=F32, tm=None, tn=None):
    m, k = x.shape
    n = w.shape[1]
    tm = tm or _pick_tile(m, (640, 512, 256, 128, 64, 32, 16, 8))
    tn = tn or _pick_tile(n, (512, 256, 128))

    def body(x_ref, w_ref, *rest):
        o_ref = rest[-1]
        y = _bdot(x_ref[...], w_ref[...])
        if bias is not None:
            y = y + rest[0][...]
        if act == 'log_decay':
            y = _log_sigmoid(y) / GLA_TAU
        o_ref[...] = y.astype(o_ref.dtype)

    in_specs = [pl.BlockSpec((tm, k), lambda i, j: (i, 0)), pl.BlockSpec((k, tn), lambda i, j: (0, j))]
    args = [x, w]
    if bias is not None:
        in_specs.append(pl.BlockSpec((1, tn), lambda i, j: (0, j)))
        args.append(bias.reshape(1, n))
    return pl.pallas_call(
        body, grid=(m // tm, n // tn), in_specs=in_specs,
        out_specs=pl.BlockSpec((tm, tn), lambda i, j: (i, j)),
        out_shape=jax.ShapeDtypeStruct((m, n), out_dtype),
        compiler_params=_cparams(("parallel", "parallel")), name="dense")(*args)


def swiglu_hidden(x, w1, w3):
    m, k = x.shape
    n = w1.shape[1]
    tm = _pick_tile(m, (640, 512, 256, 128, 64, 32, 16, 8))
    tn = _pick_tile(n, (512, 256, 128))

    def body(x_ref, w1_ref, w3_ref, o_ref):
        xb = x_ref[...].astype(BF16)
        a = jnp.dot(xb, w1_ref[...].astype(BF16), preferred_element_type=F32)
        b = jnp.dot(xb, w3_ref[...].astype(BF16), preferred_element_type=F32)
        o_ref[...] = (_silu(a) * b).astype(o_ref.dtype)

    return pl.pallas_call(
        body, grid=(m // tm, n // tn),
        in_specs=[pl.BlockSpec((tm, k), lambda i, j: (i, 0)), pl.BlockSpec((k, tn), lambda i, j: (0, j)),
                  pl.BlockSpec((k, tn), lambda i, j: (0, j))],
        out_specs=pl.BlockSpec((tm, tn), lambda i, j: (i, j)),
        out_shape=jax.ShapeDtypeStruct((m, n), BF16),
        compiler_params=_cparams(("parallel", "parallel")), name="swiglu_hidden")(x, w1, w3)


def glu_gate(y, w, b):
    m, n = y.shape
    tm = _pick_tile(m, (640, 512, 256, 128, 64, 32, 16, 8))
    tn = _pick_tile(n, (512, 256, 128))

    def body(y_ref, yt_ref, w_ref, b_ref, o_ref):
        g = _bdot(y_ref[...], w_ref[...]) + b_ref[...]
        o_ref[...] = (yt_ref[...] * _sigmoid(g)).astype(o_ref.dtype)

    return pl.pallas_call(
        body, grid=(m // tm, n // tn),
        in_specs=[pl.BlockSpec((tm, n), lambda i, j: (i, 0)), pl.BlockSpec((tm, tn), lambda i, j: (i, j)),
                  pl.BlockSpec((n, tn), lambda i, j: (0, j)), pl.BlockSpec((1, tn), lambda i, j: (0, j))],
        out_specs=pl.BlockSpec((tm, tn), lambda i, j: (i, j)),
        out_shape=jax.ShapeDtypeStruct((m, n), BF16),
        compiler_params=_cparams(("parallel", "parallel")), name="glu_gate")(y, y, w, b.reshape(1, n))


def merge_branches(xb, w_gates, br_a, br_s, br_c, w_a, w_s, w_c):
    m, k = xb.shape
    d = w_a.shape[1]
    kb = br_a.shape[1]
    tm = _pick_tile(m, (832, 640, 512, 256, 128, 64, 32, 16, 8))
    tn = _pick_tile(d, (256, 128))
    nj = d // tn

    def body(x_ref, g0_ref, g1_ref, g2_ref, a_ref, s_ref, c_ref, wa_ref, ws_ref, wc_ref, o_ref):
        x = x_ref[...]
        acc = _sigmoid(_bdot(x, g0_ref[...])) * _bdot(a_ref[...], wa_ref[...])
        acc = acc + _sigmoid(_bdot(x, g1_ref[...])) * _bdot(s_ref[...], ws_ref[...])
        acc = acc + _sigmoid(_bdot(x, g2_ref[...])) * _bdot(c_ref[...], wc_ref[...])
        o_ref[...] = acc.astype(o_ref.dtype)

    def gate_spec(b):
        return pl.BlockSpec((k, tn), lambda i, j: (0, b * nj + j))

    act_spec = pl.BlockSpec((tm, kb), lambda i, j: (i, 0))
    w_spec = pl.BlockSpec((kb, tn), lambda i, j: (0, j))
    return pl.pallas_call(
        body, grid=(m // tm, nj),
        in_specs=[pl.BlockSpec((tm, k), lambda i, j: (i, 0)), gate_spec(0), gate_spec(1), gate_spec(2),
                  act_spec, act_spec, act_spec, w_spec, w_spec, w_spec],
        out_specs=pl.BlockSpec((tm, tn), lambda i, j: (i, j)),
        out_shape=jax.ShapeDtypeStruct((m, d), BF16),
        compiler_params=_cparams(("parallel", "parallel")), name="merge_branches")(
            xb, w_gates, w_gates, w_gates, br_a, br_s, br_c, w_a, w_s, w_c)


def _layer_norm_rows(y, g, b):
    mu = jnp.mean(y, axis=-1, keepdims=True)
    yc = y - mu
    var = jnp.mean(yc * yc, axis=-1, keepdims=True)
    return yc * lax.rsqrt(var + LN_EPS) * g + b


def out_proj_ln(merged, w_out, x, g, b):
    m, k = merged.shape
    d = w_out.shape[1]
    tm = _pick_tile(m, (416, 256, 128, 64, 32, 16))

    def body(m_ref, w_ref, x_ref, g_ref, b_ref, o_ref, ob_ref):
        y = DEEPNORM_ALPHA * x_ref[...] + _bdot(m_ref[...], w_ref[...])
        y = _layer_norm_rows(y, g_ref[...], b_ref[...])
        o_ref[...] = y
        ob_ref[...] = y.astype(BF16)

    row = pl.BlockSpec((1, d), lambda i: (0, 0))
    return pl.pallas_call(
        body, grid=(m // tm,),
        in_specs=[pl.BlockSpec((tm, k), lambda i: (i, 0)), pl.BlockSpec((k, d), lambda i: (0, 0)),
                  pl.BlockSpec((tm, d), lambda i: (i, 0)), row, row],
        out_specs=[pl.BlockSpec((tm, d), lambda i: (i, 0)), pl.BlockSpec((tm, d), lambda i: (i, 0))],
        out_shape=[jax.ShapeDtypeStruct((m, d), F32), jax.ShapeDtypeStruct((m, d), BF16)],
        compiler_params=_cparams(("parallel",)), name="out_proj_ln")(
            merged, w_out, x, g.reshape(1, d), b.reshape(1, d))


def ple_mix(x, xb, w_gate, pe, w_proj):
    m, d = x.shape
    kp = pe.shape[1]
    tm = _pick_tile(m, (640, 512, 256, 128, 64, 32, 16))
    tn = _pick_tile(d, (512, 256, 128))

    def body(xb_ref, wg_ref, pe_ref, wp_ref, x_ref, o_ref, ob_ref):
        y = x_ref[...] + _sigmoid(_bdot(xb_ref[...], wg_ref[...])) * _bdot(pe_ref[...], wp_ref[...])
        o_ref[...] = y
        ob_ref[...] = y.astype(BF16)

    return pl.pallas_call(
        body, grid=(m // tm, d // tn),
        in_specs=[pl.BlockSpec((tm, d), lambda i, j: (i, 0)), pl.BlockSpec((d, tn), lambda i, j: (0, j)),
                  pl.BlockSpec((tm, kp), lambda i, j: (i, 0)), pl.BlockSpec((kp, tn), lambda i, j: (0, j)),
                  pl.BlockSpec((tm, tn), lambda i, j: (i, j))],
        out_specs=[pl.BlockSpec((tm, tn), lambda i, j: (i, j)), pl.BlockSpec((tm, tn), lambda i, j: (i, j))],
        out_shape=[jax.ShapeDtypeStruct((m, d), F32), jax.ShapeDtypeStruct((m, d), BF16)],
        compiler_params=_cparams(("parallel", "parallel")), name="ple_mix")(xb, w_gate, pe, w_proj, x)


def s5_tables(lam_re, lam_im, log_dt, b_re, b_im, c_re, c_im):
    hp = lax.Precision.HIGHEST
    cs, nt, gl = S5_CS, S5_TILES, LANES // S5_CH
    dt = jnp.exp(log_dt)[:, None]
    mag = jnp.exp(lam_re * dt)
    ab_re, ab_im = mag * jnp.cos(lam_im * dt), mag * jnp.sin(lam_im * dt)
    den = lam_re * lam_re + lam_im * lam_im
    nr = ab_re - 1.0
    co_re = (nr * lam_re + ab_im * lam_im) / den
    co_im = (ab_im * lam_re - nr * lam_im) / den
    bb_re = co_re[..., None] * b_re - co_im[..., None] * b_im
    bb_im = co_re[..., None] * b_im + co_im[..., None] * b_re
    pr, pi = [jnp.ones_like(ab_re)], [jnp.zeros_like(ab_im)]
    for _ in range(cs):
        pr.append(pr[-1] * ab_re - pi[-1] * ab_im)
        pi.append(pr[-2] * ab_im + pi[-1] * ab_re)
    ap_re, ap_im = jnp.stack(pr), jnp.stack(pi)
    abr = ap_re[:, :, :, None] * bb_re - ap_im[:, :, :, None] * bb_im
    abi = ap_re[:, :, :, None] * bb_im + ap_im[:, :, :, None] * bb_re
    kern = (jnp.einsum('gcp,egpd->egcd', c_re, abr[:cs], precision=hp)
            - jnp.einsum('gcp,egpd->egcd', c_im, abi[:cs], precision=hp))
    eye = jnp.eye(gl, dtype=F32)
    lag = jnp.arange(cs)[None, :] - jnp.arange(cs)[:, None]
    ksel = jnp.where((lag >= 0)[:, :, None, None, None], kern[jnp.clip(lag, 0, cs - 1)], 0.0)
    ksel = ksel.reshape(cs, cs, nt, gl, S5_CH, S5_CH)
    toep = jnp.einsum('stjgcd,gh->jsgdthc', ksel, eye).reshape(nt, cs * LANES, cs * LANES)
    er = abr[:cs][::-1].reshape(cs, nt, gl, S5_STATE, S5_CH)
    ei = abi[:cs][::-1].reshape(cs, nt, gl, S5_STATE, S5_CH)
    bend = jnp.stack([jnp.einsum('sjgpd,gh->jsgdhp', er, eye), jnp.einsum('sjgpd,gh->jsgdhp', ei, eye)], axis=4)
    bend = bend.reshape(nt, cs * LANES, 2 * S5_TSTATE)
    car = c_re[None] * ap_re[:, :, None, :] - c_im[None] * ap_im[:, :, None, :]
    cai = -(c_re[None] * ap_im[:, :, None, :] + c_im[None] * ap_re[:, :, None, :])
    car = car.reshape(cs + 1, nt, gl, S5_CH, S5_STATE)
    cai = cai.reshape(cs + 1, nt, gl, S5_CH, S5_STATE)
    ccar = jnp.stack([jnp.einsum('ejgcp,gh->jgpehc', car, eye), jnp.einsum('ejgcp,gh->jgpehc', cai, eye)], axis=1)
    ccar = ccar.reshape(nt, 2 * S5_TSTATE, (cs + 1) * LANES)

    def state_row(re, im):
        return jnp.concatenate([re.reshape(nt, 1, S5_TSTATE), im.reshape(nt, 1, S5_TSTATE)], axis=-1)

    return dict(toep=toep.astype(BF16), bend=bend.astype(BF16), bbar=bend[:, (cs - 1) * LANES:].astype(BF16),
                c0=ccar[:, :, :LANES].astype(BF16), ccar=ccar[:, :, LANES:].astype(BF16),
                a1=state_row(ap_re[1], ap_im[1]), acs=state_row(ap_re[cs], ap_im[cs]))


def s5_chunk_states(uf, bend):
    nt, nc, kc = uf.shape
    n = bend.shape[2]
    tn = 512

    def body(u_ref, b_ref, o_ref):
        o_ref[0] = _bdot(u_ref[0], b_ref[0])

    return pl.pallas_call(
        body, grid=(nt, n // tn),
        in_specs=[pl.BlockSpec((1, nc, kc), lambda j, n_: (j, 0, 0)),
                  pl.BlockSpec((1, kc, tn), lambda j, n_: (j, 0, n_))],
        out_specs=pl.BlockSpec((1, nc, tn), lambda j, n_: (j, 0, n_)),
        out_shape=jax.ShapeDtypeStruct((nt, nc, n), F32),
        compiler_params=_cparams(("parallel", "parallel")), name="s5_chunk_states")(uf, bend)


def s5_carry_scan(xe, acs, n_seq):
    nt, nc, n = xe.shape
    per = nc // n_seq
    half = n // 2

    def body(x_ref, a_ref, hp_ref, hf_ref):
        ar = a_ref[0, :, :half]
        ai = a_ref[0, :, half:]

        def step(k, carry):
            hr, hi = carry
            hp_ref[0, pl.ds(k, 1), :] = jnp.concatenate([hr, hi], axis=1)
            x = x_ref[0, pl.ds(k, 1), :]
            return (ar * hr - ai * hi + x[:, :half], ar * hi + ai * hr + x[:, half:])

        zero = jnp.zeros((1, half), F32)
        hr, hi = lax.fori_loop(0, per, step, (zero, zero))
        hf_ref[0, 0] = jnp.concatenate([hr, hi], axis=1)

    return pl.pallas_call(
        body, grid=(nt, n_seq),
        in_specs=[pl.BlockSpec((1, per, n), lambda j, b: (j, b, 0)), pl.BlockSpec((1, 1, n), lambda j, b: (j, 0, 0))],
        out_specs=[pl.BlockSpec((1, per, n), lambda j, b: (j, b, 0)),
                   pl.BlockSpec((1, 1, 1, n), lambda j, b: (j, b, 0, 0))],
        out_shape=[jax.ShapeDtypeStruct((nt, nc, n), F32), jax.ShapeDtypeStruct((nt, n_seq, 1, n), F32)],
        compiler_params=_cparams(("parallel", "parallel")), name="s5_carry_scan")(xe, acs)


def s5_outputs(uf, toep, hprev, ccar, d_flat):
    nt, nc, kc = uf.shape
    ns = hprev.shape[2]
    tn = 512

    def body(u_ref, ut_ref, t_ref, h_ref, c_ref, d_ref, o_ref):
        y = _bdot(u_ref[0], t_ref[0]) + _bdot(h_ref[0], c_ref[0]) + d_ref[0] * ut_ref[0]
        o_ref[0] = _gelu_tanh(y)

    return pl.pallas_call(
        body, grid=(nt, kc // tn),
        in_specs=[pl.BlockSpec((1, nc, kc), lambda j, n_: (j, 0, 0)), pl.BlockSpec((1, nc, tn), lambda j, n_: (j, 0, n_)),
                  pl.BlockSpec((1, kc, tn), lambda j, n_: (j, 0, n_)), pl.BlockSpec((1, nc, ns), lambda j, n_: (j, 0, 0)),
                  pl.BlockSpec((1, ns, tn), lambda j, n_: (j, 0, n_)), pl.BlockSpec((1, 1, tn), lambda j, n_: (j, 0, n_))],
        out_specs=pl.BlockSpec((1, nc, tn), lambda j, n_: (j, 0, n_)),
        out_shape=jax.ShapeDtypeStruct((nt, nc, kc), F32),
        compiler_params=_cparams(("parallel", "parallel")), name="s5_outputs")(uf, uf, toep, hprev, ccar, d_flat)


def s5_decode(u, h_re, h_im, bbar, a1, c0, d_skip):
    s, w = u.shape
    nt = w // LANES
    ts = S5_TSTATE

    def body(u_ref, hr_ref, hi_ref, b_ref, a_ref, c_ref, d_ref, y_ref, nr_ref, ni_ref):
        uu = u_ref[...]
        x = _bdot(uu, b_ref[0])
        ar, ai = a_ref[0, :, :ts], a_ref[0, :, ts:]
        hr, hi = hr_ref[...], hi_ref[...]
        nr = ar * hr - ai * hi + x[:, :ts]
        ni = ar * hi + ai * hr + x[:, ts:]
        nr_ref[...] = nr
        ni_ref[...] = ni
        y = _bdot(jnp.concatenate([nr, ni], axis=1), c_ref[0]) + d_ref[...] * uu
        y_ref[...] = _gelu_tanh(y)

    col = pl.BlockSpec((s, LANES), lambda j: (0, j))
    st = pl.BlockSpec((s, ts), lambda j: (0, j))
    return pl.pallas_call(
        body, grid=(nt,),
        in_specs=[col, st, st, pl.BlockSpec((1, LANES, 2 * ts), lambda j: (j, 0, 0)),
                  pl.BlockSpec((1, 1, 2 * ts), lambda j: (j, 0, 0)), pl.BlockSpec((1, 2 * ts, LANES), lambda j: (j, 0, 0)),
                  pl.BlockSpec((1, LANES), lambda j: (0, j))],
        out_specs=[col, st, st],
        out_shape=[jax.ShapeDtypeStruct((s, w), F32), jax.ShapeDtypeStruct(h_re.shape, F32),
                   jax.ShapeDtypeStruct(h_im.shape, F32)],
        compiler_params=_cparams(("parallel",)), name="s5_decode")(u, h_re, h_im, bbar, a1, c0, d_skip.reshape(1, w))


def s5_branch(u_all, n_seq, seq_len, h_re, h_im, tabs, d_skip):
    t_p = n_seq * seq_len
    nt, cs = S5_TILES, S5_CS
    nc = t_p // cs
    uf = u_all[:t_p].reshape(nc, cs, nt, LANES).transpose(2, 0, 1, 3).reshape(nt, nc, cs * LANES)
    xe = s5_chunk_states(uf, tabs['bend'])
    hprev, hfin = s5_carry_scan(xe, tabs['acs'], n_seq)
    d_flat = jnp.tile(d_skip.reshape(nt, 1, LANES), (1, 1, cs))
    yf = s5_outputs(uf, tabs['toep'], hprev, tabs['ccar'], d_flat)
    y_p = yf.reshape(nt, nc, cs, LANES).transpose(1, 2, 0, 3).reshape(t_p, nt * LANES)
    hfin = hfin.reshape(nt, n_seq, 2, S5_TSTATE).transpose(2, 1, 0, 3).reshape(2, n_seq, S5_GROUPS, S5_STATE)
    s_rows = u_all.shape[0] - t_p
    y_s, nr, ni = s5_decode(u_all[t_p:], h_re.reshape(s_rows, -1), h_im.reshape(s_rows, -1), tabs['bbar'],
                            tabs['a1'], tabs['c0'], d_skip)
    return (jnp.concatenate([y_p, y_s], axis=0), hfin[0], hfin[1], nr.reshape(h_re.shape), ni.reshape(h_im.shape))


def moe_route(xb, w_router_t, b_router, *, tm):
    t, d = xb.shape
    ne, ng, gs = N_EXPERTS, N_GROUPS, N_EXPERTS // N_GROUPS
    neg = -jnp.inf

    def body(x_ref, w_ref, b_ref, u_ref, idx_ref, wt_ref, rank_ref, cnt_ref, carry_ref):
        @pl.when(pl.program_id(0) == 0)
        def _():
            carry_ref[...] = jnp.zeros_like(carry_ref)

        logits = lax.dot_general(w_ref[...], x_ref[...], (((1,), (1,)), ((), ())), preferred_element_type=F32)
        scores = _sigmoid(logits).reshape(ng, gs, tm)
        choice = scores + b_ref[...].reshape(ng, gs, 1)
        e_in = lax.broadcasted_iota(I32, (ng, gs, tm), 1).astype(F32)
        g_id = lax.broadcasted_iota(I32, (ng, 1, tm), 0).astype(F32)
        e_id = g_id * gs + e_in
        m1 = jnp.max(choice, axis=1, keepdims=True)
        i1 = jnp.min(jnp.where(choice == m1, e_in, float(gs)), axis=1, keepdims=True)
        m2 = jnp.max(jnp.where(e_in == i1, neg, choice), axis=1, keepdims=True)
        gscore = m1 + m2
        keep = jnp.zeros((ng, 1, tm), F32)
        for _ in range(TOPK_GROUPS):
            gm = jnp.max(gscore, axis=0, keepdims=True)
            gi = jnp.min(jnp.where(gscore == gm, g_id, float(ng)), axis=0, keepdims=True)
            hit = g_id == gi
            keep = jnp.where(hit, 1.0, keep)
            gscore = jnp.where(hit, neg, gscore)
        cand = jnp.where(keep > 0.0, choice, neg)
        member = jnp.zeros((ng, gs, tm), F32)
        picks, wts = [], []
        for _ in range(TOP_K):
            cm = jnp.max(jnp.max(cand, axis=1, keepdims=True), axis=0, keepdims=True)
            ei = jnp.min(jnp.min(jnp.where(cand == cm, e_id, float(ne)), axis=1, keepdims=True), axis=0, keepdims=True)
            sel = e_id == ei
            wts.append(jnp.sum(jnp.sum(jnp.where(sel, scores, 0.0), axis=1, keepdims=True), axis=0, keepdims=True))
            picks.append(ei)
            member = jnp.where(sel, 1.0, member)
            cand = jnp.where(sel, neg, cand)
        wsum = wts[0]
        for w in wts[1:]:
            wsum = wsum + w
        member2 = member.reshape(ne, tm)
        prefix = jnp.dot(member2.astype(BF16), u_ref[...], preferred_element_type=F32) + carry_ref[:, 0:1]
        prefix = prefix.reshape(ng, gs, tm)
        for j in range(TOP_K):
            sel = e_id == picks[j]
            rk = jnp.sum(jnp.sum(jnp.where(sel, prefix, 0.0), axis=1, keepdims=True), axis=0, keepdims=True)
            idx_ref[j:j + 1, :] = picks[j].reshape(1, tm).astype(I32)
            rank_ref[j:j + 1, :] = rk.reshape(1, tm).astype(I32)
            wt_ref[j:j + 1, :] = (wts[j] / wsum * ROUTED_SCALE).reshape(1, tm)
        carry_ref[...] = carry_ref[...] + jnp.sum(member2, axis=1, keepdims=True)
        cnt_ref[...] = carry_ref[...]

    upper = jnp.triu(jnp.ones((tm, tm), F32), 1).astype(BF16)
    tok = pl.BlockSpec((TOP_K, tm), lambda i: (0, i))
    idx, wt, rank, cnt = pl.pallas_call(
        body, grid=(t // tm,),
        in_specs=[pl.BlockSpec((tm, d), lambda i: (i, 0)), pl.BlockSpec((ne, d), lambda i: (0, 0)),
                  pl.BlockSpec((ne, 1), lambda i: (0, 0)), pl.BlockSpec((tm, tm), lambda i: (0, 0))],
        out_specs=[tok, tok, tok, pl.BlockSpec((ne, LANES), lambda i: (0, 0))],
        out_shape=[jax.ShapeDtypeStruct((TOP_K, t), I32), jax.ShapeDtypeStruct((TOP_K, t), F32),
                   jax.ShapeDtypeStruct((TOP_K, t), I32), jax.ShapeDtypeStruct((ne, LANES), F32)],
        scratch_shapes=[pltpu.VMEM((ne, LANES), F32)],
        compiler_params=_cparams(("arbitrary",)), name="moe_route")(xb, w_router_t, b_router.reshape(ne, 1), upper)
    return idx, wt, rank, cnt[:, 0]


def moe_experts(x, row_tok, block_e, n_used, w1, w3, w2, *, rows):
    t, d = x.shape
    nb = row_tok.shape[0]
    f = w1.shape[2]

    def body(be_ref, nu_ref, x_hbm, tok_ref, tokn_ref, w1_ref, w3_ref, w2_ref, o_ref, buf, sem, w1b, w3b, w2b):
        i = pl.program_id(0)
        slot = lax.rem(i, 2)

        def gather(tref, sl):
            def issue(r, c):
                tok = tref[0, 0, r]
                pltpu.make_async_copy(x_hbm.at[pl.ds(tok, 1), :], buf.at[sl, pl.ds(r, 1), :], sem.at[sl]).start()
                return c
            lax.fori_loop(0, rows, issue, 0)

        @pl.when(i == 0)
        def _():
            gather(tok_ref, 0)

        @pl.when(i + 1 < nb)
        def _():
            gather(tokn_ref, 1 - slot)

        e = be_ref[i]
        changed = jnp.logical_or(i == 0, e != be_ref[jnp.maximum(i - 1, 0)])

        @pl.when(changed)
        def _():
            w1b[...] = w1_ref[0].astype(BF16)
            w3b[...] = w3_ref[0].astype(BF16)
            w2b[...] = w2_ref[0].astype(BF16)

        pltpu.make_async_copy(x_hbm.at[pl.ds(0, rows), :], buf.at[slot], sem.at[slot]).wait()

        @pl.when(i < nu_ref[0])
        def _():
            xb = buf[slot].astype(BF16)
            h = _silu(jnp.dot(xb, w1b[...], preferred_element_type=F32)) * jnp.dot(xb, w3b[...],
                                                                                preferred_element_type=F32)
            o_ref[...] = jnp.dot(h.astype(BF16), w2b[...], preferred_element_type=F32)

        @pl.when(i >= nu_ref[0])
        def _():
            o_ref[...] = jnp.zeros_like(o_ref)

    grid_spec = pltpu.PrefetchScalarGridSpec(
        num_scalar_prefetch=2, grid=(nb,),
        in_specs=[pl.BlockSpec(memory_space=pl.ANY),
                  pl.BlockSpec((1, 1, rows), lambda i, be, nu: (i, 0, 0), memory_space=pltpu.SMEM),
                  pl.BlockSpec((1, 1, rows), lambda i, be, nu: (jnp.minimum(i + 1, nb - 1), 0, 0),
                               memory_space=pltpu.SMEM),
                  pl.BlockSpec((1, d, f), lambda i, be, nu: (be[i], 0, 0)),
                  pl.BlockSpec((1, d, f), lambda i, be, nu: (be[i], 0, 0)),
                  pl.BlockSpec((1, f, d), lambda i, be, nu: (be[i], 0, 0))],
        out_specs=pl.BlockSpec((rows, d), lambda i, be, nu: (i, 0)),
        scratch_shapes=[pltpu.VMEM((2, rows, d), F32), pltpu.SemaphoreType.DMA((2,)),
                        pltpu.VMEM((d, f), BF16), pltpu.VMEM((d, f), BF16), pltpu.VMEM((f, d), BF16)])
    return pl.pallas_call(
        body, grid_spec=grid_spec, out_shape=jax.ShapeDtypeStruct((nb * rows, d), F32),
        compiler_params=_cparams(("arbitrary",)), name="moe_experts")(block_e, n_used, x, row_tok, row_tok, w1, w3, w2)


def moe_combine_ln(ys, dest, wts, x, shared, g, b, *, tm):
    t, d = x.shape
    nt = t // tm

    def body(ys_hbm, d_ref, dn_ref, w_ref, x_ref, s_ref, g_ref, b_ref, o_ref, ob_ref, buf, sem):
        i = pl.program_id(0)
        slot = lax.rem(i, 2)

        def gather(dref, sl):
            def issue(r, c):
                row = dref[0, 0, r]
                pltpu.make_async_copy(ys_hbm.at[pl.ds(row, 1), :], buf.at[sl, pl.ds(r, 1), :], sem.at[sl]).start()
                return c
            lax.fori_loop(0, TOP_K * tm, issue, 0)

        @pl.when(i == 0)
        def _():
            gather(d_ref, 0)

        @pl.when(i + 1 < nt)
        def _():
            gather(dn_ref, 1 - slot)

        pltpu.make_async_copy(ys_hbm.at[pl.ds(0, TOP_K * tm), :], buf.at[slot], sem.at[slot]).wait()
        w = w_ref[...]
        acc = DEEPNORM_ALPHA * x_ref[...] + s_ref[...]
        for j in range(TOP_K):
            acc = acc + w[:, j:j + 1] * buf[slot, j * tm:(j + 1) * tm, :]
        y = _layer_norm_rows(acc, g_ref[...], b_ref[...])
        o_ref[...] = y
        ob_ref[...] = y.astype(BF16)

    row = pl.BlockSpec((1, d), lambda i: (0, 0))
    tile = pl.BlockSpec((tm, d), lambda i: (i, 0))
    return pl.pallas_call(
        body, grid=(nt,),
        in_specs=[pl.BlockSpec(memory_space=pl.ANY),
                  pl.BlockSpec((1, 1, TOP_K * tm), lambda i: (i, 0, 0), memory_space=pltpu.SMEM),
                  pl.BlockSpec((1, 1, TOP_K * tm), lambda i: (jnp.minimum(i + 1, nt - 1), 0, 0),
                               memory_space=pltpu.SMEM),
                  pl.BlockSpec((tm, TOP_K), lambda i: (i, 0)), tile, tile, row, row],
        out_specs=[tile, tile],
        out_shape=[jax.ShapeDtypeStruct((t, d), F32), jax.ShapeDtypeStruct((t, d), BF16)],
        scratch_shapes=[pltpu.VMEM((2, TOP_K * tm, d), F32), pltpu.SemaphoreType.DMA((2,))],
        compiler_params=_cparams(("arbitrary",)), name="moe_combine_ln")(
            ys, dest, dest, wts, x, shared, g.reshape(1, d), b.reshape(1, d))


def moe_layer(x, xb, lw, *, route_tm, rows=MOE_ROWS, tok_tile=MOE_TOK_TILE):
    t, d = x.shape
    idx, wt, rank, counts = moe_route(xb, lw['moe_w_router'].T.astype(BF16), lw['moe_b_router'], tm=route_tm)
    counts = counts.astype(I32)
    padded = (counts + rows - 1) // rows * rows
    pad_end = jnp.cumsum(padded)
    pad_start = pad_end - padded
    dest = pad_start[idx] + rank
    n_rows = -(-(t * TOP_K + N_EXPERTS * (rows - 1)) // rows) * rows
    nb = n_rows // rows
    tok_id = jnp.broadcast_to(jnp.arange(t, dtype=I32)[None, :], (TOP_K, t))
    row_tok = jnp.zeros((n_rows,), I32).at[dest.reshape(-1)].set(tok_id.reshape(-1))
    block_e = jnp.minimum(jnp.searchsorted(pad_end, jnp.arange(nb, dtype=I32) * rows, side='right'),
                          N_EXPERTS - 1).astype(I32)
    n_used = (pad_end[-1] // rows).astype(I32).reshape(1)
    ys = moe_experts(x, row_tok.reshape(nb, 1, rows), block_e, n_used, lw['moe_w1'], lw['moe_w3'], lw['moe_w2'],
                     rows=rows)
    hs = swiglu_hidden(xb, lw['moe_ws1'], lw['moe_ws3'])
    shared = dense(hs, lw['moe_ws2'])
    dest_t = dest.reshape(TOP_K, t // tok_tile, tok_tile).transpose(1, 0, 2).reshape(t // tok_tile, 1, TOP_K * tok_tile)
    return moe_combine_ln(ys, dest_t, wt.T, x, shared, lw['ln2_g'], lw['ln2_b'], tm=tok_tile)


def rms_norm(x, w):
    return x * lax.rsqrt(jnp.mean(x * x, -1, keepdims=True) + NORM_EPS) * w


def l2_norm(x):
    return x * lax.rsqrt(jnp.sum(x * x, -1, keepdims=True) + NORM_EPS)


def split_cols(z, sizes):
    parts, off = [], 0
    for s in sizes:
        parts.append(z[..., off:off + s])
        off += s
    return parts


def to_chunks(a, c):
    b, l = a.shape[:2]
    n = -(-l // c)
    a = jnp.pad(a, [(0, 0), (0, n * c - l)] + [(0, 0)] * (a.ndim - 2))
    a = a.reshape((b, n, c) + a.shape[2:])
    return jnp.moveaxis(jnp.moveaxis(a, 1, 0), 2, 3)


def from_chunks(a, l):
    n, b, h, c = a.shape[:4]
    a = jnp.moveaxis(jnp.moveaxis(a, 3, 2), 0, 1)
    return a.reshape((b, n * c) + a.shape[3:])[:, :l]


def gla_recurrence(q, k, v, log_a, s0):
    l = q.shape[1]
    c = min(CHUNK, l)
    qc, kc, vc, gc = (to_chunks(t, c) for t in (q, k, v, log_a))
    incl = jnp.tril(jnp.ones((c, c), bool))[:, :, None]

    def step(s, blk):
        qb, kb, vb, gb = blk
        cum = jnp.cumsum(gb, axis=2)
        rel = cum[:, :, :, None, :] - cum[:, :, None, :, :]
        dec = jnp.where(incl, jnp.exp(jnp.where(incl, rel, 0.0)), 0.0)
        scores = jnp.einsum('bhtd,bhsd,bhtsd->bhts', qb, kb, dec)
        o = jnp.einsum('bhtd,bhde->bhte', qb * jnp.exp(cum), s) + jnp.einsum('bhts,bhse->bhte', scores, vb)
        last = cum[:, :, -1:, :]
        s = jnp.exp(last[:, :, 0, :, None]) * s + jnp.einsum('bhsd,bhse->bhde', kb * jnp.exp(last - cum), vb)
        return s, o

    s, o = lax.scan(step, s0, (qc, kc, vc, gc))
    return from_chunks(o, l), s


def gdn_recurrence(q, k, v, g, beta, s0):
    l = q.shape[1]
    c = min(CHUNK, l)
    qc, kc, vc, gc, bc = (to_chunks(t, c) for t in (q, k, v, g, beta))
    incl = jnp.tril(jnp.ones((c, c), bool))
    strict = jnp.tril(jnp.ones((c, c), bool), -1)
    eye = jnp.eye(c, dtype=jnp.float32)

    def step(s, blk):
        qb, kb, vb, gb, bb = blk
        cum = jnp.cumsum(gb, axis=-1)
        rel = cum[..., :, None] - cum[..., None, :]
        dec = jnp.where(incl, jnp.exp(jnp.where(incl, rel, 0.0)), 0.0)
        kbeta = kb * bb[..., None]
        a = jnp.where(strict, jnp.einsum('bhtd,bhsd->bhts', kbeta, kb) * dec, 0.0)
        t_inv = lax.linalg.triangular_solve(eye + a, jnp.broadcast_to(eye, a.shape), left_side=True, lower=True,
                                            unit_diagonal=True)
        w_dec = jnp.einsum('bhts,bhsd->bhtd', t_inv, kbeta * jnp.exp(cum)[..., None])
        u = jnp.einsum('bhts,bhse->bhte', t_inv, vb * bb[..., None]) - jnp.einsum('bhtd,bhde->bhte', w_dec, s)
        qk = jnp.einsum('bhtd,bhsd->bhts', qb, kb) * dec
        o = jnp.einsum('bhtd,bhde->bhte', qb * jnp.exp(cum)[..., None], s) + jnp.einsum('bhts,bhse->bhte', qk, u)
        last = cum[..., -1:]
        s = jnp.exp(last)[..., None] * s + jnp.einsum('bhsd,bhse->bhde', kb * jnp.exp(last - cum)[..., None], u)
        return s, o

    s, o = lax.scan(step, s0, (qc, kc, vc, gc, bc))
    return from_chunks(o, l), s


def short_conv(x, buf, w):
    l = x.shape[1]
    xp = jnp.concatenate([buf.astype(x.dtype), x], axis=1)
    acc = xp[:, 0:l] * w[0]
    for i in range(1, GDN_CONV):
        acc = acc + xp[:, i:i + l] * w[i]
    return jax.nn.silu(acc), xp[:, l:]


def gla_branch(z1, log_a, bsz, l, s0, lw):
    a_q, a_k, a_v, a_r = split_cols(z1.reshape(bsz, l, -1), IN_SIZES[0:4])
    q = a_q.reshape(bsz, l, GLA_HEADS, GLA_DK) * GLA_DK ** -0.5
    k = a_k.reshape(bsz, l, GLA_HEADS, GLA_DK)
    v = a_v.reshape(bsz, l, GLA_HEADS, GLA_DV)
    o_a, gla_new = gla_recurrence(q, k, v, log_a.reshape(bsz, l, GLA_HEADS, GLA_DK), s0)
    br = rms_norm(o_a, lw['gla_norm']).reshape(bsz, l, GLA_VW) * jax.nn.silu(a_r)
    return br.reshape(bsz * l, GLA_VW).astype(BF16), gla_new


def gdn_branch(zc, z4, bsz, l, s0, conv_buf, lw):
    zc = zc.reshape(bsz, l, -1)
    c_qkv, c_z = zc[..., :2 * GDN_KW + GDN_VW], zc[..., 2 * GDN_KW + GDN_VW:]
    c_a, c_b = split_cols(z4.reshape(bsz, l, -1), IN_SIZES[10:12])
    qkv, conv_new = short_conv(c_qkv, conv_buf, lw['gdn_conv_w'])
    gq, gk, gv = split_cols(qkv, (GDN_KW, GDN_KW, GDN_VW))
    gq = l2_norm(gq.reshape(bsz, l, GDN_HEADS, GDN_DK)) * GDN_DK ** -0.5
    gk = l2_norm(gk.reshape(bsz, l, GDN_HEADS, GDN_DK))
    gv = gv.reshape(bsz, l, GDN_HEADS, GDN_DV)
    g = -jnp.exp(lw['gdn_a_log']) * jax.nn.softplus(c_a + lw['gdn_dt_bias'])
    beta = jax.nn.sigmoid(c_b)
    o_c, gdn_new = gdn_recurrence(gq, gk, gv, g, beta, s0)
    zg = jax.nn.silu(c_z.reshape(bsz, l, GDN_HEADS, GDN_DV))
    br = (rms_norm(o_c, lw['gdn_norm']) * zg).reshape(bsz * l, GDN_VW)
    return br.astype(BF16), gdn_new, conv_new


def trunk_layer(x, xb, pe_b, states, lw, *, n_seq, seq_len, route_tm):
    t_p = n_seq * seq_len
    n_s = x.shape[0] - t_p
    gla_s, s5_re, s5_im, gdn_s, conv_s = states
    w_in = lw['w_in']
    z1 = dense(xb, w_in[:, 0:3072].astype(BF16))
    a_lr = dense(xb, w_in[:, 3072:3088].astype(BF16))
    z3 = dense(xb, w_in[:, 3088:8208].astype(BF16))
    z4 = dense(xb, w_in[:, 8208:8224].astype(BF16))
    log_a = dense(a_lr, lw['gla_w_gate'], bias=lw['gla_b_gate'], act='log_decay')
    zero_gla = jnp.zeros((n_seq, GLA_HEADS, GLA_DK, GLA_DV), F32)
    br_a_p, gla_p = gla_branch(z1[:t_p], log_a[:t_p], n_seq, seq_len, zero_gla, lw)
    br_a_s, gla_n = gla_branch(z1[t_p:], log_a[t_p:], n_s, 1, gla_s, lw)
    br_a = jnp.concatenate([br_a_p, br_a_s], axis=0)
    tabs = s5_tables(lw['s5_lam_re'], lw['s5_lam_im'], lw['s5_log_dt'], lw['s5_b_re'], lw['s5_b_im'],
                     lw['s5_c_re'], lw['s5_c_im'])
    y_s, s5r_p, s5i_p, s5r_n, s5i_n = s5_branch(z3[:, :S5_WIDTH], n_seq, seq_len, s5_re, s5_im, tabs, lw['s5_d'])
    br_s = glu_gate(y_s, lw['s5_w_glu'].astype(BF16), lw['s5_b_glu'])
    zero_gdn = jnp.zeros((n_seq, GDN_HEADS, GDN_DK, GDN_DV), F32)
    zero_conv = jnp.zeros((n_seq, GDN_CONV - 1, 2 * GDN_KW + GDN_VW), F32)
    br_c_p, gdn_p, conv_p = gdn_branch(z3[:t_p, S5_WIDTH:], z4[:t_p], n_seq, seq_len, zero_gdn, zero_conv, lw)
    br_c_s, gdn_n, conv_n = gdn_branch(z3[t_p:, S5_WIDTH:], z4[t_p:], n_s, 1, gdn_s, conv_s, lw)
    br_c = jnp.concatenate([br_c_p, br_c_s], axis=0)
    merged = merge_branches(xb, w_in[:, 8224:].astype(BF16), br_a, br_s, br_c, lw['w_branch_a'].astype(BF16),
                            lw['w_branch_s'].astype(BF16), lw['w_branch_c'].astype(BF16))
    x1, x1b = out_proj_ln(merged, lw['w_out'].astype(BF16), x, lw['ln1_g'], lw['ln1_b'])
    x2, x2b = moe_layer(x1, x1b, lw, route_tm=route_tm)
    x3, x3b = ple_mix(x2, x2b, lw['ple_w_gate'].astype(BF16), pe_b, lw['ple_w_proj'].astype(BF16))
    return x3, x3b, (gla_p, s5r_p, s5i_p, gdn_p, conv_p), (gla_n, s5r_n, s5i_n, gdn_n, conv_n)


_NAMES = ('w_in', 'gla_w_gate', 'gla_b_gate', 'gla_norm', 's5_lam_re', 's5_lam_im', 's5_log_dt', 's5_b_re',
          's5_b_im', 's5_c_re', 's5_c_im', 's5_d', 's5_w_glu', 's5_b_glu', 'gdn_conv_w', 'gdn_a_log',
          'gdn_dt_bias', 'gdn_norm', 'w_branch_a', 'w_branch_s', 'w_branch_c', 'w_out', 'ln1_g', 'ln1_b',
          'ln2_g', 'ln2_b', 'moe_w_router', 'moe_b_router', 'moe_w1', 'moe_w3', 'moe_w2', 'moe_ws1', 'moe_ws3',
          'moe_ws2', 'ple_w_proj', 'ple_w_gate')


def run_trunk(x_prompt, x_sample, p_prompt, p_sample, states, weights, *, route_tm):
    n_seq, seq_len, d = x_prompt.shape
    n_s = x_sample.shape[0]
    t_p = n_seq * seq_len
    depth = weights[0].shape[0]
    x = jnp.concatenate([x_prompt.reshape(t_p, d), x_sample.reshape(n_s, d)], axis=0)
    xb = x.astype(BF16)
    pe = jnp.concatenate([p_prompt.reshape(depth, t_p, -1), p_sample.reshape(depth, n_s, -1)], axis=1).astype(BF16)
    new_p, new_s = [], []
    for i in range(depth):
        lw = {n: w[i] for n, w in zip(_NAMES, weights)}
        x, xb, st_p, st_s = trunk_layer(x, xb, pe[i], tuple(s[i] for s in states), lw, n_seq=n_seq,
                                        seq_len=seq_len, route_tm=route_tm)
        new_p.append(st_p)
        new_s.append(st_s)
    gla_p, s5r_p, s5i_p, gdn_p, conv_p = (jnp.stack(f) for f in zip(*new_p))
    gla_s, s5r_s, s5i_s, gdn_s, conv_s = (jnp.stack(f) for f in zip(*new_s))
    yp = x[:t_p].reshape(n_seq, seq_len, d)
    ys = x[t_p:].reshape(n_s, 1, d)
    return (yp, ys, gla_p, gla_s, s5r_p, s5r_s, s5i_p, s5i_s, gdn_p, gdn_s, conv_p, conv_s)


def kernel(x_prompt, x_sample, p_prompt, p_sample, state_gla, state_s5_re, state_s5_im, state_gdn, state_gdn_conv,
           w_in, gla_w_gate, gla_b_gate, gla_norm, s5_lam_re, s5_lam_im, s5_log_dt, s5_b_re, s5_b_im, s5_c_re,
           s5_c_im, s5_d, s5_w_glu, s5_b_glu, gdn_conv_w, gdn_a_log, gdn_dt_bias, gdn_norm, w_branch_a,
           w_branch_s, w_branch_c, w_out, ln1_g, ln1_b, ln2_g, ln2_b, moe_w_router, moe_b_router, moe_w1, moe_w3,
           moe_w2, moe_ws1, moe_ws3, moe_ws2, ple_w_proj, ple_w_gate):
    weights = (w_in, gla_w_gate, gla_b_gate, gla_norm, s5_lam_re, s5_lam_im, s5_log_dt, s5_b_re, s5_b_im, s5_c_re,
               s5_c_im, s5_d, s5_w_glu, s5_b_glu, gdn_conv_w, gdn_a_log, gdn_dt_bias, gdn_norm, w_branch_a,
               w_branch_s, w_branch_c, w_out, ln1_g, ln1_b, ln2_g, ln2_b, moe_w_router, moe_b_router, moe_w1,
               moe_w3, moe_w2, moe_ws1, moe_ws3, moe_ws2, ple_w_proj, ple_w_gate)
    states = (state_gla, state_s5_re, state_s5_im, state_gdn, state_gdn_conv)
    return run_trunk(x_prompt, x_sample, p_prompt, p_sample, states, weights, route_tm=640)
```

```python
import functools
import math

import jax
import jax.numpy as jnp
from jax import lax
from jax.experimental import pallas as pl
from jax.experimental.pallas import tpu as pltpu

F32 = jnp.float32
BF16 = jnp.bfloat16
I32 = jnp.int32

D_MODEL = 2048
DEPTH = 4
GLA_HEADS, GLA_DK, GLA_DV = 4, 128, 256
GLA_KW, GLA_VW, GLA_RANK, GLA_TAU = 512, 1024, 16, 16.0
S5_WIDTH, S5_CH, S5_GROUPS, S5_STATE = 1024, 16, 64, 64
GDN_HEADS, GDN_DK, GDN_DV = 8, 128, 128
GDN_KW, GDN_VW, GDN_CONV = 1024, 1024, 4
N_BRANCH = 3
IN_SIZES = (GLA_KW, GLA_KW, GLA_VW, GLA_VW, GLA_RANK, S5_WIDTH, GDN_KW, GDN_KW, GDN_VW, GDN_VW, GDN_HEADS,
            GDN_HEADS, N_BRANCH * D_MODEL)
CHUNK = 64
N_EXPERTS, TOP_K, N_GROUPS, TOPK_GROUPS = 64, 8, 8, 4
D_EXPERT = 512
ROUTED_SCALE = 2.5
LN_EPS = 1e-5
NORM_EPS = 1e-6
DEEPNORM_ALPHA = (2 * DEPTH) ** 0.25

LANES = 128
VMEM_LIMIT_BYTES = 56 * 1024 * 1024
S5_CS = 16
S5_TILES = S5_WIDTH // LANES
S5_TSTATE = (LANES // S5_CH) * S5_STATE
MOE_ROWS = 256
MOE_TOK_TILE = 128


def _cparams(sem):
    return pltpu.CompilerParams(dimension_semantics=sem, vmem_limit_bytes=VMEM_LIMIT_BYTES)


def _pick_tile(n, candidates):
    for c in candidates:
        if n % c == 0:
            return c
    return n


def _sigmoid(x):
    return 1.0 / (1.0 + jnp.exp(-x))


def _silu(x):
    return x * _sigmoid(x)


def _gelu_tanh(x):
    return 0.5 * x * (1.0 + jnp.tanh(math.sqrt(2.0 / math.pi) * (x + 0.044715 * (x * x * x))))


def _log_sigmoid(x):
    return jnp.minimum(x, 0.0) - jnp.log1p(jnp.exp(-jnp.abs(x)))


def _bdot(a, b):
    return jnp.dot(a.astype(BF16), b.astype(BF16), preferred_element_type=F32)


def dense(x, w, *, bias=None, act=None, out_dtype=F32, tm=None, tn=None):
    m, k = x.shape
    n = w.shape[1]
    tm = tm or _pick_tile(m, (640, 512, 256, 128, 64, 32, 16, 8))
    tn = tn or _pick_tile(n, (512, 256, 128))

    def body(x_ref, w_ref, *rest):
        o_ref = rest[-1]
        y = _bdot(x_ref[...], w_ref[...])
        if bias is not None:
            y = y + rest[0][...]
        if act == 'log_decay':
            y = _log_sigmoid(y) / GLA_TAU
        o_ref[...] = y.astype(o_ref.dtype)

    in_specs = [pl.BlockSpec((tm, k), lambda i, j: (i, 0)), pl.BlockSpec((k, tn), lambda i, j: (0, j))]
    args = [x, w]
    if bias is not None:
        in_specs.append(pl.BlockSpec((1, tn), lambda i, j: (0, j)))
        args.append(bias.reshape(1, n))
    return pl.pallas_call(
        body, grid=(m // tm, n // tn), in_specs=in_specs,
        out_specs=pl.BlockSpec((tm, tn), lambda i, j: (i, j)),
        out_shape=jax.ShapeDtypeStruct((m, n), out_dtype),
        compiler_params=_cparams(("parallel", "parallel")), name="dense")(*args)


def swiglu_hidden(x, w1, w3):
    m, k = x.shape
    n = w1.shape[1]
    tm = _pick_tile(m, (640, 512, 256, 128, 64, 32, 16, 8))
    tn = _pick_tile(n, (512, 256, 128))

    def body(x_ref, w1_ref, w3_ref, o_ref):
        xb = x_ref[...].astype(BF16)
        a = jnp.dot(xb, w1_ref[...].astype(BF16), preferred_element_type=F32)
        b = jnp.dot(xb, w3_ref[...].astype(BF16), preferred_element_type=F32)
        o_ref[...] = (_silu(a) * b).astype(o_ref.dtype)

    return pl.pallas_call(
        body, grid=(m // tm, n // tn),
        in_specs=[pl.BlockSpec((tm, k), lambda i, j: (i, 0)), pl.BlockSpec((k, tn), lambda i, j: (0, j)),
                  pl.BlockSpec((k, tn), lambda i, j: (0, j))],
        out_specs=pl.BlockSpec((tm, tn), lambda i, j: (i, j)),
        out_shape=jax.ShapeDtypeStruct((m, n), BF16),
        compiler_params=_cparams(("parallel", "parallel")), name="swiglu_hidden")(x, w1, w3)


def glu_gate(y, w, b):
    m, n = y.shape
    tm = _pick_tile(m, (640, 512, 256, 128, 64, 32, 16, 8))
    tn = _pick_tile(n, (512, 256, 128))

    def body(y_ref, yt_ref, w_ref, b_ref, o_ref):
        g = _bdot(y_ref[...], w_ref[...]) + b_ref[...]
        o_ref[...] = (yt_ref[...] * _sigmoid(g)).astype(o_ref.dtype)

    return pl.pallas_call(
        body, grid=(m // tm, n // tn),
        in_specs=[pl.BlockSpec((tm, n), lambda i, j: (i, 0)), pl.BlockSpec((tm, tn), lambda i, j: (i, j)),
                  pl.BlockSpec((n, tn), lambda i, j: (0, j)), pl.BlockSpec((1, tn), lambda i, j: (0, j))],
        out_specs=pl.BlockSpec((tm, tn), lambda i, j: (i, j)),
        out_shape=jax.ShapeDtypeStruct((m, n), BF16),
        compiler_params=_cparams(("parallel", "parallel")), name="glu_gate")(y, y, w, b.reshape(1, n))


def merge_branches(xb, w_gates, br_a, br_s, br_c, w_a, w_s, w_c):
    m, k = xb.shape
    d = w_a.shape[1]
    kb = br_a.shape[1]
    tm = _pick_tile(m, (832, 640, 512, 256, 128, 64, 32, 16, 8))
    tn = _pick_tile(d, (256, 128))
    nj = d // tn

    def body(x_ref, g0_ref, g1_ref, g2_ref, a_ref, s_ref, c_ref, wa_ref, ws_ref, wc_ref, o_ref):
        x = x_ref[...]
        acc = _sigmoid(_bdot(x, g0_ref[...])) * _bdot(a_ref[...], wa_ref[...])
        acc = acc + _sigmoid(_bdot(x, g1_ref[...])) * _bdot(s_ref[...], ws_ref[...])
        acc = acc + _sigmoid(_bdot(x, g2_ref[...])) * _bdot(c_ref[...], wc_ref[...])
        o_ref[...] = acc.astype(o_ref.dtype)

    def gate_spec(b):
        return pl.BlockSpec((k, tn), lambda i, j: (0, b * nj + j))

    act_spec = pl.BlockSpec((tm, kb), lambda i, j: (i, 0))
    w_spec = pl.BlockSpec((kb, tn), lambda i, j: (0, j))
    return pl.pallas_call(
        body, grid=(m // tm, nj),
        in_specs=[pl.BlockSpec((tm, k), lambda i, j: (i, 0)), gate_spec(0), gate_spec(1), gate_spec(2),
                  act_spec, act_spec, act_spec, w_spec, w_spec, w_spec],
        out_specs=pl.BlockSpec((tm, tn), lambda i, j: (i, j)),
        out_shape=jax.ShapeDtypeStruct((m, d), BF16),
        compiler_params=_cparams(("parallel", "parallel")), name="merge_branches")(
            xb, w_gates, w_gates, w_gates, br_a, br_s, br_c, w_a, w_s, w_c)


def _layer_norm_rows(y, g, b):
    mu = jnp.mean(y, axis=-1, keepdims=True)
    yc = y - mu
    var = jnp.mean(yc * yc, axis=-1, keepdims=True)
    return yc * lax.rsqrt(var + LN_EPS) * g + b


def out_proj_ln(merged, w_out, x, g, b):
    m, k = merged.shape
    d = w_out.shape[1]
    tm = _pick_tile(m, (416, 256, 128, 64, 32, 16))

    def body(m_ref, w_ref, x_ref, g_ref, b_ref, o_ref, ob_ref):
        y = DEEPNORM_ALPHA * x_ref[...] + _bdot(m_ref[...], w_ref[...])
        y = _layer_norm_rows(y, g_ref[...], b_ref[...])
        o_ref[...] = y
        ob_ref[...] = y.astype(BF16)

    row = pl.BlockSpec((1, d), lambda i: (0, 0))
    return pl.pallas_call(
        body, grid=(m // tm,),
        in_specs=[pl.BlockSpec((tm, k), lambda i: (i, 0)), pl.BlockSpec((k, d), lambda i: (0, 0)),
                  pl.BlockSpec((tm, d), lambda i: (i, 0)), row, row],
        out_specs=[pl.BlockSpec((tm, d), lambda i: (i, 0)), pl.BlockSpec((tm, d), lambda i: (i, 0))],
        out_shape=[jax.ShapeDtypeStruct((m, d), F32), jax.ShapeDtypeStruct((m, d), BF16)],
        compiler_params=_cparams(("parallel",)), name="out_proj_ln")(
            merged, w_out, x, g.reshape(1, d), b.reshape(1, d))


def ple_mix(x, xb, w_gate, pe, w_proj):
    m, d = x.shape
    kp = pe.shape[1]
    tm = _pick_tile(m, (640, 512, 256, 128, 64, 32, 16))
    tn = _pick_tile(d, (512, 256, 128))

    def body(xb_ref, wg_ref, pe_ref, wp_ref, x_ref, o_ref, ob_ref):
        y = x_ref[...] + _sigmoid(_bdot(xb_ref[...], wg_ref[...])) * _bdot(pe_ref[...], wp_ref[...])
        o_ref[...] = y
        ob_ref[...] = y.astype(BF16)

    return pl.pallas_call(
        body, grid=(m // tm, d // tn),
        in_specs=[pl.BlockSpec((tm, d), lambda i, j: (i, 0)), pl.BlockSpec((d, tn), lambda i, j: (0, j)),
                  pl.BlockSpec((tm, kp), lambda i, j: (i, 0)), pl.BlockSpec((kp, tn), lambda i, j: (0, j)),
                  pl.BlockSpec((tm, tn), lambda i, j: (i, j))],
        out_specs=[pl.BlockSpec((tm, tn), lambda i, j: (i, j)), pl.BlockSpec((tm, tn), lambda i, j: (i, j))],
        out_shape=[jax.ShapeDtypeStruct((m, d), F32), jax.ShapeDtypeStruct((m, d), BF16)],
        compiler_params=_cparams(("parallel", "parallel")), name="ple_mix")(xb, w_gate, pe, w_proj, x)


def s5_tables(lam_re, lam_im, log_dt, b_re, b_im, c_re, c_im):
    hp = lax.Precision.HIGHEST
    cs, nt, gl = S5_CS, S5_TILES, LANES // S5_CH
    dt = jnp.exp(log_dt)[:, None]
    mag = jnp.exp(lam_re * dt)
    ab_re, ab_im = mag * jnp.cos(lam_im * dt), mag * jnp.sin(lam_im * dt)
    den = lam_re * lam_re + lam_im * lam_im
    nr = ab_re - 1.0
    co_re = (nr * lam_re + ab_im * lam_im) / den
    co_im = (ab_im * lam_re - nr * lam_im) / den
    bb_re = co_re[..., None] * b_re - co_im[..., None] * b_im
    bb_im = co_re[..., None] * b_im + co_im[..., None] * b_re
    pr, pi = [jnp.ones_like(ab_re)], [jnp.zeros_like(ab_im)]
    for _ in range(cs):
        pr.append(pr[-1] * ab_re - pi[-1] * ab_im)
        pi.append(pr[-2] * ab_im + pi[-1] * ab_re)
    ap_re, ap_im = jnp.stack(pr), jnp.stack(pi)
    abr = ap_re[:, :, :, None] * bb_re - ap_im[:, :, :, None] * bb_im
    abi = ap_re[:, :, :, None] * bb_im + ap_im[:, :, :, None] * bb_re
    kern = (jnp.einsum('gcp,egpd->egcd', c_re, abr[:cs], precision=hp)
            - jnp.einsum('gcp,egpd->egcd', c_im, abi[:cs], precision=hp))
    eye = jnp.eye(gl, dtype=F32)
    lag = jnp.arange(cs)[None, :] - jnp.arange(cs)[:, None]
    ksel = jnp.where((lag >= 0)[:, :, None, None, None], kern[jnp.clip(lag, 0, cs - 1)], 0.0)
    ksel = ksel.reshape(cs, cs, nt, gl, S5_CH, S5_CH)
    toep = jnp.einsum('stjgcd,gh->jsgdthc', ksel, eye).reshape(nt, cs * LANES, cs * LANES)
    er = abr[:cs][::-1].reshape(cs, nt, gl, S5_STATE, S5_CH)
    ei = abi[:cs][::-1].reshape(cs, nt, gl, S5_STATE, S5_CH)
    bend = jnp.stack([jnp.einsum('sjgpd,gh->jsgdhp', er, eye), jnp.einsum('sjgpd,gh->jsgdhp', ei, eye)], axis=4)
    bend = bend.reshape(nt, cs * LANES, 2 * S5_TSTATE)
    car = c_re[None] * ap_re[:, :, None, :] - c_im[None] * ap_im[:, :, None, :]
    cai = -(c_re[None] * ap_im[:, :, None, :] + c_im[None] * ap_re[:, :, None, :])
    car = car.reshape(cs + 1, nt, gl, S5_CH, S5_STATE)
    cai = cai.reshape(cs + 1, nt, gl, S5_CH, S5_STATE)
    ccar = jnp.stack([jnp.einsum('ejgcp,gh->jgpehc', car, eye), jnp.einsum('ejgcp,gh->jgpehc', cai, eye)], axis=1)
    ccar = ccar.reshape(nt, 2 * S5_TSTATE, (cs + 1) * LANES)

    def state_row(re, im):
        return jnp.concatenate([re.reshape(nt, 1, S5_TSTATE), im.reshape(nt, 1, S5_TSTATE)], axis=-1)

    return dict(toep=toep.astype(BF16), bend=bend.astype(BF16), bbar=bend[:, (cs - 1) * LANES:].astype(BF16),
                c0=ccar[:, :, :LANES].astype(BF16), ccar=ccar[:, :, LANES:].astype(BF16),
                a1=state_row(ap_re[1], ap_im[1]), acs=state_row(ap_re[cs], ap_im[cs]))


def s5_chunk_states(uf, bend):
    nt, nc, kc = uf.shape
    n = bend.shape[2]
    tn = 512

    def body(u_ref, b_ref, o_ref):
        o_ref[0] = _bdot(u_ref[0], b_ref[0])

    return pl.pallas_call(
        body, grid=(nt, n // tn),
        in_specs=[pl.BlockSpec((1, nc, kc), lambda j, n_: (j, 0, 0)),
                  pl.BlockSpec((1, kc, tn), lambda j, n_: (j, 0, n_))],
        out_specs=pl.BlockSpec((1, nc, tn), lambda j, n_: (j, 0, n_)),
        out_shape=jax.ShapeDtypeStruct((nt, nc, n), F32),
        compiler_params=_cparams(("parallel", "parallel")), name="s5_chunk_states")(uf, bend)


def s5_carry_scan(xe, acs, n_seq):
    nt, nc, n = xe.shape
    per = nc // n_seq
    half = n // 2

    def body(x_ref, a_ref, hp_ref, hf_ref):
        ar = a_ref[0, :, :half]
        ai = a_ref[0, :, half:]

        def step(k, carry):
            hr, hi = carry
            hp_ref[0, pl.ds(k, 1), :] = jnp.concatenate([hr, hi], axis=1)
            x = x_ref[0, pl.ds(k, 1), :]
            return (ar * hr - ai * hi + x[:, :half], ar * hi + ai * hr + x[:, half:])

        zero = jnp.zeros((1, half), F32)
        hr, hi = lax.fori_loop(0, per, step, (zero, zero))
        hf_ref[0, 0] = jnp.concatenate([hr, hi], axis=1)

    return pl.pallas_call(
        body, grid=(nt, n_seq),
        in_specs=[pl.BlockSpec((1, per, n), lambda j, b: (j, b, 0)), pl.BlockSpec((1, 1, n), lambda j, b: (j, 0, 0))],
        out_specs=[pl.BlockSpec((1, per, n), lambda j, b: (j, b, 0)),
                   pl.BlockSpec((1, 1, 1, n), lambda j, b: (j, b, 0, 0))],
        out_shape=[jax.ShapeDtypeStruct((nt, nc, n), F32), jax.ShapeDtypeStruct((nt, n_seq, 1, n), F32)],
        compiler_params=_cparams(("parallel", "parallel")), name="s5_carry_scan")(xe, acs)


def s5_outputs(uf, toep, hprev, ccar, d_flat):
    nt, nc, kc = uf.shape
    ns = hprev.shape[2]
    tn = 512

    def body(u_ref, ut_ref, t_ref, h_ref, c_ref, d_ref, o_ref):
        y = _bdot(u_ref[0], t_ref[0]) + _bdot(h_ref[0], c_ref[0]) + d_ref[0] * ut_ref[0]
        o_ref[0] = _gelu_tanh(y)

    return pl.pallas_call(
        body, grid=(nt, kc // tn),
        in_specs=[pl.BlockSpec((1, nc, kc), lambda j, n_: (j, 0, 0)), pl.BlockSpec((1, nc, tn), lambda j, n_: (j, 0, n_)),
                  pl.BlockSpec((1, kc, tn), lambda j, n_: (j, 0, n_)), pl.BlockSpec((1, nc, ns), lambda j, n_: (j, 0, 0)),
                  pl.BlockSpec((1, ns, tn), lambda j, n_: (j, 0, n_)), pl.BlockSpec((1, 1, tn), lambda j, n_: (j, 0, n_))],
        out_specs=pl.BlockSpec((1, nc, tn), lambda j, n_: (j, 0, n_)),
        out_shape=jax.ShapeDtypeStruct((nt, nc, kc), F32),
        compiler_params=_cparams(("parallel", "parallel")), name="s5_outputs")(uf, uf, toep, hprev, ccar, d_flat)


def s5_decode(u, h_re, h_im, bbar, a1, c0, d_skip):
    s, w = u.shape
    nt = w // LANES
    ts = S5_TSTATE

    def body(u_ref, hr_ref, hi_ref, b_ref, a_ref, c_ref, d_ref, y_ref, nr_ref, ni_ref):
        uu = u_ref[...]
        x = _bdot(uu, b_ref[0])
        ar, ai = a_ref[0, :, :ts], a_ref[0, :, ts:]
        hr, hi = hr_ref[...], hi_ref[...]
        nr = ar * hr - ai * hi + x[:, :ts]
        ni = ar * hi + ai * hr + x[:, ts:]
        nr_ref[...] = nr
        ni_ref[...] = ni
        y = _bdot(jnp.concatenate([nr, ni], axis=1), c_ref[0]) + d_ref[...] * uu
        y_ref[...] = _gelu_tanh(y)

    col = pl.BlockSpec((s, LANES), lambda j: (0, j))
    st = pl.BlockSpec((s, ts), lambda j: (0, j))
    return pl.pallas_call(
        body, grid=(nt,),
        in_specs=[col, st, st, pl.BlockSpec((1, LANES, 2 * ts), lambda j: (j, 0, 0)),
                  pl.BlockSpec((1, 1, 2 * ts), lambda j: (j, 0, 0)), pl.BlockSpec((1, 2 * ts, LANES), lambda j: (j, 0, 0)),
                  pl.BlockSpec((1, LANES), lambda j: (0, j))],
        out_specs=[col, st, st],
        out_shape=[jax.ShapeDtypeStruct((s, w), F32), jax.ShapeDtypeStruct(h_re.shape, F32),
                   jax.ShapeDtypeStruct(h_im.shape, F32)],
        compiler_params=_cparams(("parallel",)), name="s5_decode")(u, h_re, h_im, bbar, a1, c0, d_skip.reshape(1, w))


def s5_branch(u_all, n_seq, seq_len, h_re, h_im, tabs, d_skip):
    t_p = n_seq * seq_len
    nt, cs = S5_TILES, S5_CS
    nc = t_p // cs
    uf = u_all[:t_p].reshape(nc, cs, nt, LANES).transpose(2, 0, 1, 3).reshape(nt, nc, cs * LANES)
    xe = s5_chunk_states(uf, tabs['bend'])
    hprev, hfin = s5_carry_scan(xe, tabs['acs'], n_seq)
    d_flat = jnp.tile(d_skip.reshape(nt, 1, LANES), (1, 1, cs))
    yf = s5_outputs(uf, tabs['toep'], hprev, tabs['ccar'], d_flat)
    y_p = yf.reshape(nt, nc, cs, LANES).transpose(1, 2, 0, 3).reshape(t_p, nt * LANES)
    hfin = hfin.reshape(nt, n_seq, 2, S5_TSTATE).transpose(2, 1, 0, 3).reshape(2, n_seq, S5_GROUPS, S5_STATE)
    s_rows = u_all.shape[0] - t_p
    y_s, nr, ni = s5_decode(u_all[t_p:], h_re.reshape(s_rows, -1), h_im.reshape(s_rows, -1), tabs['bbar'],
                            tabs['a1'], tabs['c0'], d_skip)
    return (jnp.concatenate([y_p, y_s], axis=0), hfin[0], hfin[1], nr.reshape(h_re.shape), ni.reshape(h_im.shape))


def moe_route(xb, w_router_t, b_router, *, tm):
    t, d = xb.shape
    ne, ng, gs = N_EXPERTS, N_GROUPS, N_EXPERTS // N_GROUPS
    neg = -jnp.inf

    def body(x_ref, w_ref, b_ref, u_ref, idx_ref, wt_ref, rank_ref, cnt_ref, carry_ref):
        @pl.when(pl.program_id(0) == 0)
        def _():
            carry_ref[...] = jnp.zeros_like(carry_ref)

        logits = lax.dot_general(w_ref[...], x_ref[...], (((1,), (1,)), ((), ())), preferred_element_type=F32)
        scores = _sigmoid(logits).reshape(ng, gs, tm)
        choice = scores + b_ref[...].reshape(ng, gs, 1)
        e_in = lax.broadcasted_iota(I32, (ng, gs, tm), 1).astype(F32)
        g_id = lax.broadcasted_iota(I32, (ng, 1, tm), 0).astype(F32)
        e_id = g_id * gs + e_in
        m1 = jnp.max(choice, axis=1, keepdims=True)
        i1 = jnp.min(jnp.where(choice == m1, e_in, float(gs)), axis=1, keepdims=True)
        m2 = jnp.max(jnp.where(e_in == i1, neg, choice), axis=1, keepdims=True)
        gscore = m1 + m2
        keep = jnp.zeros((ng, 1, tm), F32)
        for _ in range(TOPK_GROUPS):
            gm = jnp.max(gscore, axis=0, keepdims=True)
            gi = jnp.min(jnp.where(gscore == gm, g_id, float(ng)), axis=0, keepdims=True)
            hit = g_id == gi
            keep = jnp.where(hit, 1.0, keep)
            gscore = jnp.where(hit, neg, gscore)
        cand = jnp.where(keep > 0.0, choice, neg)
        member = jnp.zeros((ng, gs, tm), F32)
        picks, wts = [], []
        for _ in range(TOP_K):
            cm = jnp.max(jnp.max(cand, axis=1, keepdims=True), axis=0, keepdims=True)
            ei = jnp.min(jnp.min(jnp.where(cand == cm, e_id, float(ne)), axis=1, keepdims=True), axis=0, keepdims=True)
            sel = e_id == ei
            wts.append(jnp.sum(jnp.sum(jnp.where(sel, scores, 0.0), axis=1, keepdims=True), axis=0, keepdims=True))
            picks.append(ei)
            member = jnp.where(sel, 1.0, member)
            cand = jnp.where(sel, neg, cand)
        wsum = wts[0]
        for w in wts[1:]:
            wsum = wsum + w
        member2 = member.reshape(ne, tm)
        prefix = jnp.dot(member2.astype(BF16), u_ref[...], preferred_element_type=F32) + carry_ref[:, 0:1]
        prefix = prefix.reshape(ng, gs, tm)
        for j in range(TOP_K):
            sel = e_id == picks[j]
            rk = jnp.sum(jnp.sum(jnp.where(sel, prefix, 0.0), axis=1, keepdims=True), axis=0, keepdims=True)
            idx_ref[j:j + 1, :] = picks[j].reshape(1, tm).astype(I32)
            rank_ref[j:j + 1, :] = rk.reshape(1, tm).astype(I32)
            wt_ref[j:j + 1, :] = (wts[j] / wsum * ROUTED_SCALE).reshape(1, tm)
        carry_ref[...] = carry_ref[...] + jnp.sum(member2, axis=1, keepdims=True)
        cnt_ref[...] = carry_ref[...]

    upper = jnp.triu(jnp.ones((tm, tm), F32), 1).astype(BF16)
    tok = pl.BlockSpec((TOP_K, tm), lambda i: (0, i))
    idx, wt, rank, cnt = pl.pallas_call(
        body, grid=(t // tm,),
        in_specs=[pl.BlockSpec((tm, d), lambda i: (i, 0)), pl.BlockSpec((ne, d), lambda i: (0, 0)),
                  pl.BlockSpec((ne, 1), lambda i: (0, 0)), pl.BlockSpec((tm, tm), lambda i: (0, 0))],
        out_specs=[tok, tok, tok, pl.BlockSpec((ne, LANES), lambda i: (0, 0))],
        out_shape=[jax.ShapeDtypeStruct((TOP_K, t), I32), jax.ShapeDtypeStruct((TOP_K, t), F32),
                   jax.ShapeDtypeStruct((TOP_K, t), I32), jax.ShapeDtypeStruct((ne, LANES), F32)],
        scratch_shapes=[pltpu.VMEM((ne, LANES), F32)],
        compiler_params=_cparams(("arbitrary",)), name="moe_route")(xb, w_router_t, b_router.reshape(ne, 1), upper)
    return idx, wt, rank, cnt[:, 0]


def moe_experts(x, row_tok, block_e, n_used, w1, w3, w2, *, rows):
    t, d = x.shape
    nb = row_tok.shape[0]
    f = w1.shape[2]

    def body(be_ref, nu_ref, x_hbm, tok_ref, tokn_ref, w1_ref, w3_ref, w2_ref, o_ref, buf, sem, w1b, w3b, w2b):
        i = pl.program_id(0)
        slot = lax.rem(i, 2)

        def gather(tref, sl):
            def issue(r, c):
                tok = tref[0, 0, r]
                pltpu.make_async_copy(x_hbm.at[pl.ds(tok, 1), :], buf.at[sl, pl.ds(r, 1), :], sem.at[sl]).start()
                return c
            lax.fori_loop(0, rows, issue, 0, unroll=8)

        @pl.when(i == 0)
        def _():
            gather(tok_ref, 0)

        @pl.when(i + 1 < nb)
        def _():
            gather(tokn_ref, 1 - slot)

        e = be_ref[i]
        changed = jnp.logical_or(i == 0, e != be_ref[jnp.maximum(i - 1, 0)])

        @pl.when(changed)
        def _():
            w1b[...] = w1_ref[0].astype(BF16)
            w3b[...] = w3_ref[0].astype(BF16)
            w2b[...] = w2_ref[0].astype(BF16)

        pltpu.make_async_copy(x_hbm.at[pl.ds(0, rows), :], buf.at[slot], sem.at[slot]).wait()

        @pl.when(i < nu_ref[0])
        def _():
            xb = buf[slot].astype(BF16)
            h = _silu(jnp.dot(xb, w1b[...], preferred_element_type=F32)) * jnp.dot(xb, w3b[...],
                                                                                preferred_element_type=F32)
            o_ref[...] = jnp.dot(h.astype(BF16), w2b[...], preferred_element_type=F32)

        @pl.when(i >= nu_ref[0])
        def _():
            o_ref[...] = jnp.zeros_like(o_ref)

    grid_spec = pltpu.PrefetchScalarGridSpec(
        num_scalar_prefetch=2, grid=(nb,),
        in_specs=[pl.BlockSpec(memory_space=pl.ANY),
                  pl.BlockSpec((1, 1, rows), lambda i, be, nu: (i, 0, 0), memory_space=pltpu.SMEM),
                  pl.BlockSpec((1, 1, rows), lambda i, be, nu: (jnp.minimum(i + 1, nb - 1), 0, 0),
                               memory_space=pltpu.SMEM),
                  pl.BlockSpec((1, d, f), lambda i, be, nu: (be[i], 0, 0)),
                  pl.BlockSpec((1, d, f), lambda i, be, nu: (be[i], 0, 0)),
                  pl.BlockSpec((1, f, d), lambda i, be, nu: (be[i], 0, 0))],
        out_specs=pl.BlockSpec((rows, d), lambda i, be, nu: (i, 0)),
        scratch_shapes=[pltpu.VMEM((2, rows, d), F32), pltpu.SemaphoreType.DMA((2,)),
                        pltpu.VMEM((d, f), BF16), pltpu.VMEM((d, f), BF16), pltpu.VMEM((f, d), BF16)])
    return pl.pallas_call(
        body, grid_spec=grid_spec, out_shape=jax.ShapeDtypeStruct((nb * rows, d), F32),
        compiler_params=_cparams(("arbitrary",)), name="moe_experts")(block_e, n_used, x, row_tok, row_tok, w1, w3, w2)


def moe_combine_ln(ys, dest, wts, x, shared, g, b, *, tm):
    t, d = x.shape
    nt = t // tm

    def body(ys_hbm, d_ref, dn_ref, w_ref, x_ref, s_ref, g_ref, b_ref, o_ref, ob_ref, buf, sem):
        i = pl.program_id(0)
        slot = lax.rem(i, 2)

        def gather(dref, sl):
            def issue(r, c):
                row = dref[0, 0, r]
                pltpu.make_async_copy(ys_hbm.at[pl.ds(row, 1), :], buf.at[sl, pl.ds(r, 1), :], sem.at[sl]).start()
                return c
            lax.fori_loop(0, TOP_K * tm, issue, 0, unroll=8)

        @pl.when(i == 0)
        def _():
            gather(d_ref, 0)

        @pl.when(i + 1 < nt)
        def _():
            gather(dn_ref, 1 - slot)

        pltpu.make_async_copy(ys_hbm.at[pl.ds(0, TOP_K * tm), :], buf.at[slot], sem.at[slot]).wait()
        w = w_ref[...]
        acc = DEEPNORM_ALPHA * x_ref[...] + s_ref[...]
        for j in range(TOP_K):
            acc = acc + w[:, j:j + 1] * buf[slot, j * tm:(j + 1) * tm, :]
        y = _layer_norm_rows(acc, g_ref[...], b_ref[...])
        o_ref[...] = y
        ob_ref[...] = y.astype(BF16)

    row = pl.BlockSpec((1, d), lambda i: (0, 0))
    tile = pl.BlockSpec((tm, d), lambda i: (i, 0))
    return pl.pallas_call(
        body, grid=(nt,),
        in_specs=[pl.BlockSpec(memory_space=pl.ANY),
                  pl.BlockSpec((1, 1, TOP_K * tm), lambda i: (i, 0, 0), memory_space=pltpu.SMEM),
                  pl.BlockSpec((1, 1, TOP_K * tm), lambda i: (jnp.minimum(i + 1, nt - 1), 0, 0),
                               memory_space=pltpu.SMEM),
                  pl.BlockSpec((tm, TOP_K), lambda i: (i, 0)), tile, tile, row, row],
        out_specs=[tile, tile],
        out_shape=[jax.ShapeDtypeStruct((t, d), F32), jax.ShapeDtypeStruct((t, d), BF16)],
        scratch_shapes=[pltpu.VMEM((2, TOP_K * tm, d), F32), pltpu.SemaphoreType.DMA((2,))],
        compiler_params=_cparams(("arbitrary",)), name="moe_combine_ln")(
            ys, dest, dest, wts, x, shared, g.reshape(1, d), b.reshape(1, d))


def moe_layer(x, xb, lw, *, route_tm, rows=MOE_ROWS, tok_tile=MOE_TOK_TILE):
    t, d = x.shape
    idx, wt, rank, counts = moe_route(xb, lw['moe_w_router'].T.astype(BF16), lw['moe_b_router'], tm=route_tm)
    counts = counts.astype(I32)
    padded = (counts + rows - 1) // rows * rows
    pad_end = jnp.cumsum(padded)
    pad_start = pad_end - padded
    onehot = idx[:, :, None] == jnp.arange(N_EXPERTS, dtype=I32)[None, None, :]
    dest = jnp.sum(jnp.where(onehot, pad_start[None, None, :], 0), axis=-1) + rank
    n_rows = -(-(t * TOP_K + N_EXPERTS * (rows - 1)) // rows) * rows
    nb = n_rows // rows
    tok_id = jnp.broadcast_to(jnp.arange(t, dtype=I32)[None, :], (TOP_K, t))
    row_tok = jnp.zeros((n_rows,), I32).at[dest.reshape(-1)].set(tok_id.reshape(-1))
    block_e = jnp.minimum(jnp.searchsorted(pad_end, jnp.arange(nb, dtype=I32) * rows, side='right'),
                          N_EXPERTS - 1).astype(I32)
    n_used = (pad_end[-1] // rows).astype(I32).reshape(1)
    ys = moe_experts(x, row_tok.reshape(nb, 1, rows), block_e, n_used, lw['moe_w1'], lw['moe_w3'], lw['moe_w2'],
                     rows=rows)
    hs = swiglu_hidden(xb, lw['moe_ws1'], lw['moe_ws3'])
    shared = dense(hs, lw['moe_ws2'])
    dest_t = dest.reshape(TOP_K, t // tok_tile, tok_tile).transpose(1, 0, 2).reshape(t // tok_tile, 1, TOP_K * tok_tile)
    return moe_combine_ln(ys, dest_t, wt.T, x, shared, lw['ln2_g'], lw['ln2_b'], tm=tok_tile)


def _level_tables(c):
    import numpy as np
    idx = np.arange(c)
    t, r = idx[:, None], idx[None, :]
    wl, pm = [], []
    b = 1
    while b < c:
        blk, odd = t // b, (t // b) % 2 == 1
        w = np.where(odd, (r >= blk * b) & (r <= t), (r > t) & (r <= blk * b + b - 1))
        wl.append(w.astype(np.float32))
        pm.append((odd & (r // b == blk - 1)).astype(np.float32))
        b *= 2
    incl = (r <= t).astype(np.float32)
    after = (r > t).astype(np.float32)
    return wl, pm, incl, after


def _split3(x):
    hi = x.astype(BF16)
    r1 = x - hi.astype(F32)
    mid = r1.astype(BF16)
    lo = (r1 - mid.astype(F32)).astype(BF16)
    return hi, mid, lo


def _table_dot(tab, x):
    hi, mid, lo = _split3(x)
    return (jnp.dot(tab, hi, preferred_element_type=F32) + jnp.dot(tab, mid, preferred_element_type=F32)
            + jnp.dot(tab, lo, preferred_element_type=F32))


def _dot_hi(a, b):
    ah = a.astype(BF16)
    al = (a - ah.astype(F32)).astype(BF16)
    bh = b.astype(BF16)
    bl = (b - bh.astype(F32)).astype(BF16)
    return (jnp.dot(ah, bh, preferred_element_type=F32) + jnp.dot(ah, bl, preferred_element_type=F32)
            + jnp.dot(al, bh, preferred_element_type=F32))


def _dot_nt(a, b):
    return lax.dot_general(a.astype(BF16), b.astype(BF16), (((1,), (1,)), ((), ())), preferred_element_type=F32)


def _dot_tn(a, b):
    return lax.dot_general(a.astype(BF16), b.astype(BF16), (((0,), (0,)), ((), ())), preferred_element_type=F32)


def _rms_rows(o, w):
    return o * lax.rsqrt(jnp.mean(o * o, axis=-1, keepdims=True) + NORM_EPS) * w


def gla_prompt(z1, log_a, norm_w, *, n_seq, seq_len):
    c, h_, dk, dv = CHUNK, GLA_HEADS, GLA_DK, GLA_DV
    nck = seq_len // c
    wl, pm, incl, after = _level_tables(c)
    nl = len(wl)
    wcat = jnp.asarray(jnp.concatenate([jnp.asarray(w) for w in wl] + [jnp.asarray(incl), jnp.asarray(after)], axis=0),
                       BF16)
    pmask = jnp.stack([jnp.eye(c, dtype=F32)] + [jnp.asarray(p) for p in pm])
    scale = dk ** -0.5

    def body(q_ref, k_ref, v_ref, r_ref, g_ref, w_ref, p_ref, n_ref, o_ref, st_ref, s_scr):
        ci = pl.program_id(1)

        @pl.when(ci == 0)
        def _():
            s_scr[...] = jnp.zeros_like(s_scr)

        x = _table_dot(w_ref[...], g_ref[...])
        ex = jnp.exp(x)
        for h in range(h_):
            ks = slice(h * dk, (h + 1) * dk)
            vs = slice(h * dv, (h + 1) * dv)
            q = q_ref[:, ks] * scale
            k = k_ref[:, ks]
            v = v_ref[:, vs]
            scores = p_ref[0] * _dot_nt(q, k)
            for l in range(nl):
                f = ex[l * c:(l + 1) * c, ks]
                scores = scores + p_ref[l + 1] * _dot_nt(q * f, k * f)
            st = s_scr[h]
            o = _dot_nt(q * ex[nl * c:(nl + 1) * c, ks], st) + _bdot(scores, v)
            tot = x[(nl + 1) * c - 1:(nl + 1) * c, ks]
            s_scr[h] = jnp.exp(tot) * st + _dot_tn(v, k * ex[(nl + 1) * c:(nl + 2) * c, ks])
            o_ref[:, vs] = (_rms_rows(o, n_ref[...]) * _silu(r_ref[:, vs])).astype(o_ref.dtype)

        @pl.when(ci == nck - 1)
        def _():
            st_ref[0] = s_scr[...]

    def rows(width, col):
        return pl.BlockSpec((c, width), lambda b, i: (b * nck + i, col))

    return pl.pallas_call(
        body, grid=(n_seq, nck),
        in_specs=[rows(h_ * dk, 0), rows(h_ * dk, 1), rows(h_ * dv, 1), rows(h_ * dv, 2), rows(h_ * dk, 0),
                  pl.BlockSpec(wcat.shape, lambda b, i: (0, 0)), pl.BlockSpec(pmask.shape, lambda b, i: (0, 0, 0)),
                  pl.BlockSpec((1, dv), lambda b, i: (0, 0))],
        out_specs=[rows(h_ * dv, 0), pl.BlockSpec((1, h_, dv, dk), lambda b, i: (b, 0, 0, 0))],
        out_shape=[jax.ShapeDtypeStruct((n_seq * seq_len, h_ * dv), BF16),
                   jax.ShapeDtypeStruct((n_seq, h_, dv, dk), F32)],
        scratch_shapes=[pltpu.VMEM((h_, dv, dk), F32)],
        compiler_params=_cparams(("parallel", "arbitrary")), name="gla_prompt")(
            z1, z1, z1, z1, log_a, wcat, pmask, norm_w.reshape(1, dv))


def _columns(rows_list, width):
    used = sum(r.shape[0] for r in rows_list)
    stack = jnp.concatenate(list(rows_list) + [jnp.zeros((LANES - used, width), F32)], axis=0)
    return stack.T


def gla_decode(z1, log_a, state, norm_w, *, row0):
    sb = 8
    n_s = state.shape[0]
    h_, dk, dv = GLA_HEADS, GLA_DK, GLA_DV
    scale = dk ** -0.5
    r0 = row0 // sb

    def body(q_ref, k_ref, v_ref, r_ref, g_ref, s_ref, n_ref, o_ref, ns_ref):
        for h in range(h_):
            ks = slice(h * dk, (h + 1) * dk)
            vs = slice(h * dv, (h + 1) * dv)
            cols = _columns([jnp.exp(g_ref[:, ks]), k_ref[:, ks], q_ref[:, ks] * scale], dk)
            v = v_ref[:, vs]
            outs = []
            for s in range(sb):
                s_new = cols[:, s:s + 1] * s_ref[s, h] + cols[:, sb + s:sb + s + 1] * v[s:s + 1, :]
                ns_ref[s, h] = s_new
                outs.append(jnp.sum(cols[:, 2 * sb + s:2 * sb + s + 1] * s_new, axis=0, keepdims=True))
            o = jnp.concatenate(outs, axis=0)
            o_ref[:, vs] = _rms_rows(o, n_ref[...]) * _silu(r_ref[:, vs])

    def rows(width, col):
        return pl.BlockSpec((sb, width), lambda i: (r0 + i, col))

    st = pl.BlockSpec((sb, h_, dk, dv), lambda i: (i, 0, 0, 0))
    return pl.pallas_call(
        body, grid=(n_s // sb,),
        in_specs=[rows(h_ * dk, 0), rows(h_ * dk, 1), rows(h_ * dv, 1), rows(h_ * dv, 2), rows(h_ * dk, 0), st,
                  pl.BlockSpec((1, dv), lambda i: (0, 0))],
        out_specs=[pl.BlockSpec((sb, h_ * dv), lambda i: (i, 0)), st],
        out_shape=[jax.ShapeDtypeStruct((n_s, h_ * dv), F32), jax.ShapeDtypeStruct(state.shape, F32)],
        compiler_params=_cparams(("parallel",)), name="gla_decode")(
            z1, z1, z1, z1, log_a, state, norm_w.reshape(1, dv))


def _conv_silu(ext, w, c):
    acc = ext[5:5 + c] * w[0:1]
    for i in range(1, GDN_CONV):
        acc = acc + ext[5 + i:5 + i + c] * w[i:i + 1]
    return _silu(acc)


def _softplus(x):
    return jnp.maximum(x, 0.0) + jnp.log1p(jnp.exp(-jnp.abs(x)))


def _l2n(x):
    return x * lax.rsqrt(jnp.sum(x * x, axis=-1, keepdims=True) + NORM_EPS)


def gdn_prompt(z3, z4, conv_w, a_log, dt_bias, norm_w, *, n_seq, seq_len):
    c, h_, dk, dv = CHUNK, GDN_HEADS, GDN_DK, GDN_DV
    kw = h_ * dk
    nck = seq_len // c
    _, pm, incl, after = _level_tables(c)
    nl = len(pm)
    import numpy as np
    strict = (np.arange(c)[:, None] > np.arange(c)[None, :]).astype(np.float32)
    tabs = jnp.asarray(np.concatenate([incl, after, np.ones((c, c), np.float32)], axis=0), BF16)
    masks = jnp.stack([jnp.asarray(incl), jnp.asarray(strict), jnp.eye(c, dtype=F32)] + [jnp.asarray(p) for p in pm])
    strict_pad = jnp.asarray(np.concatenate([strict, np.zeros((c, LANES - c), np.float32)], axis=1))
    qscale = dk ** -0.5

    def body(q_ref, k_ref, v_ref, zg_ref, ab_ref, cw_ref, al_ref, db_ref, t_ref, m_ref, sp_ref, n_ref,
             o_ref, st_ref, s_scr, hist):
        ci = pl.program_id(1)

        @pl.when(ci == 0)
        def _():
            s_scr[...] = jnp.zeros_like(s_scr)
            hist[...] = jnp.zeros_like(hist)

        def conv(ref, j):
            cols = slice(j * kw, (j + 1) * kw)
            raw = ref[...]
            ext = jnp.concatenate([hist[:, cols], raw], axis=0)
            y = _conv_silu(ext, cw_ref[:, cols], c)
            hist[:, cols] = raw[c - 8:c]
            return y

        qc, kc, vc = conv(q_ref, 0), conv(k_ref, 1), conv(v_ref, 2)
        ab = ab_ref[...]
        g = -jnp.exp(al_ref[...]) * _softplus(ab[:, :h_] + db_ref[...])
        beta = _sigmoid(ab[:, h_:])
        sums = _table_dot(t_ref[...], jnp.concatenate([g, jnp.zeros((c, LANES - h_), F32)], axis=1))
        e_cum = jnp.exp(sums[0:c])
        e_rest = jnp.exp(sums[c:2 * c])
        e_last = jnp.exp(sums[2 * c:2 * c + 1])
        grel = jnp.concatenate([g[:, h:h + 1] * sp_ref[...] for h in range(h_)], axis=1)
        rel = _table_dot(t_ref[0:c], grel)
        m_incl, m_strict, m_eye = m_ref[0], m_ref[1], m_ref[2]
        qs, ks, vs, kbs, decs, amat, tinv = [], [], [], [], [], [], []
        for h in range(h_):
            hs = slice(h * dk, (h + 1) * dk)
            qs.append(_l2n(qc[:, hs]) * qscale)
            ks.append(_l2n(kc[:, hs]))
            vs.append(vc[:, hs])
            decs.append(m_incl * jnp.exp(m_incl * rel[:, h * LANES:h * LANES + c]))
            kbs.append(ks[h] * beta[:, h:h + 1])
            amat.append(m_strict * _dot_nt(kbs[h], ks[h]) * decs[h])
            tinv.append(m_eye - m_ref[3] * amat[h])
        for l in range(1, nl):
            tinv = [tinv[h] - _dot_hi(_dot_hi(tinv[h], m_ref[3 + l] * amat[h]), tinv[h]) for h in range(h_)]
        for h in range(h_):
            hs = slice(h * dk, (h + 1) * dk)
            q, k, v, kb, dec, t = qs[h], ks[h], vs[h], kbs[h], decs[h], tinv[h]
            bcol = beta[:, h:h + 1]
            tw = _bdot(t, jnp.concatenate([kb * e_cum[:, h:h + 1], v * bcol], axis=1))
            s_old = s_scr[h]
            both = _bdot(jnp.concatenate([q * e_cum[:, h:h + 1], tw[:, :dk]], axis=0), s_old)
            u = tw[:, dk:] - both[c:]
            o = both[:c] + _bdot(_dot_nt(q, k) * dec, u)
            s_scr[h] = e_last[:, h:h + 1] * s_old + _dot_tn(k * e_rest[:, h:h + 1], u)
            o_ref[:, hs] = (_rms_rows(o, n_ref[...]) * _silu(zg_ref[:, hs])).astype(o_ref.dtype)

        @pl.when(ci == nck - 1)
        def _():
            st_ref[0] = s_scr[...]

    def rows(width, col):
        return pl.BlockSpec((c, width), lambda b, i: (b * nck + i, col))

    def const(arr):
        nd = arr.ndim
        return pl.BlockSpec(arr.shape, lambda b, i: (0,) * nd)

    cw = conv_w
    al = a_log.reshape(1, h_)
    db = dt_bias.reshape(1, h_)
    nw = norm_w.reshape(1, dv)
    return pl.pallas_call(
        body, grid=(n_seq, nck),
        in_specs=[rows(kw, 1), rows(kw, 2), rows(kw, 3), rows(kw, 4), pl.BlockSpec((c, 2 * h_), lambda b, i: (b * nck + i, 0)),
                  const(cw), const(al), const(db), const(tabs), const(masks), const(strict_pad), const(nw)],
        out_specs=[rows(kw, 0), pl.BlockSpec((1, h_, dk, dv), lambda b, i: (b, 0, 0, 0))],
        out_shape=[jax.ShapeDtypeStruct((n_seq * seq_len, kw), BF16), jax.ShapeDtypeStruct((n_seq, h_, dk, dv), F32)],
        scratch_shapes=[pltpu.VMEM((h_, dk, dv), F32), pltpu.VMEM((8, 3 * kw), F32)],
        compiler_params=_cparams(("parallel", "arbitrary")), name="gdn_prompt")(
            z3, z3, z3, z3, z4, cw, al, db, tabs, masks, strict_pad, nw)


def gdn_decode(z3, z4, state, conv_buf, conv_w, a_log, dt_bias, norm_w, *, row0):
    sb = 8
    n_s = state.shape[0]
    h_, dk, dv = GDN_HEADS, GDN_DK, GDN_DV
    kw = h_ * dk
    r0 = row0 // sb
    qscale = dk ** -0.5

    def body(q_ref, k_ref, v_ref, zg_ref, ab_ref, hb_ref, cw_ref, al_ref, db_ref, s_ref, n_ref, o_ref, ns_ref):
        def conv(ref, j):
            cols = slice(j * kw, (j + 1) * kw)
            acc = ref[...] * cw_ref[GDN_CONV - 1:GDN_CONV, cols]
            for i in range(GDN_CONV - 1):
                acc = acc + hb_ref[:, i, cols] * cw_ref[i:i + 1, cols]
            return _silu(acc)

        qc, kc, vc = conv(q_ref, 0), conv(k_ref, 1), conv(v_ref, 2)
        ab = ab_ref[...]
        eg = jnp.exp(-jnp.exp(al_ref[...]) * _softplus(ab[:, :h_] + db_ref[...]))
        beta = _sigmoid(ab[:, h_:])
        for h in range(h_):
            hs = slice(h * dk, (h + 1) * dk)
            q = _l2n(qc[:, hs]) * qscale
            k = _l2n(kc[:, hs])
            v = vc[:, hs]
            cols = _columns([k, q], dk)
            qk = jnp.sum(q * k, axis=-1, keepdims=True)
            outs = []
            for s in range(sb):
                s_old = s_ref[s, h]
                kcol = cols[:, s:s + 1]
                k_s = jnp.sum(kcol * s_old, axis=0, keepdims=True)
                q_s = jnp.sum(cols[:, sb + s:sb + s + 1] * s_old, axis=0, keepdims=True)
                e = eg[s:s + 1, h:h + 1]
                u = beta[s:s + 1, h:h + 1] * (v[s:s + 1, :] - e * k_s)
                ns_ref[s, h] = e * s_old + kcol * u
                outs.append(e * q_s + qk[s:s + 1, :] * u)
            o = jnp.concatenate(outs, axis=0)
            o_ref[:, hs] = _rms_rows(o, n_ref[...]) * _silu(zg_ref[:, hs])

    def rows(width, col):
        return pl.BlockSpec((sb, width), lambda i: (r0 + i, col))

    def const(arr):
        nd = arr.ndim
        return pl.BlockSpec(arr.shape, lambda i: (0,) * nd)

    st = pl.BlockSpec((sb, h_, dk, dv), lambda i: (i, 0, 0, 0))
    al = a_log.reshape(1, h_)
    db = dt_bias.reshape(1, h_)
    nw = norm_w.reshape(1, dv)
    return pl.pallas_call(
        body, grid=(n_s // sb,),
        in_specs=[rows(kw, 1), rows(kw, 2), rows(kw, 3), rows(kw, 4), pl.BlockSpec((sb, 2 * h_), lambda i: (r0 + i, 0)),
                  pl.BlockSpec((sb, GDN_CONV - 1, 3 * kw), lambda i: (i, 0, 0)), const(conv_w), const(al), const(db),
                  st, const(nw)],
        out_specs=[pl.BlockSpec((sb, kw), lambda i: (i, 0)), st],
        out_shape=[jax.ShapeDtypeStruct((n_s, kw), F32), jax.ShapeDtypeStruct(state.shape, F32)],
        compiler_params=_cparams(("parallel",)), name="gdn_decode")(
            z3, z3, z3, z3, z4, conv_buf, conv_w, al, db, state, nw)


def trunk_layer(x, xb, pe_b, states, lw, *, n_seq, seq_len, route_tm):
    t_p = n_seq * seq_len
    n_s = x.shape[0] - t_p
    gla_s, s5_re, s5_im, gdn_s, conv_s = states
    w_in = lw['w_in']
    z1 = dense(xb, w_in[:, 0:3072].astype(BF16))
    a_lr = dense(xb, w_in[:, 3072:3088].astype(BF16))
    z3 = dense(xb, w_in[:, 3088:8208].astype(BF16))
    z4 = dense(xb, w_in[:, 8208:8224].astype(BF16))
    log_a = dense(a_lr, lw['gla_w_gate'], bias=lw['gla_b_gate'], act='log_decay')
    br_a_p, gla_pt = gla_prompt(z1, log_a, lw['gla_norm'], n_seq=n_seq, seq_len=seq_len)
    gla_p = jnp.swapaxes(gla_pt, 2, 3)
    br_a_s, gla_n = gla_decode(z1, log_a, gla_s, lw['gla_norm'], row0=t_p)
    br_a = jnp.concatenate([br_a_p, br_a_s.astype(BF16)], axis=0)
    tabs = s5_tables(lw['s5_lam_re'], lw['s5_lam_im'], lw['s5_log_dt'], lw['s5_b_re'], lw['s5_b_im'],
                     lw['s5_c_re'], lw['s5_c_im'])
    y_s, s5r_p, s5i_p, s5r_n, s5i_n = s5_branch(z3[:, :S5_WIDTH], n_seq, seq_len, s5_re, s5_im, tabs, lw['s5_d'])
    br_s = glu_gate(y_s, lw['s5_w_glu'].astype(BF16), lw['s5_b_glu'])
    br_c_p, gdn_p = gdn_prompt(z3, z4, lw['gdn_conv_w'], lw['gdn_a_log'], lw['gdn_dt_bias'], lw['gdn_norm'],
                               n_seq=n_seq, seq_len=seq_len)
    br_c_s, gdn_n = gdn_decode(z3, z4, gdn_s, conv_s, lw['gdn_conv_w'], lw['gdn_a_log'], lw['gdn_dt_bias'],
                               lw['gdn_norm'], row0=t_p)
    br_c = jnp.concatenate([br_c_p, br_c_s.astype(BF16)], axis=0)
    raw = z3[:, S5_WIDTH:S5_WIDTH + 2 * GDN_KW + GDN_VW]
    conv_p = raw[:t_p].reshape(n_seq, seq_len, -1)[:, seq_len - (GDN_CONV - 1):]
    conv_n = jnp.concatenate([conv_s[:, 1:], raw[t_p:, None, :]], axis=1)
    merged = merge_branches(xb, w_in[:, 8224:].astype(BF16), br_a, br_s, br_c, lw['w_branch_a'].astype(BF16),
                            lw['w_branch_s'].astype(BF16), lw['w_branch_c'].astype(BF16))
    x1, x1b = out_proj_ln(merged, lw['w_out'].astype(BF16), x, lw['ln1_g'], lw['ln1_b'])
    x2, x2b = moe_layer(x1, x1b, lw, route_tm=route_tm)
    x3, x3b = ple_mix(x2, x2b, lw['ple_w_gate'].astype(BF16), pe_b, lw['ple_w_proj'].astype(BF16))
    return x3, x3b, (gla_p, s5r_p, s5i_p, gdn_p, conv_p), (gla_n, s5r_n, s5i_n, gdn_n, conv_n)


_NAMES = ('w_in', 'gla_w_gate', 'gla_b_gate', 'gla_norm', 's5_lam_re', 's5_lam_im', 's5_log_dt', 's5_b_re',
          's5_b_im', 's5_c_re', 's5_c_im', 's5_d', 's5_w_glu', 's5_b_glu', 'gdn_conv_w', 'gdn_a_log',
          'gdn_dt_bias', 'gdn_norm', 'w_branch_a', 'w_branch_s', 'w_branch_c', 'w_out', 'ln1_g', 'ln1_b',
          'ln2_g', 'ln2_b', 'moe_w_router', 'moe_b_router', 'moe_w1', 'moe_w3', 'moe_w2', 'moe_ws1', 'moe_ws3',
          'moe_ws2', 'ple_w_proj', 'ple_w_gate')


def run_trunk(x_prompt, x_sample, p_prompt, p_sample, states, weights, *, route_tm):
    n_seq, seq_len, d = x_prompt.shape
    n_s = x_sample.shape[0]
    t_p = n_seq * seq_len
    depth = weights[0].shape[0]
    x = jnp.concatenate([x_prompt.reshape(t_p, d), x_sample.reshape(n_s, d)], axis=0)
    xb = x.astype(BF16)
    pe = jnp.concatenate([p_prompt.reshape(depth, t_p, -1), p_sample.reshape(depth, n_s, -1)], axis=1).astype(BF16)
    new_p, new_s = [], []
    for i in range(depth):
        lw = {n: w[i] for n, w in zip(_NAMES, weights)}
        x, xb, st_p, st_s = trunk_layer(x, xb, pe[i], tuple(s[i] for s in states), lw, n_seq=n_seq,
                                        seq_len=seq_len, route_tm=route_tm)
        new_p.append(st_p)
        new_s.append(st_s)
    gla_p, s5r_p, s5i_p, gdn_p, conv_p = (jnp.stack(f) for f in zip(*new_p))
    gla_s, s5r_s, s5i_s, gdn_s, conv_s = (jnp.stack(f) for f in zip(*new_s))
    yp = x[:t_p].reshape(n_seq, seq_len, d)
    ys = x[t_p:].reshape(n_s, 1, d)
    return (yp, ys, gla_p, gla_s, s5r_p, s5r_s, s5i_p, s5i_s, gdn_p, gdn_s, conv_p, conv_s)


def kernel(x_prompt, x_sample, p_prompt, p_sample, state_gla, state_s5_re, state_s5_im, state_gdn, state_gdn_conv,
           w_in, gla_w_gate, gla_b_gate, gla_norm, s5_lam_re, s5_lam_im, s5_log_dt, s5_b_re, s5_b_im, s5_c_re,
           s5_c_im, s5_d, s5_w_glu, s5_b_glu, gdn_conv_w, gdn_a_log, gdn_dt_bias, gdn_norm, w_branch_a,
           w_branch_s, w_branch_c, w_out, ln1_g, ln1_b, ln2_g, ln2_b, moe_w_router, moe_b_router, moe_w1, moe_w3,
           moe_w2, moe_ws1, moe_ws3, moe_ws2, ple_w_proj, ple_w_gate):
    weights = (w_in, gla_w_gate, gla_b_gate, gla_norm, s5_lam_re, s5_lam_im, s5_log_dt, s5_b_re, s5_b_im, s5_c_re,
               s5_c_im, s5_d, s5_w_glu, s5_b_glu, gdn_conv_w, gdn_a_log, gdn_dt_bias, gdn_norm, w_branch_a,
               w_branch_s, w_branch_c, w_out, ln1_g, ln1_b, ln2_g, ln2_b, moe_w_router, moe_b_router, moe_w1,
               moe_w3, moe_w2, moe_ws1, moe_ws3, moe_ws2, ple_w_proj, ple_w_gate)
    states = (state_gla, state_s5_re, state_s5_im, state_gdn, state_gdn_conv)
    return run_trunk(x_prompt, x_sample, p_prompt, p_sample, states, weights, route_tm=640)
```

```python
import functools
import math

import jax
import jax.numpy as jnp
from jax import lax
from jax.experimental import pallas as pl
from jax.experimental.pallas import tpu as pltpu

F32 = jnp.float32
BF16 = jnp.bfloat16
I32 = jnp.int32

D_MODEL = 2048
DEPTH = 4
GLA_HEADS, GLA_DK, GLA_DV = 4, 128, 256
GLA_KW, GLA_VW, GLA_RANK, GLA_TAU = 512, 1024, 16, 16.0
S5_WIDTH, S5_CH, S5_GROUPS, S5_STATE = 1024, 16, 64, 64
GDN_HEADS, GDN_DK, GDN_DV = 8, 128, 128
GDN_KW, GDN_VW, GDN_CONV = 1024, 1024, 4
N_BRANCH = 3
IN_SIZES = (GLA_KW, GLA_KW, GLA_VW, GLA_VW, GLA_RANK, S5_WIDTH, GDN_KW, GDN_KW, GDN_VW, GDN_VW, GDN_HEADS,
            GDN_HEADS, N_BRANCH * D_MODEL)
CHUNK = 64
N_EXPERTS, TOP_K, N_GROUPS, TOPK_GROUPS = 64, 8, 8, 4
D_EXPERT = 512
ROUTED_SCALE = 2.5
LN_EPS = 1e-5
NORM_EPS = 1e-6
DEEPNORM_ALPHA = (2 * DEPTH) ** 0.25

LANES = 128
VMEM_LIMIT_BYTES = 56 * 1024 * 1024
S5_CS = 16
S5_TILES = S5_WIDTH // LANES
S5_TSTATE = (LANES // S5_CH) * S5_STATE
MOE_ROWS = 256
MOE_TOK_TILE = 128


def _cparams(sem):
    return pltpu.CompilerParams(dimension_semantics=sem, vmem_limit_bytes=VMEM_LIMIT_BYTES)


def _pick_tile(n, candidates):
    for c in candidates:
        if n % c == 0:
            return c
    return n


def _sigmoid(x):
    return 1.0 / (1.0 + jnp.exp(-x))


def _silu(x):
    return x * _sigmoid(x)


def _gelu_tanh(x):
    return 0.5 * x * (1.0 + jnp.tanh(math.sqrt(2.0 / math.pi) * (x + 0.044715 * (x * x * x))))


def _log_sigmoid(x):
    return jnp.minimum(x, 0.0) - jnp.log1p(jnp.exp(-jnp.abs(x)))


def _bdot(a, b):
    return jnp.dot(a.astype(BF16), b.astype(BF16), preferred_element_type=F32)


def dense(x, w, *, bias=None, act=None, out_dtype=F32, tm=None, tn=None):
    m, k = x.shape
    n = w.shape[1]
    tm = tm or _pick_tile(m, (640, 512, 256, 128, 64, 32, 16, 8))
    tn = tn or _pick_tile(n, (512, 256, 128))

    def body(x_ref, w_ref, *rest):
        o_ref = rest[-1]
        y = _bdot(x_ref[...], w_ref[...])
        if bias is not None:
            y = y + rest[0][...]
        if act == 'log_decay':
            y = _log_sigmoid(y) / GLA_TAU
        o_ref[...] = y.astype(o_ref.dtype)

    in_specs = [pl.BlockSpec((tm, k), lambda i, j: (i, 0)), pl.BlockSpec((k, tn), lambda i, j: (0, j))]
    args = [x, w]
    if bias is not None:
        in_specs.append(pl.BlockSpec((1, tn), lambda i, j: (0, j)))
        args.append(bias.reshape(1, n))
    return pl.pallas_call(
        body, grid=(m // tm, n // tn), in_specs=in_specs,
        out_specs=pl.BlockSpec((tm, tn), lambda i, j: (i, j)),
        out_shape=jax.ShapeDtypeStruct((m, n), out_dtype),
        compiler_params=_cparams(("parallel", "parallel")), name="dense")(*args)


def swiglu_hidden(x, w1, w3):
    m, k = x.shape
    n = w1.shape[1]
    tm = _pick_tile(m, (640, 512, 256, 128, 64, 32, 16, 8))
    tn = _pick_tile(n, (512, 256, 128))

    def body(x_ref, w1_ref, w3_ref, o_ref):
        xb = x_ref[...].astype(BF16)
        a = jnp.dot(xb, w1_ref[...].astype(BF16), preferred_element_type=F32)
        b = jnp.dot(xb, w3_ref[...].astype(BF16), preferred_element_type=F32)
        o_ref[...] = (_silu(a) * b).astype(o_ref.dtype)

    return pl.pallas_call(
        body, grid=(m // tm, n // tn),
        in_specs=[pl.BlockSpec((tm, k), lambda i, j: (i, 0)), pl.BlockSpec((k, tn), lambda i, j: (0, j)),
                  pl.BlockSpec((k, tn), lambda i, j: (0, j))],
        out_specs=pl.BlockSpec((tm, tn), lambda i, j: (i, j)),
        out_shape=jax.ShapeDtypeStruct((m, n), BF16),
        compiler_params=_cparams(("parallel", "parallel")), name="swiglu_hidden")(x, w1, w3)


def glu_gate(y, w, b):
    m, n = y.shape
    tm = _pick_tile(m, (640, 512, 256, 128, 64, 32, 16, 8))
    tn = _pick_tile(n, (512, 256, 128))

    def body(y_ref, yt_ref, w_ref, b_ref, o_ref):
        g = _bdot(y_ref[...], w_ref[...]) + b_ref[...]
        o_ref[...] = (yt_ref[...] * _sigmoid(g)).astype(o_ref.dtype)

    return pl.pallas_call(
        body, grid=(m // tm, n // tn),
        in_specs=[pl.BlockSpec((tm, n), lambda i, j: (i, 0)), pl.BlockSpec((tm, tn), lambda i, j: (i, j)),
                  pl.BlockSpec((n, tn), lambda i, j: (0, j)), pl.BlockSpec((1, tn), lambda i, j: (0, j))],
        out_specs=pl.BlockSpec((tm, tn), lambda i, j: (i, j)),
        out_shape=jax.ShapeDtypeStruct((m, n), BF16),
        compiler_params=_cparams(("parallel", "parallel")), name="glu_gate")(y, y, w, b.reshape(1, n))


def merge_branches(xb, w_gates, br_a, br_s, br_c, w_a, w_s, w_c):
    m, k = xb.shape
    d = w_a.shape[1]
    kb = br_a.shape[1]
    tm = _pick_tile(m, (832, 640, 512, 256, 128, 64, 32, 16, 8))
    tn = _pick_tile(d, (256, 128))
    nj = d // tn

    def body(x_ref, g0_ref, g1_ref, g2_ref, a_ref, s_ref, c_ref, wa_ref, ws_ref, wc_ref, o_ref):
        x = x_ref[...]
        acc = _sigmoid(_bdot(x, g0_ref[...])) * _bdot(a_ref[...], wa_ref[...])
        acc = acc + _sigmoid(_bdot(x, g1_ref[...])) * _bdot(s_ref[...], ws_ref[...])
        acc = acc + _sigmoid(_bdot(x, g2_ref[...])) * _bdot(c_ref[...], wc_ref[...])
        o_ref[...] = acc.astype(o_ref.dtype)

    def gate_spec(b):
        return pl.BlockSpec((k, tn), lambda i, j: (0, b * nj + j))

    act_spec = pl.BlockSpec((tm, kb), lambda i, j: (i, 0))
    w_spec = pl.BlockSpec((kb, tn), lambda i, j: (0, j))
    return pl.pallas_call(
        body, grid=(m // tm, nj),
        in_specs=[pl.BlockSpec((tm, k), lambda i, j: (i, 0)), gate_spec(0), gate_spec(1), gate_spec(2),
                  act_spec, act_spec, act_spec, w_spec, w_spec, w_spec],
        out_specs=pl.BlockSpec((tm, tn), lambda i, j: (i, j)),
        out_shape=jax.ShapeDtypeStruct((m, d), BF16),
        compiler_params=_cparams(("parallel", "parallel")), name="merge_branches")(
            xb, w_gates, w_gates, w_gates, br_a, br_s, br_c, w_a, w_s, w_c)


def _layer_norm_rows(y, g, b):
    mu = jnp.mean(y, axis=-1, keepdims=True)
    yc = y - mu
    var = jnp.mean(yc * yc, axis=-1, keepdims=True)
    return yc * lax.rsqrt(var + LN_EPS) * g + b


def out_proj_ln(merged, w_out, x, g, b):
    m, k = merged.shape
    d = w_out.shape[1]
    tm = _pick_tile(m, (416, 256, 128, 64, 32, 16))

    def body(m_ref, w_ref, x_ref, g_ref, b_ref, o_ref, ob_ref):
        y = DEEPNORM_ALPHA * x_ref[...] + _bdot(m_ref[...], w_ref[...])
        y = _layer_norm_rows(y, g_ref[...], b_ref[...])
        o_ref[...] = y
        ob_ref[...] = y.astype(BF16)

    row = pl.BlockSpec((1, d), lambda i: (0, 0))
    return pl.pallas_call(
        body, grid=(m // tm,),
        in_specs=[pl.BlockSpec((tm, k), lambda i: (i, 0)), pl.BlockSpec((k, d), lambda i: (0, 0)),
                  pl.BlockSpec((tm, d), lambda i: (i, 0)), row, row],
        out_specs=[pl.BlockSpec((tm, d), lambda i: (i, 0)), pl.BlockSpec((tm, d), lambda i: (i, 0))],
        out_shape=[jax.ShapeDtypeStruct((m, d), F32), jax.ShapeDtypeStruct((m, d), BF16)],
        compiler_params=_cparams(("parallel",)), name="out_proj_ln")(
            merged, w_out, x, g.reshape(1, d), b.reshape(1, d))


def ple_mix(x, xb, w_gate, pe, w_proj):
    m, d = x.shape
    kp = pe.shape[1]
    tm = _pick_tile(m, (640, 512, 256, 128, 64, 32, 16))
    tn = _pick_tile(d, (512, 256, 128))

    def body(xb_ref, wg_ref, pe_ref, wp_ref, x_ref, o_ref, ob_ref):
        y = x_ref[...] + _sigmoid(_bdot(xb_ref[...], wg_ref[...])) * _bdot(pe_ref[...], wp_ref[...])
        o_ref[...] = y
        ob_ref[...] = y.astype(BF16)

    return pl.pallas_call(
        body, grid=(m // tm, d // tn),
        in_specs=[pl.BlockSpec((tm, d), lambda i, j: (i, 0)), pl.BlockSpec((d, tn), lambda i, j: (0, j)),
                  pl.BlockSpec((tm, kp), lambda i, j: (i, 0)), pl.BlockSpec((kp, tn), lambda i, j: (0, j)),
                  pl.BlockSpec((tm, tn), lambda i, j: (i, j))],
        out_specs=[pl.BlockSpec((tm, tn), lambda i, j: (i, j)), pl.BlockSpec((tm, tn), lambda i, j: (i, j))],
        out_shape=[jax.ShapeDtypeStruct((m, d), F32), jax.ShapeDtypeStruct((m, d), BF16)],
        compiler_params=_cparams(("parallel", "parallel")), name="ple_mix")(xb, w_gate, pe, w_proj, x)


def s5_tables(lam_re, lam_im, log_dt, b_re, b_im, c_re, c_im):
    hp = lax.Precision.HIGHEST
    cs, nt, gl = S5_CS, S5_TILES, LANES // S5_CH
    dt = jnp.exp(log_dt)[:, None]
    mag = jnp.exp(lam_re * dt)
    ab_re, ab_im = mag * jnp.cos(lam_im * dt), mag * jnp.sin(lam_im * dt)
    den = lam_re * lam_re + lam_im * lam_im
    nr = ab_re - 1.0
    co_re = (nr * lam_re + ab_im * lam_im) / den
    co_im = (ab_im * lam_re - nr * lam_im) / den
    bb_re = co_re[..., None] * b_re - co_im[..., None] * b_im
    bb_im = co_re[..., None] * b_im + co_im[..., None] * b_re
    pr, pi = [jnp.ones_like(ab_re)], [jnp.zeros_like(ab_im)]
    for _ in range(cs):
        pr.append(pr[-1] * ab_re - pi[-1] * ab_im)
        pi.append(pr[-2] * ab_im + pi[-1] * ab_re)
    ap_re, ap_im = jnp.stack(pr), jnp.stack(pi)
    abr = ap_re[:, :, :, None] * bb_re - ap_im[:, :, :, None] * bb_im
    abi = ap_re[:, :, :, None] * bb_im + ap_im[:, :, :, None] * bb_re
    kern = (jnp.einsum('gcp,egpd->egcd', c_re, abr[:cs], precision=hp)
            - jnp.einsum('gcp,egpd->egcd', c_im, abi[:cs], precision=hp))
    same_group = jnp.eye(gl, dtype=bool)

    def block_diag(a, g_axis, h_axis):
        shape = [1] * (a.ndim + 1)
        shape[g_axis if g_axis < h_axis else g_axis + 1] = gl
        shape[h_axis] = gl
        return jnp.where(same_group.reshape(shape), jnp.expand_dims(a, h_axis), 0.0).astype(BF16)

    lag = jnp.arange(cs)[None, :] - jnp.arange(cs)[:, None]
    ksel = jnp.where((lag >= 0)[:, :, None, None, None], kern[jnp.clip(lag, 0, cs - 1)], 0.0)
    ksel = ksel.reshape(cs, cs, nt, gl, S5_CH, S5_CH).transpose(2, 0, 3, 5, 1, 4)
    toep = block_diag(ksel, 2, 5).reshape(nt, cs * LANES, cs * LANES)
    er = abr[:cs][::-1].reshape(cs, nt, gl, S5_STATE, S5_CH).transpose(1, 0, 2, 4, 3)
    ei = abi[:cs][::-1].reshape(cs, nt, gl, S5_STATE, S5_CH).transpose(1, 0, 2, 4, 3)
    bend = block_diag(jnp.stack([er, ei], axis=4), 2, 5)
    bend = bend.reshape(nt, cs * LANES, 2 * S5_TSTATE)
    car = c_re[None] * ap_re[:, :, None, :] - c_im[None] * ap_im[:, :, None, :]
    cai = -(c_re[None] * ap_im[:, :, None, :] + c_im[None] * ap_re[:, :, None, :])
    car = car.reshape(cs + 1, nt, gl, S5_CH, S5_STATE).transpose(1, 2, 4, 0, 3)
    cai = cai.reshape(cs + 1, nt, gl, S5_CH, S5_STATE).transpose(1, 2, 4, 0, 3)
    ccar = block_diag(jnp.stack([car, cai], axis=1), 2, 5)
    ccar = ccar.reshape(nt, 2 * S5_TSTATE, (cs + 1) * LANES)

    def state_row(re, im):
        return jnp.concatenate([re.reshape(nt, 1, S5_TSTATE), im.reshape(nt, 1, S5_TSTATE)], axis=-1)

    return dict(toep=toep.astype(BF16), bend=bend.astype(BF16), bbar=bend[:, (cs - 1) * LANES:].astype(BF16),
                c0=ccar[:, :, :LANES].astype(BF16), ccar=ccar[:, :, LANES:].astype(BF16),
                a1=state_row(ap_re[1], ap_im[1]), acs=state_row(ap_re[cs], ap_im[cs]))


def _chunk_rows(u_ref, nc):
    return jnp.concatenate([u_ref[pl.ds(s, nc, stride=S5_CS), :].astype(BF16) for s in range(S5_CS)], axis=1)


def s5_chunk_states(u_src, bend, *, t_p):
    nt, kc, n = bend.shape
    nc = t_p // S5_CS
    tn = 512

    def body(u_ref, b_ref, o_ref):
        o_ref[0] = jnp.dot(_chunk_rows(u_ref, nc), b_ref[0], preferred_element_type=F32)

    return pl.pallas_call(
        body, grid=(nt, n // tn),
        in_specs=[pl.BlockSpec((t_p, LANES), lambda j, n_: (0, j)),
                  pl.BlockSpec((1, kc, tn), lambda j, n_: (j, 0, n_))],
        out_specs=pl.BlockSpec((1, nc, tn), lambda j, n_: (j, 0, n_)),
        out_shape=jax.ShapeDtypeStruct((nt, nc, n), F32),
        compiler_params=_cparams(("parallel", "parallel")), name="s5_chunk_states")(u_src, bend)


def s5_carry_scan(xe, acs, n_seq):
    nt, nc, n = xe.shape
    per = nc // n_seq
    half = n // 2

    def body(x_ref, a_ref, hp_ref, hf_ref):
        ar = a_ref[0, :, :half]
        ai = a_ref[0, :, half:]

        def step(k, carry):
            hr, hi = carry
            hp_ref[0, pl.ds(k, 1), :] = jnp.concatenate([hr, hi], axis=1)
            x = x_ref[0, pl.ds(k, 1), :]
            return (ar * hr - ai * hi + x[:, :half], ar * hi + ai * hr + x[:, half:])

        zero = jnp.zeros((1, half), F32)
        hr, hi = lax.fori_loop(0, per, step, (zero, zero))
        hf_ref[0, 0] = jnp.concatenate([hr, hi], axis=1)

    return pl.pallas_call(
        body, grid=(nt, n_seq),
        in_specs=[pl.BlockSpec((1, per, n), lambda j, b: (j, b, 0)), pl.BlockSpec((1, 1, n), lambda j, b: (j, 0, 0))],
        out_specs=[pl.BlockSpec((1, per, n), lambda j, b: (j, b, 0)),
                   pl.BlockSpec((1, 1, 1, n), lambda j, b: (j, b, 0, 0))],
        out_shape=[jax.ShapeDtypeStruct((nt, nc, n), F32), jax.ShapeDtypeStruct((nt, n_seq, 1, n), F32)],
        compiler_params=_cparams(("parallel", "parallel")), name="s5_carry_scan")(xe, acs)


def s5_outputs(u_src, toep, hprev, ccar, d_skip, *, t_p):
    nt, kc, _ = toep.shape
    ns = hprev.shape[2]
    nc = t_p // S5_CS
    tn = 512
    per = tn // LANES

    def body(u_ref, t_ref, h_ref, c_ref, d_ref, o_ref):
        uc = _chunk_rows(u_ref, nc)
        hb = h_ref[0].astype(BF16)
        for n_ in range(kc // tn):
            cols = slice(n_ * tn, (n_ + 1) * tn)
            y = (jnp.dot(uc, t_ref[0, :, cols], preferred_element_type=F32)
                 + jnp.dot(hb, c_ref[0, :, cols], preferred_element_type=F32))
            for i in range(per):
                rows = pl.ds(n_ * per + i, nc, stride=S5_CS)
                o_ref[rows, :] = _gelu_tanh(y[:, i * LANES:(i + 1) * LANES] + d_ref[...] * u_ref[rows, :])

    return pl.pallas_call(
        body, grid=(nt,),
        in_specs=[pl.BlockSpec((t_p, LANES), lambda j: (0, j)),
                  pl.BlockSpec((1, kc, kc), lambda j: (j, 0, 0)), pl.BlockSpec((1, nc, ns), lambda j: (j, 0, 0)),
                  pl.BlockSpec((1, ns, kc), lambda j: (j, 0, 0)), pl.BlockSpec((1, LANES), lambda j: (0, j))],
        out_specs=pl.BlockSpec((t_p, LANES), lambda j: (0, j)),
        out_shape=jax.ShapeDtypeStruct((t_p, nt * LANES), F32),
        compiler_params=_cparams(("parallel",)), name="s5_outputs")(
            u_src, toep, hprev, ccar, d_skip.reshape(1, nt * LANES))


def s5_decode(u_src, h_re, h_im, bbar, a1, c0, d_skip, *, row0):
    s = h_re.shape[0]
    nt = bbar.shape[0]
    w = nt * LANES
    ts = S5_TSTATE
    rb = row0 // s

    def body(u_ref, hr_ref, hi_ref, b_ref, a_ref, c_ref, d_ref, y_ref, nr_ref, ni_ref):
        uu = u_ref[...]
        x = _bdot(uu, b_ref[0])
        ar, ai = a_ref[0, :, :ts], a_ref[0, :, ts:]
        hr, hi = hr_ref[...], hi_ref[...]
        nr = ar * hr - ai * hi + x[:, :ts]
        ni = ar * hi + ai * hr + x[:, ts:]
        nr_ref[...] = nr
        ni_ref[...] = ni
        y = _bdot(jnp.concatenate([nr, ni], axis=1), c_ref[0]) + d_ref[...] * uu
        y_ref[...] = _gelu_tanh(y)

    col = pl.BlockSpec((s, LANES), lambda j: (0, j))
    st = pl.BlockSpec((s, ts), lambda j: (0, j))
    return pl.pallas_call(
        body, grid=(nt,),
        in_specs=[pl.BlockSpec((s, LANES), lambda j: (rb, j)), st, st,
                  pl.BlockSpec((1, LANES, 2 * ts), lambda j: (j, 0, 0)),
                  pl.BlockSpec((1, 1, 2 * ts), lambda j: (j, 0, 0)), pl.BlockSpec((1, 2 * ts, LANES), lambda j: (j, 0, 0)),
                  pl.BlockSpec((1, LANES), lambda j: (0, j))],
        out_specs=[col, st, st],
        out_shape=[jax.ShapeDtypeStruct((s, w), F32), jax.ShapeDtypeStruct(h_re.shape, F32),
                   jax.ShapeDtypeStruct(h_im.shape, F32)],
        compiler_params=_cparams(("parallel",)), name="s5_decode")(u_src, h_re, h_im, bbar, a1, c0, d_skip.reshape(1, w))


def s5_branch(u_src, n_seq, seq_len, h_re, h_im, tabs, d_skip):
    t_p = n_seq * seq_len
    nt = S5_TILES
    xe = s5_chunk_states(u_src, tabs['bend'], t_p=t_p)
    hprev, hfin = s5_carry_scan(xe, tabs['acs'], n_seq)
    y_p = s5_outputs(u_src, tabs['toep'], hprev, tabs['ccar'], d_skip, t_p=t_p)
    hfin = hfin.reshape(nt, n_seq, 2, S5_TSTATE).transpose(2, 1, 0, 3).reshape(2, n_seq, S5_GROUPS, S5_STATE)
    s_rows = u_src.shape[0] - t_p
    y_s, nr, ni = s5_decode(u_src, h_re.reshape(s_rows, -1), h_im.reshape(s_rows, -1), tabs['bbar'],
                            tabs['a1'], tabs['c0'], d_skip, row0=t_p)
    return (jnp.concatenate([y_p, y_s], axis=0), hfin[0], hfin[1], nr.reshape(h_re.shape), ni.reshape(h_im.shape))


def moe_route(xb, w_router_t, b_router, *, tm):
    t, d = xb.shape
    ne, ng, gs = N_EXPERTS, N_GROUPS, N_EXPERTS // N_GROUPS
    neg = -jnp.inf

    def body(x_ref, w_ref, b_ref, u_ref, idx_ref, wt_ref, rank_ref, cnt_ref, carry_ref):
        @pl.when(pl.program_id(0) == 0)
        def _():
            carry_ref[...] = jnp.zeros_like(carry_ref)

        logits = lax.dot_general(w_ref[...], x_ref[...], (((1,), (1,)), ((), ())), preferred_element_type=F32)
        scores = _sigmoid(logits).reshape(ng, gs, tm)
        choice = scores + b_ref[...].reshape(ng, gs, 1)
        e_in = lax.broadcasted_iota(I32, (ng, gs, tm), 1).astype(F32)
        g_id = lax.broadcasted_iota(I32, (ng, 1, tm), 0).astype(F32)
        e_id = g_id * gs + e_in
        m1 = jnp.max(choice, axis=1, keepdims=True)
        i1 = jnp.min(jnp.where(choice == m1, e_in, float(gs)), axis=1, keepdims=True)
        m2 = jnp.max(jnp.where(e_in == i1, neg, choice), axis=1, keepdims=True)
        gscore = m1 + m2
        keep = jnp.zeros((ng, 1, tm), F32)
        for _ in range(TOPK_GROUPS):
            gm = jnp.max(gscore, axis=0, keepdims=True)
            gi = jnp.min(jnp.where(gscore == gm, g_id, float(ng)), axis=0, keepdims=True)
            hit = g_id == gi
            keep = jnp.where(hit, 1.0, keep)
            gscore = jnp.where(hit, neg, gscore)
        cand = jnp.where(keep > 0.0, choice, neg)
        member = jnp.zeros((ng, gs, tm), F32)
        picks, wts = [], []
        for _ in range(TOP_K):
            cm = jnp.max(jnp.max(cand, axis=1, keepdims=True), axis=0, keepdims=True)
            ei = jnp.min(jnp.min(jnp.where(cand == cm, e_id, float(ne)), axis=1, keepdims=True), axis=0, keepdims=True)
            sel = e_id == ei
            wts.append(jnp.sum(jnp.sum(jnp.where(sel, scores, 0.0), axis=1, keepdims=True), axis=0, keepdims=True))
            picks.append(ei)
            member = jnp.where(sel, 1.0, member)
            cand = jnp.where(sel, neg, cand)
        wsum = wts[0]
        for w in wts[1:]:
            wsum = wsum + w
        member2 = member.reshape(ne, tm)
        prefix = jnp.dot(member2.astype(BF16), u_ref[...], preferred_element_type=F32) + carry_ref[:, 0:1]
        prefix = prefix.reshape(ng, gs, tm)
        for j in range(TOP_K):
            sel = e_id == picks[j]
            rk = jnp.sum(jnp.sum(jnp.where(sel, prefix, 0.0), axis=1, keepdims=True), axis=0, keepdims=True)
            idx_ref[j:j + 1, :] = picks[j].reshape(1, tm).astype(I32)
            rank_ref[j:j + 1, :] = rk.reshape(1, tm).astype(I32)
            wt_ref[j:j + 1, :] = (wts[j] / wsum * ROUTED_SCALE).reshape(1, tm)
        carry_ref[...] = carry_ref[...] + jnp.sum(member2, axis=1, keepdims=True)
        cnt_ref[...] = carry_ref[...]

    upper = jnp.triu(jnp.ones((tm, tm), F32), 1).astype(BF16)
    tok = pl.BlockSpec((TOP_K, tm), lambda i: (0, i))
    idx, wt, rank, cnt = pl.pallas_call(
        body, grid=(t // tm,),
        in_specs=[pl.BlockSpec((tm, d), lambda i: (i, 0)), pl.BlockSpec((ne, d), lambda i: (0, 0)),
                  pl.BlockSpec((ne, 1), lambda i: (0, 0)), pl.BlockSpec((tm, tm), lambda i: (0, 0))],
        out_specs=[tok, tok, tok, pl.BlockSpec((ne, LANES), lambda i: (0, 0))],
        out_shape=[jax.ShapeDtypeStruct((TOP_K, t), I32), jax.ShapeDtypeStruct((TOP_K, t), F32),
                   jax.ShapeDtypeStruct((TOP_K, t), I32), jax.ShapeDtypeStruct((ne, LANES), F32)],
        scratch_shapes=[pltpu.VMEM((ne, LANES), F32)],
        compiler_params=_cparams(("arbitrary",)), name="moe_route")(xb, w_router_t, b_router.reshape(ne, 1), upper)
    return idx, wt, rank, cnt[:, 0]


def moe_experts(x, row_tok, block_e, n_used, w1, w3, w2, *, rows, layer):
    t, d = x.shape
    nb = row_tok.shape[0]
    f = w1.shape[3]

    def body(be_ref, nu_ref, x_hbm, tok_ref, tokn_ref, w1_ref, w3_ref, w2_ref, o_ref, buf, sem, w1b, w3b, w2b):
        i = pl.program_id(0)
        slot = lax.rem(i, 2)

        def gather(tref, sl):
            def issue(r, c):
                tok = tref[0, 0, r]
                pltpu.make_async_copy(x_hbm.at[pl.ds(tok, 1), :], buf.at[sl, pl.ds(r, 1), :],
                                      sem.at[sl]).start(priority=1)
                return c
            lax.fori_loop(0, rows, issue, 0, unroll=8)

        @pl.when(i == 0)
        def _():
            gather(tok_ref, 0)

        @pl.when(i + 1 < nb)
        def _():
            gather(tokn_ref, 1 - slot)

        e = be_ref[i]
        changed = jnp.logical_or(i == 0, e != be_ref[jnp.maximum(i - 1, 0)])

        @pl.when(changed)
        def _():
            w1b[...] = w1_ref[0, 0].astype(BF16)
            w3b[...] = w3_ref[0, 0].astype(BF16)
            w2b[...] = w2_ref[0, 0].astype(BF16)

        pltpu.make_async_copy(x_hbm.at[pl.ds(0, rows), :], buf.at[slot], sem.at[slot]).wait()

        @pl.when(i < nu_ref[0])
        def _():
            xb = buf[slot].astype(BF16)
            h = _silu(jnp.dot(xb, w1b[...], preferred_element_type=F32)) * jnp.dot(xb, w3b[...],
                                                                                preferred_element_type=F32)
            o_ref[...] = jnp.dot(h.astype(BF16), w2b[...], preferred_element_type=F32)

        @pl.when(i >= nu_ref[0])
        def _():
            o_ref[...] = jnp.zeros_like(o_ref)

    grid_spec = pltpu.PrefetchScalarGridSpec(
        num_scalar_prefetch=2, grid=(nb,),
        in_specs=[pl.BlockSpec(memory_space=pl.ANY),
                  pl.BlockSpec((1, 1, rows), lambda i, be, nu: (i, 0, 0), memory_space=pltpu.SMEM),
                  pl.BlockSpec((1, 1, rows), lambda i, be, nu: (jnp.minimum(i + 1, nb - 1), 0, 0),
                               memory_space=pltpu.SMEM),
                  pl.BlockSpec((1, 1, d, f), lambda i, be, nu: (layer, be[i], 0, 0)),
                  pl.BlockSpec((1, 1, d, f), lambda i, be, nu: (layer, be[i], 0, 0)),
                  pl.BlockSpec((1, 1, f, d), lambda i, be, nu: (layer, be[i], 0, 0))],
        out_specs=pl.BlockSpec((rows, d), lambda i, be, nu: (i, 0)),
        scratch_shapes=[pltpu.VMEM((2, rows, d), F32), pltpu.SemaphoreType.DMA((2,)),
                        pltpu.VMEM((d, f), BF16), pltpu.VMEM((d, f), BF16), pltpu.VMEM((f, d), BF16)])
    return pl.pallas_call(
        body, grid_spec=grid_spec, out_shape=jax.ShapeDtypeStruct((nb * rows, d), F32),
        compiler_params=_cparams(("arbitrary",)), name="moe_experts")(block_e, n_used, x, row_tok, row_tok, w1, w3, w2)


def moe_combine_ln(ys, dest, wts, x, shared, g, b, *, tm):
    t, d = x.shape
    nt = t // tm

    def body(ys_hbm, d_ref, dn_ref, w_ref, x_ref, s_ref, g_ref, b_ref, o_ref, ob_ref, buf, sem):
        i = pl.program_id(0)
        slot = lax.rem(i, 2)

        def gather(dref, sl):
            def issue(r, c):
                row = dref[0, 0, r]
                pltpu.make_async_copy(ys_hbm.at[pl.ds(row, 1), :], buf.at[sl, pl.ds(r, 1), :],
                                      sem.at[sl]).start(priority=1)
                return c
            lax.fori_loop(0, TOP_K * tm, issue, 0, unroll=8)

        @pl.when(i == 0)
        def _():
            gather(d_ref, 0)

        @pl.when(i + 1 < nt)
        def _():
            gather(dn_ref, 1 - slot)

        pltpu.make_async_copy(ys_hbm.at[pl.ds(0, TOP_K * tm), :], buf.at[slot], sem.at[slot]).wait()
        w = w_ref[...]
        acc = DEEPNORM_ALPHA * x_ref[...] + s_ref[...]
        for j in range(TOP_K):
            acc = acc + w[:, j:j + 1] * buf[slot, j * tm:(j + 1) * tm, :]
        y = _layer_norm_rows(acc, g_ref[...], b_ref[...])
        o_ref[...] = y
        ob_ref[...] = y.astype(BF16)

    row = pl.BlockSpec((1, d), lambda i: (0, 0))
    tile = pl.BlockSpec((tm, d), lambda i: (i, 0))
    return pl.pallas_call(
        body, grid=(nt,),
        in_specs=[pl.BlockSpec(memory_space=pl.ANY),
                  pl.BlockSpec((1, 1, TOP_K * tm), lambda i: (i, 0, 0), memory_space=pltpu.SMEM),
                  pl.BlockSpec((1, 1, TOP_K * tm), lambda i: (jnp.minimum(i + 1, nt - 1), 0, 0),
                               memory_space=pltpu.SMEM),
                  pl.BlockSpec((tm, TOP_K), lambda i: (i, 0)), tile, tile, row, row],
        out_specs=[tile, tile],
        out_shape=[jax.ShapeDtypeStruct((t, d), F32), jax.ShapeDtypeStruct((t, d), BF16)],
        scratch_shapes=[pltpu.VMEM((2, TOP_K * tm, d), F32), pltpu.SemaphoreType.DMA((2,))],
        compiler_params=_cparams(("arbitrary",)), name="moe_combine_ln")(
            ys, dest, dest, wts, x, shared, g.reshape(1, d), b.reshape(1, d))


def moe_layer(x, xb, lw, *, route_tm, rows=MOE_ROWS, tok_tile=MOE_TOK_TILE):
    t, d = x.shape
    idx, wt, rank, counts = moe_route(xb, lw['moe_w_router'].T.astype(BF16), lw['moe_b_router'], tm=route_tm)
    counts = counts.astype(I32)
    padded = (counts + rows - 1) // rows * rows
    pad_end = jnp.cumsum(padded)
    pad_start = pad_end - padded
    onehot = idx[:, :, None] == jnp.arange(N_EXPERTS, dtype=I32)[None, None, :]
    dest = jnp.sum(jnp.where(onehot, pad_start[None, None, :], 0), axis=-1) + rank
    n_rows = -(-(t * TOP_K + N_EXPERTS * (rows - 1)) // rows) * rows
    nb = n_rows // rows
    tok_id = jnp.broadcast_to(jnp.arange(t, dtype=I32)[None, :], (TOP_K, t))
    row_tok = jnp.zeros((n_rows,), I32).at[dest.reshape(-1)].set(tok_id.reshape(-1))
    blk_start = jnp.arange(nb, dtype=I32) * rows
    block_e = jnp.minimum(jnp.sum((pad_end[None, :] <= blk_start[:, None]).astype(I32), axis=1), N_EXPERTS - 1)
    n_used = (pad_end[-1] // rows).astype(I32).reshape(1)
    ys = moe_experts(x, row_tok.reshape(nb, 1, rows), block_e, n_used, lw['moe_w1'], lw['moe_w3'], lw['moe_w2'],
                     rows=rows, layer=lw['layer'])
    hs = swiglu_hidden(xb, lw['moe_ws1'], lw['moe_ws3'])
    shared = dense(hs, lw['moe_ws2'])
    dest_t = dest.reshape(TOP_K, t // tok_tile, tok_tile).transpose(1, 0, 2).reshape(t // tok_tile, 1, TOP_K * tok_tile)
    return moe_combine_ln(ys, dest_t, wt.T, x, shared, lw['ln2_g'], lw['ln2_b'], tm=tok_tile)


def _level_tables(c):
    import numpy as np
    idx = np.arange(c)
    t, r = idx[:, None], idx[None, :]
    wl, pm = [], []
    b = 1
    while b < c:
        blk, odd = t // b, (t // b) % 2 == 1
        w = np.where(odd, (r >= blk * b) & (r <= t), (r > t) & (r <= blk * b + b - 1))
        wl.append(w.astype(np.float32))
        pm.append((odd & (r // b == blk - 1)).astype(np.float32))
        b *= 2
    incl = (r <= t).astype(np.float32)
    after = (r > t).astype(np.float32)
    return wl, pm, incl, after


def _split3(x):
    hi = x.astype(BF16)
    r1 = x - hi.astype(F32)
    mid = r1.astype(BF16)
    lo = (r1 - mid.astype(F32)).astype(BF16)
    return hi, mid, lo


def _table_dot(tab, x):
    hi, mid, lo = _split3(x)
    return (jnp.dot(tab, hi, preferred_element_type=F32) + jnp.dot(tab, mid, preferred_element_type=F32)
            + jnp.dot(tab, lo, preferred_element_type=F32))


def _dot_hi(a, b):
    ah = a.astype(BF16)
    al = (a - ah.astype(F32)).astype(BF16)
    bh = b.astype(BF16)
    bl = (b - bh.astype(F32)).astype(BF16)
    return (jnp.dot(ah, bh, preferred_element_type=F32) + jnp.dot(ah, bl, preferred_element_type=F32)
            + jnp.dot(al, bh, preferred_element_type=F32))


def _dot_nt(a, b):
    return lax.dot_general(a.astype(BF16), b.astype(BF16), (((1,), (1,)), ((), ())), preferred_element_type=F32)


def _dot_tn(a, b):
    return lax.dot_general(a.astype(BF16), b.astype(BF16), (((0,), (0,)), ((), ())), preferred_element_type=F32)


def _rms_rows(o, w):
    return o * lax.rsqrt(jnp.mean(o * o, axis=-1, keepdims=True) + NORM_EPS) * w


def gla_prompt(z1, log_a, norm_w, *, n_seq, seq_len):
    c, h_, dk, dv = CHUNK, GLA_HEADS, GLA_DK, GLA_DV
    nck = seq_len // c
    wl, pm, incl, after = _level_tables(c)
    nl = len(wl)
    wcat = jnp.asarray(jnp.concatenate([jnp.asarray(w) for w in wl] + [jnp.asarray(incl), jnp.asarray(after)], axis=0),
                       BF16)
    pmask = jnp.stack([jnp.eye(c, dtype=F32)] + [jnp.asarray(p) for p in pm])
    scale = dk ** -0.5

    def body(q_ref, k_ref, v_ref, r_ref, g_ref, w_ref, p_ref, n_ref, o_ref, st_ref, s_scr):
        ci = pl.program_id(1)

        @pl.when(ci == 0)
        def _():
            s_scr[...] = jnp.zeros_like(s_scr)

        x = _table_dot(w_ref[...], g_ref[...])
        ex = jnp.exp(x)
        for h in range(h_):
            ks = slice(h * dk, (h + 1) * dk)
            vs = slice(h * dv, (h + 1) * dv)
            q = q_ref[:, ks] * scale
            k = k_ref[:, ks]
            v = v_ref[:, vs]
            scores = p_ref[0] * _dot_nt(q, k)
            for l in range(nl):
                f = ex[l * c:(l + 1) * c, ks]
                scores = scores + p_ref[l + 1] * _dot_nt(q * f, k * f)
            st = s_scr[h]
            o = _dot_nt(q * ex[nl * c:(nl + 1) * c, ks], st) + _bdot(scores, v)
            tot = x[(nl + 1) * c - 1:(nl + 1) * c, ks]
            s_scr[h] = jnp.exp(tot) * st + _dot_tn(v, k * ex[(nl + 1) * c:(nl + 2) * c, ks])
            o_ref[:, vs] = (_rms_rows(o, n_ref[...]) * _silu(r_ref[:, vs])).astype(o_ref.dtype)

        @pl.when(ci == nck - 1)
        def _():
            st_ref[0] = s_scr[...]

    def rows(width, col):
        return pl.BlockSpec((c, width), lambda b, i: (b * nck + i, col))

    return pl.pallas_call(
        body, grid=(n_seq, nck),
        in_specs=[rows(h_ * dk, 0), rows(h_ * dk, 1), rows(h_ * dv, 1), rows(h_ * dv, 2), rows(h_ * dk, 0),
                  pl.BlockSpec(wcat.shape, lambda b, i: (0, 0)), pl.BlockSpec(pmask.shape, lambda b, i: (0, 0, 0)),
                  pl.BlockSpec((1, dv), lambda b, i: (0, 0))],
        out_specs=[rows(h_ * dv, 0), pl.BlockSpec((1, h_, dv, dk), lambda b, i: (b, 0, 0, 0))],
        out_shape=[jax.ShapeDtypeStruct((n_seq * seq_len, h_ * dv), BF16),
                   jax.ShapeDtypeStruct((n_seq, h_, dv, dk), F32)],
        scratch_shapes=[pltpu.VMEM((h_, dv, dk), F32)],
        compiler_params=_cparams(("parallel", "arbitrary")), name="gla_prompt")(
            z1, z1, z1, z1, log_a, wcat, pmask, norm_w.reshape(1, dv))


def _columns(rows_list, width):
    used = sum(r.shape[0] for r in rows_list)
    stack = jnp.concatenate(list(rows_list) + [jnp.zeros((LANES - used, width), F32)], axis=0)
    return stack.T


def gla_decode(z1, log_a, state, norm_w, *, row0, layer):
    sb = 8
    n_s = state.shape[1]
    h_, dk, dv = GLA_HEADS, GLA_DK, GLA_DV
    scale = dk ** -0.5
    r0 = row0 // sb

    def body(q_ref, k_ref, v_ref, r_ref, g_ref, s_ref, n_ref, o_ref, ns_ref):
        for h in range(h_):
            ks = slice(h * dk, (h + 1) * dk)
            vs = slice(h * dv, (h + 1) * dv)
            cols = _columns([jnp.exp(g_ref[:, ks]), k_ref[:, ks], q_ref[:, ks] * scale], dk)
            v = v_ref[:, vs]
            outs = []
            for s in range(sb):
                s_new = cols[:, s:s + 1] * s_ref[0, s, h] + cols[:, sb + s:sb + s + 1] * v[s:s + 1, :]
                ns_ref[s, h] = s_new
                outs.append(jnp.sum(cols[:, 2 * sb + s:2 * sb + s + 1] * s_new, axis=0, keepdims=True))
            o = jnp.concatenate(outs, axis=0)
            o_ref[:, vs] = _rms_rows(o, n_ref[...]) * _silu(r_ref[:, vs])

    def rows(width, col):
        return pl.BlockSpec((sb, width), lambda i: (r0 + i, col))

    st_in = pl.BlockSpec((1, sb, h_, dk, dv), lambda i: (layer, i, 0, 0, 0))
    st = pl.BlockSpec((sb, h_, dk, dv), lambda i: (i, 0, 0, 0))
    return pl.pallas_call(
        body, grid=(n_s // sb,),
        in_specs=[rows(h_ * dk, 0), rows(h_ * dk, 1), rows(h_ * dv, 1), rows(h_ * dv, 2), rows(h_ * dk, 0), st_in,
                  pl.BlockSpec((1, dv), lambda i: (0, 0))],
        out_specs=[pl.BlockSpec((sb, h_ * dv), lambda i: (i, 0)), st],
        out_shape=[jax.ShapeDtypeStruct((n_s, h_ * dv), F32), jax.ShapeDtypeStruct(state.shape[1:], F32)],
        compiler_params=_cparams(("parallel",)), name="gla_decode")(
            z1, z1, z1, z1, log_a, state, norm_w.reshape(1, dv))


def _conv_silu(ext, w, c):
    acc = ext[5:5 + c] * w[0:1]
    for i in range(1, GDN_CONV):
        acc = acc + ext[5 + i:5 + i + c] * w[i:i + 1]
    return _silu(acc)


def _softplus(x):
    return jnp.maximum(x, 0.0) + jnp.log1p(jnp.exp(-jnp.abs(x)))


def _l2n(x):
    return x * lax.rsqrt(jnp.sum(x * x, axis=-1, keepdims=True) + NORM_EPS)


def gdn_prompt(z3, z4, conv_w, a_log, dt_bias, norm_w, *, n_seq, seq_len):
    c, h_, dk, dv = CHUNK, GDN_HEADS, GDN_DK, GDN_DV
    kw = h_ * dk
    nck = seq_len // c
    _, pm, incl, after = _level_tables(c)
    nl = len(pm)
    import numpy as np
    strict = (np.arange(c)[:, None] > np.arange(c)[None, :]).astype(np.float32)
    tabs = jnp.asarray(np.concatenate([incl, after, np.ones((c, c), np.float32)], axis=0), BF16)
    masks = jnp.stack([jnp.asarray(incl), jnp.asarray(strict), jnp.eye(c, dtype=F32)] + [jnp.asarray(p) for p in pm])
    strict_pad = jnp.asarray(np.concatenate([strict, np.zeros((c, LANES - c), np.float32)], axis=1))
    qscale = dk ** -0.5

    def body(q_ref, k_ref, v_ref, zg_ref, ab_ref, cw_ref, al_ref, db_ref, t_ref, m_ref, sp_ref, n_ref,
             o_ref, st_ref, s_scr, hist):
        ci = pl.program_id(1)

        @pl.when(ci == 0)
        def _():
            s_scr[...] = jnp.zeros_like(s_scr)
            hist[...] = jnp.zeros_like(hist)

        def conv(ref, j):
            cols = slice(j * kw, (j + 1) * kw)
            raw = ref[...]
            ext = jnp.concatenate([hist[:, cols], raw], axis=0)
            y = _conv_silu(ext, cw_ref[:, cols], c)
            hist[:, cols] = raw[c - 8:c]
            return y

        qc, kc, vc = conv(q_ref, 0), conv(k_ref, 1), conv(v_ref, 2)
        ab = ab_ref[...]
        g = -jnp.exp(al_ref[...]) * _softplus(ab[:, :h_] + db_ref[...])
        beta = _sigmoid(ab[:, h_:])
        sums = _table_dot(t_ref[...], jnp.concatenate([g, jnp.zeros((c, LANES - h_), F32)], axis=1))
        e_cum = jnp.exp(sums[0:c])
        e_rest = jnp.exp(sums[c:2 * c])
        e_last = jnp.exp(sums[2 * c:2 * c + 1])
        grel = jnp.concatenate([g[:, h:h + 1] * sp_ref[...] for h in range(h_)], axis=1)
        rel = _table_dot(t_ref[0:c], grel)
        m_incl, m_strict, m_eye = m_ref[0], m_ref[1], m_ref[2]
        qs, ks, vs, kbs, decs, amat, tinv = [], [], [], [], [], [], []
        for h in range(h_):
            hs = slice(h * dk, (h + 1) * dk)
            qs.append(_l2n(qc[:, hs]) * qscale)
            ks.append(_l2n(kc[:, hs]))
            vs.append(vc[:, hs])
            decs.append(m_incl * jnp.exp(m_incl * rel[:, h * LANES:h * LANES + c]))
            kbs.append(ks[h] * beta[:, h:h + 1])
            amat.append(m_strict * _dot_nt(kbs[h], ks[h]) * decs[h])
            tinv.append(m_eye - m_ref[3] * amat[h])
        for l in range(1, nl):
            tinv = [tinv[h] - _dot_hi(_dot_hi(tinv[h], m_ref[3 + l] * amat[h]), tinv[h]) for h in range(h_)]
        for h in range(h_):
            hs = slice(h * dk, (h + 1) * dk)
            q, k, v, kb, dec, t = qs[h], ks[h], vs[h], kbs[h], decs[h], tinv[h]
            bcol = beta[:, h:h + 1]
            tw = _bdot(t, jnp.concatenate([kb * e_cum[:, h:h + 1], v * bcol], axis=1))
            s_old = s_scr[h]
            both = _bdot(jnp.concatenate([q * e_cum[:, h:h + 1], tw[:, :dk]], axis=0), s_old)
            u = tw[:, dk:] - both[c:]
            o = both[:c] + _bdot(_dot_nt(q, k) * dec, u)
            s_scr[h] = e_last[:, h:h + 1] * s_old + _dot_tn(k * e_rest[:, h:h + 1], u)
            o_ref[:, hs] = (_rms_rows(o, n_ref[...]) * _silu(zg_ref[:, hs])).astype(o_ref.dtype)

        @pl.when(ci == nck - 1)
        def _():
            st_ref[0] = s_scr[...]

    def rows(width, col):
        return pl.BlockSpec((c, width), lambda b, i: (b * nck + i, col))

    def const(arr):
        nd = arr.ndim
        return pl.BlockSpec(arr.shape, lambda b, i: (0,) * nd)

    cw = conv_w
    al = a_log.reshape(1, h_)
    db = dt_bias.reshape(1, h_)
    nw = norm_w.reshape(1, dv)
    return pl.pallas_call(
        body, grid=(n_seq, nck),
        in_specs=[rows(kw, 1), rows(kw, 2), rows(kw, 3), rows(kw, 4), pl.BlockSpec((c, 2 * h_), lambda b, i: (b * nck + i, 0)),
                  const(cw), const(al), const(db), const(tabs), const(masks), const(strict_pad), const(nw)],
        out_specs=[rows(kw, 0), pl.BlockSpec((1, h_, dk, dv), lambda b, i: (b, 0, 0, 0))],
        out_shape=[jax.ShapeDtypeStruct((n_seq * seq_len, kw), BF16), jax.ShapeDtypeStruct((n_seq, h_, dk, dv), F32)],
        scratch_shapes=[pltpu.VMEM((h_, dk, dv), F32), pltpu.VMEM((8, 3 * kw), F32)],
        compiler_params=_cparams(("parallel", "arbitrary")), name="gdn_prompt")(
            z3, z3, z3, z3, z4, cw, al, db, tabs, masks, strict_pad, nw)


def gdn_decode(z3, z4, state, conv_buf, conv_w, a_log, dt_bias, norm_w, *, row0, layer):
    sb = 8
    n_s = state.shape[1]
    h_, dk, dv = GDN_HEADS, GDN_DK, GDN_DV
    kw = h_ * dk
    r0 = row0 // sb
    qscale = dk ** -0.5

    def body(q_ref, k_ref, v_ref, zg_ref, ab_ref, hb_ref, cw_ref, al_ref, db_ref, s_ref, n_ref, o_ref, ns_ref):
        def conv(ref, j):
            cols = slice(j * kw, (j + 1) * kw)
            acc = ref[...] * cw_ref[GDN_CONV - 1:GDN_CONV, cols]
            for i in range(GDN_CONV - 1):
                acc = acc + hb_ref[:, i, cols] * cw_ref[i:i + 1, cols]
            return _silu(acc)

        qc, kc, vc = conv(q_ref, 0), conv(k_ref, 1), conv(v_ref, 2)
        ab = ab_ref[...]
        eg = jnp.exp(-jnp.exp(al_ref[...]) * _softplus(ab[:, :h_] + db_ref[...]))
        beta = _sigmoid(ab[:, h_:])
        for h in range(h_):
            hs = slice(h * dk, (h + 1) * dk)
            q = _l2n(qc[:, hs]) * qscale
            k = _l2n(kc[:, hs])
            v = vc[:, hs]
            cols = _columns([k, q], dk)
            qk = jnp.sum(q * k, axis=-1, keepdims=True)
            outs = []
            for s in range(sb):
                s_old = s_ref[0, s, h]
                kcol = cols[:, s:s + 1]
                k_s = jnp.sum(kcol * s_old, axis=0, keepdims=True)
                q_s = jnp.sum(cols[:, sb + s:sb + s + 1] * s_old, axis=0, keepdims=True)
                e = eg[s:s + 1, h:h + 1]
                u = beta[s:s + 1, h:h + 1] * (v[s:s + 1, :] - e * k_s)
                ns_ref[s, h] = e * s_old + kcol * u
                outs.append(e * q_s + qk[s:s + 1, :] * u)
            o = jnp.concatenate(outs, axis=0)
            o_ref[:, hs] = _rms_rows(o, n_ref[...]) * _silu(zg_ref[:, hs])

    def rows(width, col):
        return pl.BlockSpec((sb, width), lambda i: (r0 + i, col))

    def const(arr):
        nd = arr.ndim
        return pl.BlockSpec(arr.shape, lambda i: (0,) * nd)

    st_in = pl.BlockSpec((1, sb, h_, dk, dv), lambda i: (layer, i, 0, 0, 0))
    st = pl.BlockSpec((sb, h_, dk, dv), lambda i: (i, 0, 0, 0))
    al = a_log.reshape(1, h_)
    db = dt_bias.reshape(1, h_)
    nw = norm_w.reshape(1, dv)
    return pl.pallas_call(
        body, grid=(n_s // sb,),
        in_specs=[rows(kw, 1), rows(kw, 2), rows(kw, 3), rows(kw, 4), pl.BlockSpec((sb, 2 * h_), lambda i: (r0 + i, 0)),
                  pl.BlockSpec((sb, GDN_CONV - 1, 3 * kw), lambda i: (i, 0, 0)), const(conv_w), const(al), const(db),
                  st_in, const(nw)],
        out_specs=[pl.BlockSpec((sb, kw), lambda i: (i, 0)), st],
        out_shape=[jax.ShapeDtypeStruct((n_s, kw), F32), jax.ShapeDtypeStruct(state.shape[1:], F32)],
        compiler_params=_cparams(("parallel",)), name="gdn_decode")(
            z3, z3, z3, z3, z4, conv_buf, conv_w, al, db, state, nw)


def trunk_layer(x, xb, pe_b, states, lw, *, n_seq, seq_len, route_tm):
    t_p = n_seq * seq_len
    n_s = x.shape[0] - t_p
    gla_s, s5_re, s5_im, gdn_s, conv_s = states
    w_in = lw['w_in']
    z1 = dense(xb, w_in[:, 0:3072].astype(BF16))
    a_lr = dense(xb, w_in[:, 3072:3088].astype(BF16))
    z3 = dense(xb, w_in[:, 3088:8208].astype(BF16))
    z4 = dense(xb, w_in[:, 8208:8224].astype(BF16))
    log_a = dense(a_lr, lw['gla_w_gate'], bias=lw['gla_b_gate'], act='log_decay')
    br_a_p, gla_pt = gla_prompt(z1, log_a, lw['gla_norm'], n_seq=n_seq, seq_len=seq_len)
    gla_p = jnp.swapaxes(gla_pt, 2, 3)
    br_a_s, gla_n = gla_decode(z1, log_a, gla_s, lw['gla_norm'], row0=t_p, layer=lw['layer'])
    br_a = jnp.concatenate([br_a_p, br_a_s.astype(BF16)], axis=0)
    tabs = s5_tables(lw['s5_lam_re'], lw['s5_lam_im'], lw['s5_log_dt'], lw['s5_b_re'], lw['s5_b_im'],
                     lw['s5_c_re'], lw['s5_c_im'])
    y_s, s5r_p, s5i_p, s5r_n, s5i_n = s5_branch(z3, n_seq, seq_len, s5_re, s5_im, tabs, lw['s5_d'])
    br_s = glu_gate(y_s, lw['s5_w_glu'].astype(BF16), lw['s5_b_glu'])
    br_c_p, gdn_p = gdn_prompt(z3, z4, lw['gdn_conv_w'], lw['gdn_a_log'], lw['gdn_dt_bias'], lw['gdn_norm'],
                               n_seq=n_seq, seq_len=seq_len)
    br_c_s, gdn_n = gdn_decode(z3, z4, gdn_s, conv_s, lw['gdn_conv_w'], lw['gdn_a_log'], lw['gdn_dt_bias'],
                               lw['gdn_norm'], row0=t_p, layer=lw['layer'])
    br_c = jnp.concatenate([br_c_p, br_c_s.astype(BF16)], axis=0)
    raw = z3[:, S5_WIDTH:S5_WIDTH + 2 * GDN_KW + GDN_VW]
    conv_p = raw[:t_p].reshape(n_seq, seq_len, -1)[:, seq_len - (GDN_CONV - 1):]
    conv_n = jnp.concatenate([conv_s[:, 1:], raw[t_p:, None, :]], axis=1)
    merged = merge_branches(xb, w_in[:, 8224:].astype(BF16), br_a, br_s, br_c, lw['w_branch_a'].astype(BF16),
                            lw['w_branch_s'].astype(BF16), lw['w_branch_c'].astype(BF16))
    x1, x1b = out_proj_ln(merged, lw['w_out'].astype(BF16), x, lw['ln1_g'], lw['ln1_b'])
    x2, x2b = moe_layer(x1, x1b, lw, route_tm=route_tm)
    x3, x3b = ple_mix(x2, x2b, lw['ple_w_gate'].astype(BF16), pe_b, lw['ple_w_proj'].astype(BF16))
    return x3, x3b, (gla_p, s5r_p, s5i_p, gdn_p, conv_p), (gla_n, s5r_n, s5i_n, gdn_n, conv_n)


_NAMES = ('w_in', 'gla_w_gate', 'gla_b_gate', 'gla_norm', 's5_lam_re', 's5_lam_im', 's5_log_dt', 's5_b_re',
          's5_b_im', 's5_c_re', 's5_c_im', 's5_d', 's5_w_glu', 's5_b_glu', 'gdn_conv_w', 'gdn_a_log',
          'gdn_dt_bias', 'gdn_norm', 'w_branch_a', 'w_branch_s', 'w_branch_c', 'w_out', 'ln1_g', 'ln1_b',
          'ln2_g', 'ln2_b', 'moe_w_router', 'moe_b_router', 'moe_w1', 'moe_w3', 'moe_w2', 'moe_ws1', 'moe_ws3',
          'moe_ws2', 'ple_w_proj', 'ple_w_gate')


_STACKED = ('moe_w1', 'moe_w3', 'moe_w2')


def run_trunk(x_prompt, x_sample, p_prompt, p_sample, states, weights, *, route_tm):
    n_seq, seq_len, d = x_prompt.shape
    n_s = x_sample.shape[0]
    t_p = n_seq * seq_len
    depth = weights[0].shape[0]
    x = jnp.concatenate([x_prompt.reshape(t_p, d), x_sample.reshape(n_s, d)], axis=0)
    xb = x.astype(BF16)
    pe = jnp.concatenate([p_prompt.reshape(depth, t_p, -1), p_sample.reshape(depth, n_s, -1)], axis=1).astype(BF16)
    new_p, new_s = [], []
    for i in range(depth):
        lw = {n: (w if n in _STACKED else w[i]) for n, w in zip(_NAMES, weights)}
        lw['layer'] = i
        st = (states[0], states[1][i], states[2][i], states[3], states[4][i])
        x, xb, st_p, st_s = trunk_layer(x, xb, pe[i], st, lw, n_seq=n_seq, seq_len=seq_len, route_tm=route_tm)
        new_p.append(st_p)
        new_s.append(st_s)
    gla_p, s5r_p, s5i_p, gdn_p, conv_p = (jnp.stack(f) for f in zip(*new_p))
    gla_s, s5r_s, s5i_s, gdn_s, conv_s = (jnp.stack(f) for f in zip(*new_s))
    yp = x[:t_p].reshape(n_seq, seq_len, d)
    ys = x[t_p:].reshape(n_s, 1, d)
    return (yp, ys, gla_p, gla_s, s5r_p, s5r_s, s5i_p, s5i_s, gdn_p, gdn_s, conv_p, conv_s)


def kernel(x_prompt, x_sample, p_prompt, p_sample, state_gla, state_s5_re, state_s5_im, state_gdn, state_gdn_conv,
           w_in, gla_w_gate, gla_b_gate, gla_norm, s5_lam_re, s5_lam_im, s5_log_dt, s5_b_re, s5_b_im, s5_c_re,
           s5_c_im, s5_d, s5_w_glu, s5_b_glu, gdn_conv_w, gdn_a_log, gdn_dt_bias, gdn_norm, w_branch_a,
           w_branch_s, w_branch_c, w_out, ln1_g, ln1_b, ln2_g, ln2_b, moe_w_router, moe_b_router, moe_w1, moe_w3,
           moe_w2, moe_ws1, moe_ws3, moe_ws2, ple_w_proj, ple_w_gate):
    weights = (w_in, gla_w_gate, gla_b_gate, gla_norm, s5_lam_re, s5_lam_im, s5_log_dt, s5_b_re, s5_b_im, s5_c_re,
               s5_c_im, s5_d, s5_w_glu, s5_b_glu, gdn_conv_w, gdn_a_log, gdn_dt_bias, gdn_norm, w_branch_a,
               w_branch_s, w_branch_c, w_out, ln1_g, ln1_b, ln2_g, ln2_b, moe_w_router, moe_b_router, moe_w1,
               moe_w3, moe_w2, moe_ws1, moe_ws3, moe_ws2, ple_w_proj, ple_w_gate)
    states = (state_gla, state_s5_re, state_s5_im, state_gdn, state_gdn_conv)
    return run_trunk(x_prompt, x_sample, p_prompt, p_sample, states, weights, route_tm=640)
```

```python
import functools
import math

import jax
import jax.numpy as jnp
from jax import lax
from jax.experimental import pallas as pl
from jax.experimental.pallas import tpu as pltpu

F32 = jnp.float32
BF16 = jnp.bfloat16
I32 = jnp.int32

D_MODEL = 2048
DEPTH = 4
GLA_HEADS, GLA_DK, GLA_DV = 4, 128, 256
GLA_KW, GLA_VW, GLA_RANK, GLA_TAU = 512, 1024, 16, 16.0
S5_WIDTH, S5_CH, S5_GROUPS, S5_STATE = 1024, 16, 64, 64
GDN_HEADS, GDN_DK, GDN_DV = 8, 128, 128
GDN_KW, GDN_VW, GDN_CONV = 1024, 1024, 4
N_BRANCH = 3
IN_SIZES = (GLA_KW, GLA_KW, GLA_VW, GLA_VW, GLA_RANK, S5_WIDTH, GDN_KW, GDN_KW, GDN_VW, GDN_VW, GDN_HEADS,
            GDN_HEADS, N_BRANCH * D_MODEL)
CHUNK = 64
N_EXPERTS, TOP_K, N_GROUPS, TOPK_GROUPS = 64, 8, 8, 4
D_EXPERT = 512
ROUTED_SCALE = 2.5
LN_EPS = 1e-5
NORM_EPS = 1e-6
DEEPNORM_ALPHA = (2 * DEPTH) ** 0.25

LANES = 128
VMEM_LIMIT_BYTES = 56 * 1024 * 1024
S5_CS = 16
S5_TILES = S5_WIDTH // LANES
S5_TSTATE = (LANES // S5_CH) * S5_STATE
MOE_ROWS = 256
MOE_TOK_TILE = 128


def _cparams(sem):
    return pltpu.CompilerParams(dimension_semantics=sem, vmem_limit_bytes=VMEM_LIMIT_BYTES)


def _pick_tile(n, candidates):
    for c in candidates:
        if n % c == 0:
            return c
    return n


def _sigmoid(x):
    return 1.0 / (1.0 + jnp.exp(-x))


def _silu(x):
    return x * _sigmoid(x)


def _gelu_tanh(x):
    return 0.5 * x * (1.0 + jnp.tanh(math.sqrt(2.0 / math.pi) * (x + 0.044715 * (x * x * x))))


def _log_sigmoid(x):
    return jnp.minimum(x, 0.0) - jnp.log1p(jnp.exp(-jnp.abs(x)))


def _bdot(a, b):
    return jnp.dot(a.astype(BF16), b.astype(BF16), preferred_element_type=F32)


def dense(x, w, *, bias=None, act=None, out_dtype=F32, tm=None, tn=None):
    m, k = x.shape
    n = w.shape[1]
    tm = tm or _pick_tile(m, (1664, 640, 512, 256, 128, 64, 32, 16, 8))
    tn = tn or _pick_tile(n, (512, 256, 128))

    def body(x_ref, w_ref, *rest):
        o_ref = rest[-1]
        y = _bdot(x_ref[...], w_ref[...])
        if bias is not None:
            y = y + rest[0][...]
        if act == 'log_decay':
            y = _log_sigmoid(y) / GLA_TAU
        o_ref[...] = y.astype(o_ref.dtype)

    in_specs = [pl.BlockSpec((tm, k), lambda i, j: (i, 0)), pl.BlockSpec((k, tn), lambda i, j: (0, j))]
    args = [x, w]
    if bias is not None:
        in_specs.append(pl.BlockSpec((1, tn), lambda i, j: (0, j)))
        args.append(bias.reshape(1, n))
    return pl.pallas_call(
        body, grid=(m // tm, n // tn), in_specs=in_specs,
        out_specs=pl.BlockSpec((tm, tn), lambda i, j: (i, j)),
        out_shape=jax.ShapeDtypeStruct((m, n), out_dtype),
        compiler_params=_cparams(("parallel", "parallel")), name="dense")(*args)


def swiglu_hidden(x, w1, w3):
    m, k = x.shape
    n = w1.shape[1]
    tm = _pick_tile(m, (1664, 640, 512, 256, 128, 64, 32, 16, 8))
    tn = _pick_tile(n, (512, 256, 128))

    def body(x_ref, w1_ref, w3_ref, o_ref):
        xb = x_ref[...].astype(BF16)
        a = jnp.dot(xb, w1_ref[...].astype(BF16), preferred_element_type=F32)
        b = jnp.dot(xb, w3_ref[...].astype(BF16), preferred_element_type=F32)
        o_ref[...] = (_silu(a) * b).astype(o_ref.dtype)

    return pl.pallas_call(
        body, grid=(m // tm, n // tn),
        in_specs=[pl.BlockSpec((tm, k), lambda i, j: (i, 0)), pl.BlockSpec((k, tn), lambda i, j: (0, j)),
                  pl.BlockSpec((k, tn), lambda i, j: (0, j))],
        out_specs=pl.BlockSpec((tm, tn), lambda i, j: (i, j)),
        out_shape=jax.ShapeDtypeStruct((m, n), BF16),
        compiler_params=_cparams(("parallel", "parallel")), name="swiglu_hidden")(x, w1, w3)


def glu_gate(y, w, b):
    m, n = y.shape
    tm = _pick_tile(m, (1664, 640, 512, 256, 128, 64, 32, 16, 8))
    tn = _pick_tile(n, (512, 256, 128))

    def body(y_ref, yt_ref, w_ref, b_ref, o_ref):
        g = _bdot(y_ref[...], w_ref[...]) + b_ref[...]
        o_ref[...] = (yt_ref[...] * _sigmoid(g)).astype(o_ref.dtype)

    return pl.pallas_call(
        body, grid=(m // tm, n // tn),
        in_specs=[pl.BlockSpec((tm, n), lambda i, j: (i, 0)), pl.BlockSpec((tm, tn), lambda i, j: (i, j)),
                  pl.BlockSpec((n, tn), lambda i, j: (0, j)), pl.BlockSpec((1, tn), lambda i, j: (0, j))],
        out_specs=pl.BlockSpec((tm, tn), lambda i, j: (i, j)),
        out_shape=jax.ShapeDtypeStruct((m, n), BF16),
        compiler_params=_cparams(("parallel", "parallel")), name="glu_gate")(y, y, w, b.reshape(1, n))


def merge_branches(xb, w_gates, br_a, br_s, br_c, w_a, w_s, w_c):
    m, k = xb.shape
    d = w_a.shape[1]
    kb = br_a.shape[1]
    tm = _pick_tile(m, (832, 640, 512, 256, 128, 64, 32, 16, 8))
    tn = _pick_tile(d, (256, 128))
    nj = d // tn

    def body(x_ref, g0_ref, g1_ref, g2_ref, a_ref, s_ref, c_ref, wa_ref, ws_ref, wc_ref, o_ref):
        x = x_ref[...]
        acc = _sigmoid(_bdot(x, g0_ref[...])) * _bdot(a_ref[...], wa_ref[...])
        acc = acc + _sigmoid(_bdot(x, g1_ref[...])) * _bdot(s_ref[...], ws_ref[...])
        acc = acc + _sigmoid(_bdot(x, g2_ref[...])) * _bdot(c_ref[...], wc_ref[...])
        o_ref[...] = acc.astype(o_ref.dtype)

    def gate_spec(b):
        return pl.BlockSpec((k, tn), lambda i, j: (0, b * nj + j))

    act_spec = pl.BlockSpec((tm, kb), lambda i, j: (i, 0))
    w_spec = pl.BlockSpec((kb, tn), lambda i, j: (0, j))
    return pl.pallas_call(
        body, grid=(m // tm, nj),
        in_specs=[pl.BlockSpec((tm, k), lambda i, j: (i, 0)), gate_spec(0), gate_spec(1), gate_spec(2),
                  act_spec, act_spec, act_spec, w_spec, w_spec, w_spec],
        out_specs=pl.BlockSpec((tm, tn), lambda i, j: (i, j)),
        out_shape=jax.ShapeDtypeStruct((m, d), BF16),
        compiler_params=_cparams(("parallel", "parallel")), name="merge_branches")(
            xb, w_gates, w_gates, w_gates, br_a, br_s, br_c, w_a, w_s, w_c)


def _layer_norm_rows(y, g, b):
    mu = jnp.mean(y, axis=-1, keepdims=True)
    yc = y - mu
    var = jnp.mean(yc * yc, axis=-1, keepdims=True)
    return yc * lax.rsqrt(var + LN_EPS) * g + b


def out_proj_ln(merged, w_out, x, g, b):
    m, k = merged.shape
    d = w_out.shape[1]
    tm = _pick_tile(m, (416, 256, 128, 64, 32, 16))

    def body(m_ref, w_ref, x_ref, g_ref, b_ref, o_ref, ob_ref):
        y = DEEPNORM_ALPHA * x_ref[...] + _bdot(m_ref[...], w_ref[...])
        y = _layer_norm_rows(y, g_ref[...], b_ref[...])
        o_ref[...] = y
        ob_ref[...] = y.astype(BF16)

    row = pl.BlockSpec((1, d), lambda i: (0, 0))
    return pl.pallas_call(
        body, grid=(m // tm,),
        in_specs=[pl.BlockSpec((tm, k), lambda i: (i, 0)), pl.BlockSpec((k, d), lambda i: (0, 0)),
                  pl.BlockSpec((tm, d), lambda i: (i, 0)), row, row],
        out_specs=[pl.BlockSpec((tm, d), lambda i: (i, 0)), pl.BlockSpec((tm, d), lambda i: (i, 0))],
        out_shape=[jax.ShapeDtypeStruct((m, d), F32), jax.ShapeDtypeStruct((m, d), BF16)],
        compiler_params=_cparams(("parallel",)), name="out_proj_ln")(
            merged, w_out, x, g.reshape(1, d), b.reshape(1, d))


def ple_mix(x, xb, w_gate, pe, w_proj):
    m, d = x.shape
    kp = pe.shape[1]
    tm = _pick_tile(m, (1664, 640, 512, 256, 128, 64, 32, 16))
    tn = _pick_tile(d, (512, 256, 128))

    def body(xb_ref, wg_ref, pe_ref, wp_ref, x_ref, o_ref, ob_ref):
        y = x_ref[...] + _sigmoid(_bdot(xb_ref[...], wg_ref[...])) * _bdot(pe_ref[...], wp_ref[...])
        o_ref[...] = y
        ob_ref[...] = y.astype(BF16)

    return pl.pallas_call(
        body, grid=(m // tm, d // tn),
        in_specs=[pl.BlockSpec((tm, d), lambda i, j: (i, 0)), pl.BlockSpec((d, tn), lambda i, j: (0, j)),
                  pl.BlockSpec((tm, kp), lambda i, j: (i, 0)), pl.BlockSpec((kp, tn), lambda i, j: (0, j)),
                  pl.BlockSpec((tm, tn), lambda i, j: (i, j))],
        out_specs=[pl.BlockSpec((tm, tn), lambda i, j: (i, j)), pl.BlockSpec((tm, tn), lambda i, j: (i, j))],
        out_shape=[jax.ShapeDtypeStruct((m, d), F32), jax.ShapeDtypeStruct((m, d), BF16)],
        compiler_params=_cparams(("parallel", "parallel")), name="ple_mix")(xb, w_gate, pe, w_proj, x)


def s5_tables(lam_re, lam_im, log_dt, b_re, b_im, c_re, c_im):
    hp = lax.Precision.HIGHEST
    cs, nt, gl = S5_CS, S5_TILES, LANES // S5_CH
    dt = jnp.exp(log_dt)[:, None]
    mag = jnp.exp(lam_re * dt)
    ab_re, ab_im = mag * jnp.cos(lam_im * dt), mag * jnp.sin(lam_im * dt)
    den = lam_re * lam_re + lam_im * lam_im
    nr = ab_re - 1.0
    co_re = (nr * lam_re + ab_im * lam_im) / den
    co_im = (ab_im * lam_re - nr * lam_im) / den
    bb_re = co_re[..., None] * b_re - co_im[..., None] * b_im
    bb_im = co_re[..., None] * b_im + co_im[..., None] * b_re
    pr, pi = [jnp.ones_like(ab_re)], [jnp.zeros_like(ab_im)]
    for _ in range(cs):
        pr.append(pr[-1] * ab_re - pi[-1] * ab_im)
        pi.append(pr[-2] * ab_im + pi[-1] * ab_re)
    ap_re, ap_im = jnp.stack(pr), jnp.stack(pi)
    abr = ap_re[:, :, :, None] * bb_re - ap_im[:, :, :, None] * bb_im
    abi = ap_re[:, :, :, None] * bb_im + ap_im[:, :, :, None] * bb_re
    kern = (jnp.einsum('gcp,egpd->egcd', c_re, abr[:cs], precision=hp)
            - jnp.einsum('gcp,egpd->egcd', c_im, abi[:cs], precision=hp))
    same_group = jnp.eye(gl, dtype=bool)

    def block_diag(a, g_axis, h_axis):
        shape = [1] * (a.ndim + 1)
        shape[g_axis if g_axis < h_axis else g_axis + 1] = gl
        shape[h_axis] = gl
        return jnp.where(same_group.reshape(shape), jnp.expand_dims(a, h_axis), 0.0).astype(BF16)

    k5 = kern.reshape(cs, nt, gl, S5_CH, S5_CH).transpose(1, 0, 2, 4, 3)
    kcat = block_diag(k5, 2, 4).reshape(nt, cs, LANES, LANES).transpose(0, 2, 1, 3).reshape(nt, LANES, cs * LANES)
    toep = jnp.concatenate([jnp.pad(kcat[:, :, :(cs - s) * LANES], ((0, 0), (0, 0), (s * LANES, 0)))
                            for s in range(cs)], axis=1)
    er = abr[:cs][::-1].reshape(cs, nt, gl, S5_STATE, S5_CH).transpose(1, 0, 2, 4, 3)
    ei = abi[:cs][::-1].reshape(cs, nt, gl, S5_STATE, S5_CH).transpose(1, 0, 2, 4, 3)
    bend = block_diag(jnp.stack([er, ei], axis=4), 2, 5)
    bend = bend.reshape(nt, cs * LANES, 2 * S5_TSTATE)
    car = c_re[None] * ap_re[:, :, None, :] - c_im[None] * ap_im[:, :, None, :]
    cai = -(c_re[None] * ap_im[:, :, None, :] + c_im[None] * ap_re[:, :, None, :])
    car = car.reshape(cs + 1, nt, gl, S5_CH, S5_STATE).transpose(1, 2, 4, 0, 3)
    cai = cai.reshape(cs + 1, nt, gl, S5_CH, S5_STATE).transpose(1, 2, 4, 0, 3)
    ccar = block_diag(jnp.stack([car, cai], axis=1), 2, 5)
    ccar = ccar.reshape(nt, 2 * S5_TSTATE, (cs + 1) * LANES)

    def state_row(re, im):
        return jnp.concatenate([re.reshape(nt, 1, S5_TSTATE), im.reshape(nt, 1, S5_TSTATE)], axis=-1)

    return dict(toep=toep.astype(BF16), bend=bend.astype(BF16), bbar=bend[:, (cs - 1) * LANES:].astype(BF16),
                c0=ccar[:, :, :LANES].astype(BF16), ccar=ccar[:, :, LANES:].astype(BF16),
                a1=state_row(ap_re[1], ap_im[1]), acs=state_row(ap_re[cs], ap_im[cs]))


def _chunk_rows(u_ref, nc):
    return jnp.concatenate([u_ref[pl.ds(s, nc, stride=S5_CS), :].astype(BF16) for s in range(S5_CS)], axis=1)


def s5_chunk_states(u_src, bend, *, t_p):
    nt, kc, n = bend.shape
    nc = t_p // S5_CS
    tn = 512

    def body(u_ref, b_ref, o_ref):
        o_ref[0] = jnp.dot(_chunk_rows(u_ref, nc), b_ref[0], preferred_element_type=F32)

    return pl.pallas_call(
        body, grid=(nt, n // tn),
        in_specs=[pl.BlockSpec((t_p, LANES), lambda j, n_: (0, j)),
                  pl.BlockSpec((1, kc, tn), lambda j, n_: (j, 0, n_))],
        out_specs=pl.BlockSpec((1, nc, tn), lambda j, n_: (j, 0, n_)),
        out_shape=jax.ShapeDtypeStruct((nt, nc, n), F32),
        compiler_params=_cparams(("parallel", "parallel")), name="s5_chunk_states")(u_src, bend)


def s5_carry_scan(xe, acs, n_seq):
    nt, nc, n = xe.shape
    per = nc // n_seq
    half = n // 2

    def body(x_ref, a_ref, hp_ref, hf_ref):
        ar = a_ref[0, :, :half]
        ai = a_ref[0, :, half:]

        def step(k, carry):
            hr, hi = carry
            hp_ref[0, pl.ds(k, 1), :] = jnp.concatenate([hr, hi], axis=1)
            x = x_ref[0, pl.ds(k, 1), :]
            return (ar * hr - ai * hi + x[:, :half], ar * hi + ai * hr + x[:, half:])

        zero = jnp.zeros((1, half), F32)
        hr, hi = lax.fori_loop(0, per, step, (zero, zero))
        hf_ref[0, 0] = jnp.concatenate([hr, hi], axis=1)

    return pl.pallas_call(
        body, grid=(nt, n_seq),
        in_specs=[pl.BlockSpec((1, per, n), lambda j, b: (j, b, 0)), pl.BlockSpec((1, 1, n), lambda j, b: (j, 0, 0))],
        out_specs=[pl.BlockSpec((1, per, n), lambda j, b: (j, b, 0)),
                   pl.BlockSpec((1, 1, 1, n), lambda j, b: (j, b, 0, 0))],
        out_shape=[jax.ShapeDtypeStruct((nt, nc, n), F32), jax.ShapeDtypeStruct((nt, n_seq, 1, n), F32)],
        compiler_params=_cparams(("parallel", "parallel")), name="s5_carry_scan")(xe, acs)


def s5_outputs(u_src, toep, hprev, ccar, d_skip, *, t_p):
    nt, kc, _ = toep.shape
    ns = hprev.shape[2]
    nc = t_p // S5_CS
    tn = 512
    per = tn // LANES

    def body(u_ref, t_ref, h_ref, c_ref, d_ref, o_ref):
        uc = _chunk_rows(u_ref, nc)
        hb = h_ref[0].astype(BF16)
        for n_ in range(kc // tn):
            cols = slice(n_ * tn, (n_ + 1) * tn)
            y = (jnp.dot(uc, t_ref[0, :, cols], preferred_element_type=F32)
                 + jnp.dot(hb, c_ref[0, :, cols], preferred_element_type=F32))
            for i in range(per):
                rows = pl.ds(n_ * per + i, nc, stride=S5_CS)
                o_ref[rows, :] = _gelu_tanh(y[:, i * LANES:(i + 1) * LANES] + d_ref[...] * u_ref[rows, :])

    return pl.pallas_call(
        body, grid=(nt,),
        in_specs=[pl.BlockSpec((t_p, LANES), lambda j: (0, j)),
                  pl.BlockSpec((1, kc, kc), lambda j: (j, 0, 0)), pl.BlockSpec((1, nc, ns), lambda j: (j, 0, 0)),
                  pl.BlockSpec((1, ns, kc), lambda j: (j, 0, 0)), pl.BlockSpec((1, LANES), lambda j: (0, j))],
        out_specs=pl.BlockSpec((t_p, LANES), lambda j: (0, j)),
        out_shape=jax.ShapeDtypeStruct((t_p, nt * LANES), F32),
        compiler_params=_cparams(("parallel",)), name="s5_outputs")(
            u_src, toep, hprev, ccar, d_skip.reshape(1, nt * LANES))


def s5_decode(u_src, h_re, h_im, bbar, a1, c0, d_skip, *, row0):
    s = h_re.shape[0]
    nt = bbar.shape[0]
    w = nt * LANES
    ts = S5_TSTATE
    rb = row0 // s

    def body(u_ref, hr_ref, hi_ref, b_ref, a_ref, c_ref, d_ref, y_ref, nr_ref, ni_ref):
        uu = u_ref[...]
        x = _bdot(uu, b_ref[0])
        ar, ai = a_ref[0, :, :ts], a_ref[0, :, ts:]
        hr, hi = hr_ref[...], hi_ref[...]
        nr = ar * hr - ai * hi + x[:, :ts]
        ni = ar * hi + ai * hr + x[:, ts:]
        nr_ref[...] = nr
        ni_ref[...] = ni
        y = _bdot(jnp.concatenate([nr, ni], axis=1), c_ref[0]) + d_ref[...] * uu
        y_ref[...] = _gelu_tanh(y)

    col = pl.BlockSpec((s, LANES), lambda j: (0, j))
    st = pl.BlockSpec((s, ts), lambda j: (0, j))
    return pl.pallas_call(
        body, grid=(nt,),
        in_specs=[pl.BlockSpec((s, LANES), lambda j: (rb, j)), st, st,
                  pl.BlockSpec((1, LANES, 2 * ts), lambda j: (j, 0, 0)),
                  pl.BlockSpec((1, 1, 2 * ts), lambda j: (j, 0, 0)), pl.BlockSpec((1, 2 * ts, LANES), lambda j: (j, 0, 0)),
                  pl.BlockSpec((1, LANES), lambda j: (0, j))],
        out_specs=[col, st, st],
        out_shape=[jax.ShapeDtypeStruct((s, w), F32), jax.ShapeDtypeStruct(h_re.shape, F32),
                   jax.ShapeDtypeStruct(h_im.shape, F32)],
        compiler_params=_cparams(("parallel",)), name="s5_decode")(u_src, h_re, h_im, bbar, a1, c0, d_skip.reshape(1, w))


def s5_branch(u_src, n_seq, seq_len, h_re, h_im, tabs, d_skip):
    t_p = n_seq * seq_len
    nt = S5_TILES
    xe = s5_chunk_states(u_src, tabs['bend'], t_p=t_p)
    hprev, hfin = s5_carry_scan(xe, tabs['acs'], n_seq)
    y_p = s5_outputs(u_src, tabs['toep'], hprev, tabs['ccar'], d_skip, t_p=t_p)
    hfin = hfin.reshape(nt, n_seq, 2, S5_TSTATE).transpose(2, 1, 0, 3).reshape(2, n_seq, S5_GROUPS, S5_STATE)
    s_rows = u_src.shape[0] - t_p
    y_s, nr, ni = s5_decode(u_src, h_re.reshape(s_rows, -1), h_im.reshape(s_rows, -1), tabs['bbar'],
                            tabs['a1'], tabs['c0'], d_skip, row0=t_p)
    return (jnp.concatenate([y_p, y_s], axis=0), hfin[0], hfin[1], nr.reshape(h_re.shape), ni.reshape(h_im.shape))


def moe_route(xb, w_router_t, b_router, *, tm):
    t, d = xb.shape
    ne, ng, gs = N_EXPERTS, N_GROUPS, N_EXPERTS // N_GROUPS
    neg = -jnp.inf

    def body(x_ref, w_ref, b_ref, u_ref, idx_ref, wt_ref, rank_ref, cnt_ref, carry_ref):
        @pl.when(pl.program_id(0) == 0)
        def _():
            carry_ref[...] = jnp.zeros_like(carry_ref)

        logits = lax.dot_general(w_ref[...], x_ref[...], (((1,), (1,)), ((), ())), preferred_element_type=F32)
        scores = _sigmoid(logits).reshape(ng, gs, tm)
        choice = scores + b_ref[...].reshape(ng, gs, 1)
        e_in = lax.broadcasted_iota(I32, (ng, gs, tm), 1).astype(F32)
        g_id = lax.broadcasted_iota(I32, (ng, 1, tm), 0).astype(F32)
        e_id = g_id * gs + e_in
        m1 = jnp.max(choice, axis=1, keepdims=True)
        i1 = jnp.min(jnp.where(choice == m1, e_in, float(gs)), axis=1, keepdims=True)
        m2 = jnp.max(jnp.where(e_in == i1, neg, choice), axis=1, keepdims=True)
        gscore = m1 + m2
        keep = jnp.zeros((ng, 1, tm), F32)
        for _ in range(TOPK_GROUPS):
            gm = jnp.max(gscore, axis=0, keepdims=True)
            gi = jnp.min(jnp.where(gscore == gm, g_id, float(ng)), axis=0, keepdims=True)
            hit = g_id == gi
            keep = jnp.where(hit, 1.0, keep)
            gscore = jnp.where(hit, neg, gscore)
        cand = jnp.where(keep > 0.0, choice, neg)
        member = jnp.zeros((ng, gs, tm), F32)
        picks, wts = [], []
        for _ in range(TOP_K):
            cm = jnp.max(jnp.max(cand, axis=1, keepdims=True), axis=0, keepdims=True)
            ei = jnp.min(jnp.min(jnp.where(cand == cm, e_id, float(ne)), axis=1, keepdims=True), axis=0, keepdims=True)
            sel = e_id == ei
            wts.append(jnp.sum(jnp.sum(jnp.where(sel, scores, 0.0), axis=1, keepdims=True), axis=0, keepdims=True))
            picks.append(ei)
            member = jnp.where(sel, 1.0, member)
            cand = jnp.where(sel, neg, cand)
        wsum = wts[0]
        for w in wts[1:]:
            wsum = wsum + w
        member2 = member.reshape(ne, tm)
        prefix = jnp.dot(member2.astype(BF16), u_ref[...], preferred_element_type=F32) + carry_ref[:, 0:1]
        prefix = prefix.reshape(ng, gs, tm)
        for j in range(TOP_K):
            sel = e_id == picks[j]
            rk = jnp.sum(jnp.sum(jnp.where(sel, prefix, 0.0), axis=1, keepdims=True), axis=0, keepdims=True)
            idx_ref[j:j + 1, :] = picks[j].reshape(1, tm).astype(I32)
            rank_ref[j:j + 1, :] = rk.reshape(1, tm).astype(I32)
            wt_ref[j:j + 1, :] = (wts[j] / wsum * ROUTED_SCALE).reshape(1, tm)
        carry_ref[...] = carry_ref[...] + jnp.sum(member2, axis=1, keepdims=True)
        cnt_ref[...] = carry_ref[...]

    upper = jnp.triu(jnp.ones((tm, tm), F32), 1).astype(BF16)
    tok = pl.BlockSpec((TOP_K, tm), lambda i: (0, i))
    idx, wt, rank, cnt = pl.pallas_call(
        body, grid=(t // tm,),
        in_specs=[pl.BlockSpec((tm, d), lambda i: (i, 0)), pl.BlockSpec((ne, d), lambda i: (0, 0)),
                  pl.BlockSpec((ne, 1), lambda i: (0, 0)), pl.BlockSpec((tm, tm), lambda i: (0, 0))],
        out_specs=[tok, tok, tok, pl.BlockSpec((ne, LANES), lambda i: (0, 0))],
        out_shape=[jax.ShapeDtypeStruct((TOP_K, t), I32), jax.ShapeDtypeStruct((TOP_K, t), F32),
                   jax.ShapeDtypeStruct((TOP_K, t), I32), jax.ShapeDtypeStruct((ne, LANES), F32)],
        scratch_shapes=[pltpu.VMEM((ne, LANES), F32)],
        compiler_params=_cparams(("arbitrary",)), name="moe_route")(xb, w_router_t, b_router.reshape(ne, 1), upper)
    return idx, wt, rank, cnt[:, 0]


def moe_experts(x, row_tok, block_e, n_used, w1, w3, w2, *, rows, layer):
    t, d = x.shape
    nb = row_tok.shape[0]
    f = w1.shape[3]

    def body(be_ref, nu_ref, x_hbm, tok_ref, tokn_ref, w1_ref, w3_ref, w2_ref, o_ref, buf, sem, w1b, w3b, w2b):
        i = pl.program_id(0)
        slot = lax.rem(i, 2)
        nxt = 1 - slot
        groups = 4
        per = rows // groups

        def row_copy(tref, r, sl):
            tok = tref[0, 0, r]
            return pltpu.make_async_copy(x_hbm.at[pl.ds(tok, 1), :], buf.at[sl, pl.ds(r, 1), :], sem.at[sl])

        def gather_loop(tref, sl):
            def issue(r, c):
                row_copy(tref, r, sl).start(priority=1)
                return c
            lax.fori_loop(0, rows, issue, 0, unroll=8)

        def gather_group(tref, sl, g):
            for r in range(g * per, (g + 1) * per):
                row_copy(tref, r, sl).start(priority=1)

        def block_wait(sl):
            pltpu.make_async_copy(x_hbm.at[pl.ds(0, rows), :], buf.at[sl], sem.at[sl]).wait()

        @pl.when(i == 0)
        def _():
            gather_loop(tok_ref, 0)

        e = be_ref[i]
        changed = jnp.logical_or(i == 0, e != be_ref[jnp.maximum(i - 1, 0)])

        @pl.when(changed)
        def _():
            w1b[...] = w1_ref[0, 0].astype(BF16)
            w3b[...] = w3_ref[0, 0].astype(BF16)
            w2b[...] = w2_ref[0, 0].astype(BF16)

        block_wait(slot)

        @pl.when(i < nu_ref[0])
        def _():
            xb = buf[slot].astype(BF16)
            gather_group(tokn_ref, nxt, 0)
            h1 = jnp.dot(xb, w1b[...], preferred_element_type=F32)
            gather_group(tokn_ref, nxt, 1)
            h3 = jnp.dot(xb, w3b[...], preferred_element_type=F32)
            gather_group(tokn_ref, nxt, 2)
            y = jnp.dot((_silu(h1) * h3).astype(BF16), w2b[...], preferred_element_type=F32)
            gather_group(tokn_ref, nxt, 3)
            o_ref[...] = y

        @pl.when(i >= nu_ref[0])
        def _():
            gather_loop(tokn_ref, nxt)
            o_ref[...] = jnp.zeros_like(o_ref)

        @pl.when(i == nb - 1)
        def _():
            block_wait(nxt)

    grid_spec = pltpu.PrefetchScalarGridSpec(
        num_scalar_prefetch=2, grid=(nb,),
        in_specs=[pl.BlockSpec(memory_space=pl.ANY),
                  pl.BlockSpec((1, 1, rows), lambda i, be, nu: (i, 0, 0), memory_space=pltpu.SMEM),
                  pl.BlockSpec((1, 1, rows), lambda i, be, nu: (jnp.minimum(i + 1, nb - 1), 0, 0),
                               memory_space=pltpu.SMEM),
                  pl.BlockSpec((1, 1, d, f), lambda i, be, nu: (layer, be[i], 0, 0)),
                  pl.BlockSpec((1, 1, d, f), lambda i, be, nu: (layer, be[i], 0, 0)),
                  pl.BlockSpec((1, 1, f, d), lambda i, be, nu: (layer, be[i], 0, 0))],
        out_specs=pl.BlockSpec((rows, d), lambda i, be, nu: (i, 0)),
        scratch_shapes=[pltpu.VMEM((2, rows, d), F32), pltpu.SemaphoreType.DMA((2,)),
                        pltpu.VMEM((d, f), BF16), pltpu.VMEM((d, f), BF16), pltpu.VMEM((f, d), BF16)])
    return pl.pallas_call(
        body, grid_spec=grid_spec, out_shape=jax.ShapeDtypeStruct((nb * rows, d), F32),
        compiler_params=_cparams(("arbitrary",)), name="moe_experts")(block_e, n_used, x, row_tok, row_tok, w1, w3, w2)


def moe_combine_ln(ys, dest, wts, x, shared, g, b, *, tm):
    t, d = x.shape
    nt = t // tm

    def body(ys_hbm, d_ref, dn_ref, w_ref, x_ref, s_ref, g_ref, b_ref, o_ref, ob_ref, buf, sem):
        i = pl.program_id(0)
        slot = lax.rem(i, 2)

        def gather(dref, sl):
            def issue(r, c):
                row = dref[0, 0, r]
                pltpu.make_async_copy(ys_hbm.at[pl.ds(row, 1), :], buf.at[sl, pl.ds(r, 1), :],
                                      sem.at[sl]).start(priority=1)
                return c
            lax.fori_loop(0, TOP_K * tm, issue, 0, unroll=8)

        @pl.when(i == 0)
        def _():
            gather(d_ref, 0)

        @pl.when(i + 1 < nt)
        def _():
            gather(dn_ref, 1 - slot)

        pltpu.make_async_copy(ys_hbm.at[pl.ds(0, TOP_K * tm), :], buf.at[slot], sem.at[slot]).wait()
        w = w_ref[...]
        acc = DEEPNORM_ALPHA * x_ref[...] + s_ref[...]
        for j in range(TOP_K):
            acc = acc + w[:, j:j + 1] * buf[slot, j * tm:(j + 1) * tm, :]
        y = _layer_norm_rows(acc, g_ref[...], b_ref[...])
        o_ref[...] = y
        ob_ref[...] = y.astype(BF16)

    row = pl.BlockSpec((1, d), lambda i: (0, 0))
    tile = pl.BlockSpec((tm, d), lambda i: (i, 0))
    return pl.pallas_call(
        body, grid=(nt,),
        in_specs=[pl.BlockSpec(memory_space=pl.ANY),
                  pl.BlockSpec((1, 1, TOP_K * tm), lambda i: (i, 0, 0), memory_space=pltpu.SMEM),
                  pl.BlockSpec((1, 1, TOP_K * tm), lambda i: (jnp.minimum(i + 1, nt - 1), 0, 0),
                               memory_space=pltpu.SMEM),
                  pl.BlockSpec((tm, TOP_K), lambda i: (i, 0)), tile, tile, row, row],
        out_specs=[tile, tile],
        out_shape=[jax.ShapeDtypeStruct((t, d), F32), jax.ShapeDtypeStruct((t, d), BF16)],
        scratch_shapes=[pltpu.VMEM((2, TOP_K * tm, d), F32), pltpu.SemaphoreType.DMA((2,))],
        compiler_params=_cparams(("arbitrary",)), name="moe_combine_ln")(
            ys, dest, dest, wts, x, shared, g.reshape(1, d), b.reshape(1, d))


def moe_layer(x, xb, lw, *, route_tm, rows=MOE_ROWS, tok_tile=MOE_TOK_TILE):
    t, d = x.shape
    idx, wt, rank, counts = moe_route(xb, lw['moe_w_router'].T.astype(BF16), lw['moe_b_router'], tm=route_tm)
    counts = counts.astype(I32)
    padded = (counts + rows - 1) // rows * rows
    pad_end = jnp.cumsum(padded)
    pad_start = pad_end - padded
    onehot = idx[:, :, None] == jnp.arange(N_EXPERTS, dtype=I32)[None, None, :]
    dest = jnp.sum(jnp.where(onehot, pad_start[None, None, :], 0), axis=-1) + rank
    n_rows = -(-(t * TOP_K + N_EXPERTS * (rows - 1)) // rows) * rows
    nb = n_rows // rows
    tok_id = jnp.broadcast_to(jnp.arange(t, dtype=I32)[None, :], (TOP_K, t))
    row_tok = jnp.zeros((n_rows,), I32).at[dest.reshape(-1)].set(tok_id.reshape(-1))
    blk_start = jnp.arange(nb, dtype=I32) * rows
    block_e = jnp.minimum(jnp.sum((pad_end[None, :] <= blk_start[:, None]).astype(I32), axis=1), N_EXPERTS - 1)
    n_used = (pad_end[-1] // rows).astype(I32).reshape(1)
    ys = moe_experts(x, row_tok.reshape(nb, 1, rows), block_e, n_used, lw['moe_w1'], lw['moe_w3'], lw['moe_w2'],
                     rows=rows, layer=lw['layer'])
    hs = swiglu_hidden(xb, lw['moe_ws1'], lw['moe_ws3'])
    shared = dense(hs, lw['moe_ws2'])
    dest_t = dest.reshape(TOP_K, t // tok_tile, tok_tile).transpose(1, 0, 2).reshape(t // tok_tile, 1, TOP_K * tok_tile)
    return moe_combine_ln(ys, dest_t, wt.T, x, shared, lw['ln2_g'], lw['ln2_b'], tm=tok_tile)


def _level_tables(c):
    import numpy as np
    idx = np.arange(c)
    t, r = idx[:, None], idx[None, :]
    wl, pm = [], []
    b = 1
    while b < c:
        blk, odd = t // b, (t // b) % 2 == 1
        w = np.where(odd, (r >= blk * b) & (r <= t), (r > t) & (r <= blk * b + b - 1))
        wl.append(w.astype(np.float32))
        pm.append((odd & (r // b == blk - 1)).astype(np.float32))
        b *= 2
    incl = (r <= t).astype(np.float32)
    after = (r > t).astype(np.float32)
    return wl, pm, incl, after


def _split3(x):
    hi = x.astype(BF16)
    r1 = x - hi.astype(F32)
    mid = r1.astype(BF16)
    lo = (r1 - mid.astype(F32)).astype(BF16)
    return hi, mid, lo


def _table_dot(tab, x):
    hi, mid, lo = _split3(x)
    return (jnp.dot(tab, hi, preferred_element_type=F32) + jnp.dot(tab, mid, preferred_element_type=F32)
            + jnp.dot(tab, lo, preferred_element_type=F32))


def _dot_hi(a, b):
    ah = a.astype(BF16)
    al = (a - ah.astype(F32)).astype(BF16)
    bh = b.astype(BF16)
    bl = (b - bh.astype(F32)).astype(BF16)
    return (jnp.dot(ah, bh, preferred_element_type=F32) + jnp.dot(ah, bl, preferred_element_type=F32)
            + jnp.dot(al, bh, preferred_element_type=F32))


def _dot_nt(a, b):
    return lax.dot_general(a.astype(BF16), b.astype(BF16), (((1,), (1,)), ((), ())), preferred_element_type=F32)


def _dot_tn(a, b):
    return lax.dot_general(a.astype(BF16), b.astype(BF16), (((0,), (0,)), ((), ())), preferred_element_type=F32)


def _rms_rows(o, w):
    return o * lax.rsqrt(jnp.mean(o * o, axis=-1, keepdims=True) + NORM_EPS) * w


def gla_prompt(z1, log_a, norm_w, *, n_seq, seq_len):
    c, h_, dk, dv = CHUNK, GLA_HEADS, GLA_DK, GLA_DV
    nck = seq_len // c
    wl, pm, incl, after = _level_tables(c)
    nl = len(wl)
    wcat = jnp.asarray(jnp.concatenate([jnp.asarray(w) for w in wl] + [jnp.asarray(incl), jnp.asarray(after)], axis=0),
                       BF16)
    pmask = jnp.stack([jnp.eye(c, dtype=F32)] + [jnp.asarray(p) for p in pm])
    scale = dk ** -0.5

    def body(q_ref, k_ref, v_ref, r_ref, g_ref, w_ref, p_ref, n_ref, o_ref, st_ref, s_scr):
        ci = pl.program_id(1)

        @pl.when(ci == 0)
        def _():
            s_scr[...] = jnp.zeros_like(s_scr)

        x = _table_dot(w_ref[...], g_ref[...])
        ex = jnp.exp(x)
        for h in range(h_):
            ks = slice(h * dk, (h + 1) * dk)
            vs = slice(h * dv, (h + 1) * dv)
            q = q_ref[:, ks] * scale
            k = k_ref[:, ks]
            v = v_ref[:, vs]
            scores = p_ref[0] * _dot_nt(q, k)
            for l in range(nl):
                f = ex[l * c:(l + 1) * c, ks]
                scores = scores + p_ref[l + 1] * _dot_nt(q * f, k * f)
            st = s_scr[h]
            o = _dot_nt(q * ex[nl * c:(nl + 1) * c, ks], st) + _bdot(scores, v)
            tot = x[(nl + 1) * c - 1:(nl + 1) * c, ks]
            s_scr[h] = jnp.exp(tot) * st + _dot_tn(v, k * ex[(nl + 1) * c:(nl + 2) * c, ks])
            o_ref[:, vs] = (_rms_rows(o, n_ref[...]) * _silu(r_ref[:, vs])).astype(o_ref.dtype)

        @pl.when(ci == nck - 1)
        def _():
            st_ref[0] = s_scr[...]

    def rows(width, col):
        return pl.BlockSpec((c, width), lambda b, i: (b * nck + i, col))

    return pl.pallas_call(
        body, grid=(n_seq, nck),
        in_specs=[rows(h_ * dk, 0), rows(h_ * dk, 1), rows(h_ * dv, 1), rows(h_ * dv, 2), rows(h_ * dk, 0),
                  pl.BlockSpec(wcat.shape, lambda b, i: (0, 0)), pl.BlockSpec(pmask.shape, lambda b, i: (0, 0, 0)),
                  pl.BlockSpec((1, dv), lambda b, i: (0, 0))],
        out_specs=[rows(h_ * dv, 0), pl.BlockSpec((1, h_, dv, dk), lambda b, i: (b, 0, 0, 0))],
        out_shape=[jax.ShapeDtypeStruct((n_seq * seq_len, h_ * dv), BF16),
                   jax.ShapeDtypeStruct((n_seq, h_, dv, dk), F32)],
        scratch_shapes=[pltpu.VMEM((h_, dv, dk), F32)],
        compiler_params=_cparams(("parallel", "arbitrary")), name="gla_prompt")(
            z1, z1, z1, z1, log_a, wcat, pmask, norm_w.reshape(1, dv))


def _columns(rows_list, width):
    used = sum(r.shape[0] for r in rows_list)
    stack = jnp.concatenate(list(rows_list) + [jnp.zeros((LANES - used, width), F32)], axis=0)
    return stack.T


def gla_decode(z1, log_a, state, norm_w, *, row0, layer):
    sb = 8
    n_s = state.shape[1]
    h_, dk, dv = GLA_HEADS, GLA_DK, GLA_DV
    scale = dk ** -0.5
    r0 = row0 // sb

    def body(q_ref, k_ref, v_ref, r_ref, g_ref, s_ref, n_ref, o_ref, ns_ref):
        for h in range(h_):
            ks = slice(h * dk, (h + 1) * dk)
            vs = slice(h * dv, (h + 1) * dv)
            cols = _columns([jnp.exp(g_ref[:, ks]), k_ref[:, ks], q_ref[:, ks] * scale], dk)
            v = v_ref[:, vs]
            outs = []
            for s in range(sb):
                s_new = cols[:, s:s + 1] * s_ref[0, s, h] + cols[:, sb + s:sb + s + 1] * v[s:s + 1, :]
                ns_ref[s, h] = s_new
                outs.append(jnp.sum(cols[:, 2 * sb + s:2 * sb + s + 1] * s_new, axis=0, keepdims=True))
            o = jnp.concatenate(outs, axis=0)
            o_ref[:, vs] = _rms_rows(o, n_ref[...]) * _silu(r_ref[:, vs])

    def rows(width, col):
        return pl.BlockSpec((sb, width), lambda i: (r0 + i, col))

    st_in = pl.BlockSpec((1, sb, h_, dk, dv), lambda i: (layer, i, 0, 0, 0))
    st = pl.BlockSpec((sb, h_, dk, dv), lambda i: (i, 0, 0, 0))
    return pl.pallas_call(
        body, grid=(n_s // sb,),
        in_specs=[rows(h_ * dk, 0), rows(h_ * dk, 1), rows(h_ * dv, 1), rows(h_ * dv, 2), rows(h_ * dk, 0), st_in,
                  pl.BlockSpec((1, dv), lambda i: (0, 0))],
        out_specs=[pl.BlockSpec((sb, h_ * dv), lambda i: (i, 0)), st],
        out_shape=[jax.ShapeDtypeStruct((n_s, h_ * dv), F32), jax.ShapeDtypeStruct(state.shape[1:], F32)],
        compiler_params=_cparams(("parallel",)), name="gla_decode")(
            z1, z1, z1, z1, log_a, state, norm_w.reshape(1, dv))


def _conv_silu(ext, w, c):
    acc = ext[5:5 + c] * w[0:1]
    for i in range(1, GDN_CONV):
        acc = acc + ext[5 + i:5 + i + c] * w[i:i + 1]
    return _silu(acc)


def _softplus(x):
    return jnp.maximum(x, 0.0) + jnp.log1p(jnp.exp(-jnp.abs(x)))


def _l2n(x):
    return x * lax.rsqrt(jnp.sum(x * x, axis=-1, keepdims=True) + NORM_EPS)


def gdn_prompt(z3, z4, conv_w, a_log, dt_bias, norm_w, *, n_seq, seq_len):
    c, h_, dk, dv = CHUNK, GDN_HEADS, GDN_DK, GDN_DV
    kw = h_ * dk
    nck = seq_len // c
    _, pm, incl, after = _level_tables(c)
    nl = len(pm)
    import numpy as np
    strict = (np.arange(c)[:, None] > np.arange(c)[None, :]).astype(np.float32)
    tabs = jnp.asarray(np.concatenate([incl, after, np.ones((c, c), np.float32)], axis=0), BF16)
    masks = jnp.stack([jnp.asarray(incl), jnp.asarray(strict), jnp.eye(c, dtype=F32)] + [jnp.asarray(p) for p in pm])
    strict_pad = jnp.asarray(np.concatenate([strict, np.zeros((c, LANES - c), np.float32)], axis=1))
    qscale = dk ** -0.5

    def body(q_ref, k_ref, v_ref, zg_ref, ab_ref, cw_ref, al_ref, db_ref, t_ref, m_ref, sp_ref, n_ref,
             o_ref, st_ref, s_scr, hist):
        ci = pl.program_id(1)

        @pl.when(ci == 0)
        def _():
            s_scr[...] = jnp.zeros_like(s_scr)
            hist[...] = jnp.zeros_like(hist)

        def conv(ref, j):
            cols = slice(j * kw, (j + 1) * kw)
            raw = ref[...]
            ext = jnp.concatenate([hist[:, cols], raw], axis=0)
            y = _conv_silu(ext, cw_ref[:, cols], c)
            hist[:, cols] = raw[c - 8:c]
            return y

        qc, kc, vc = conv(q_ref, 0), conv(k_ref, 1), conv(v_ref, 2)
        ab = ab_ref[...]
        g = -jnp.exp(al_ref[...]) * _softplus(ab[:, :h_] + db_ref[...])
        beta = _sigmoid(ab[:, h_:])
        sums = _table_dot(t_ref[...], jnp.concatenate([g, jnp.zeros((c, LANES - h_), F32)], axis=1))
        e_cum = jnp.exp(sums[0:c])
        e_rest = jnp.exp(sums[c:2 * c])
        e_last = jnp.exp(sums[2 * c:2 * c + 1])
        grel = jnp.concatenate([g[:, h:h + 1] * sp_ref[...] for h in range(h_)], axis=1)
        rel = _table_dot(t_ref[0:c], grel)
        m_incl, m_strict, m_eye = m_ref[0], m_ref[1], m_ref[2]
        qs, ks, vs, kbs, decs, amat, tinv = [], [], [], [], [], [], []
        for h in range(h_):
            hs = slice(h * dk, (h + 1) * dk)
            qs.append(_l2n(qc[:, hs]) * qscale)
            ks.append(_l2n(kc[:, hs]))
            vs.append(vc[:, hs])
            decs.append(m_incl * jnp.exp(m_incl * rel[:, h * LANES:h * LANES + c]))
            kbs.append(ks[h] * beta[:, h:h + 1])
            amat.append(m_strict * _dot_nt(kbs[h], ks[h]) * decs[h])
            tinv.append(m_eye - m_ref[3] * amat[h])
        for l in range(1, nl):
            tinv = [tinv[h] - _dot_hi(_dot_hi(tinv[h], m_ref[3 + l] * amat[h]), tinv[h]) for h in range(h_)]
        for h in range(h_):
            hs = slice(h * dk, (h + 1) * dk)
            q, k, v, kb, dec, t = qs[h], ks[h], vs[h], kbs[h], decs[h], tinv[h]
            bcol = beta[:, h:h + 1]
            tw = _bdot(t, jnp.concatenate([kb * e_cum[:, h:h + 1], v * bcol], axis=1))
            s_old = s_scr[h]
            both = _bdot(jnp.concatenate([q * e_cum[:, h:h + 1], tw[:, :dk]], axis=0), s_old)
            u = tw[:, dk:] - both[c:]
            o = both[:c] + _bdot(_dot_nt(q, k) * dec, u)
            s_scr[h] = e_last[:, h:h + 1] * s_old + _dot_tn(k * e_rest[:, h:h + 1], u)
            o_ref[:, hs] = (_rms_rows(o, n_ref[...]) * _silu(zg_ref[:, hs])).astype(o_ref.dtype)

        @pl.when(ci == nck - 1)
        def _():
            st_ref[0] = s_scr[...]

    def rows(width, col):
        return pl.BlockSpec((c, width), lambda b, i: (b * nck + i, col))

    def const(arr):
        nd = arr.ndim
        return pl.BlockSpec(arr.shape, lambda b, i: (0,) * nd)

    cw = conv_w
    al = a_log.reshape(1, h_)
    db = dt_bias.reshape(1, h_)
    nw = norm_w.reshape(1, dv)
    return pl.pallas_call(
        body, grid=(n_seq, nck),
        in_specs=[rows(kw, 1), rows(kw, 2), rows(kw, 3), rows(kw, 4), pl.BlockSpec((c, 2 * h_), lambda b, i: (b * nck + i, 0)),
                  const(cw), const(al), const(db), const(tabs), const(masks), const(strict_pad), const(nw)],
        out_specs=[rows(kw, 0), pl.BlockSpec((1, h_, dk, dv), lambda b, i: (b, 0, 0, 0))],
        out_shape=[jax.ShapeDtypeStruct((n_seq * seq_len, kw), BF16), jax.ShapeDtypeStruct((n_seq, h_, dk, dv), F32)],
        scratch_shapes=[pltpu.VMEM((h_, dk, dv), F32), pltpu.VMEM((8, 3 * kw), F32)],
        compiler_params=_cparams(("parallel", "arbitrary")), name="gdn_prompt")(
            z3, z3, z3, z3, z4, cw, al, db, tabs, masks, strict_pad, nw)


def gdn_decode(z3, z4, state, conv_buf, conv_w, a_log, dt_bias, norm_w, *, row0, layer):
    sb = 8
    n_s = state.shape[1]
    h_, dk, dv = GDN_HEADS, GDN_DK, GDN_DV
    kw = h_ * dk
    r0 = row0 // sb
    qscale = dk ** -0.5

    def body(q_ref, k_ref, v_ref, zg_ref, ab_ref, hb_ref, cw_ref, al_ref, db_ref, s_ref, n_ref, o_ref, ns_ref):
        def conv(ref, j):
            cols = slice(j * kw, (j + 1) * kw)
            acc = ref[...] * cw_ref[GDN_CONV - 1:GDN_CONV, cols]
            for i in range(GDN_CONV - 1):
                acc = acc + hb_ref[:, i, cols] * cw_ref[i:i + 1, cols]
            return _silu(acc)

        qc, kc, vc = conv(q_ref, 0), conv(k_ref, 1), conv(v_ref, 2)
        ab = ab_ref[...]
        eg = jnp.exp(-jnp.exp(al_ref[...]) * _softplus(ab[:, :h_] + db_ref[...]))
        beta = _sigmoid(ab[:, h_:])
        for h in range(h_):
            hs = slice(h * dk, (h + 1) * dk)
            q = _l2n(qc[:, hs]) * qscale
            k = _l2n(kc[:, hs])
            v = vc[:, hs]
            cols = _columns([k, q], dk)
            qk = jnp.sum(q * k, axis=-1, keepdims=True)
            outs = []
            for s in range(sb):
                s_old = s_ref[0, s, h]
                kcol = cols[:, s:s + 1]
                k_s = jnp.sum(kcol * s_old, axis=0, keepdims=True)
                q_s = jnp.sum(cols[:, sb + s:sb + s + 1] * s_old, axis=0, keepdims=True)
                e = eg[s:s + 1, h:h + 1]
                u = beta[s:s + 1, h:h + 1] * (v[s:s + 1, :] - e * k_s)
                ns_ref[s, h] = e * s_old + kcol * u
                outs.append(e * q_s + qk[s:s + 1, :] * u)
            o = jnp.concatenate(outs, axis=0)
            o_ref[:, hs] = _rms_rows(o, n_ref[...]) * _silu(zg_ref[:, hs])

    def rows(width, col):
        return pl.BlockSpec((sb, width), lambda i: (r0 + i, col))

    def const(arr):
        nd = arr.ndim
        return pl.BlockSpec(arr.shape, lambda i: (0,) * nd)

    st_in = pl.BlockSpec((1, sb, h_, dk, dv), lambda i: (layer, i, 0, 0, 0))
    st = pl.BlockSpec((sb, h_, dk, dv), lambda i: (i, 0, 0, 0))
    al = a_log.reshape(1, h_)
    db = dt_bias.reshape(1, h_)
    nw = norm_w.reshape(1, dv)
    return pl.pallas_call(
        body, grid=(n_s // sb,),
        in_specs=[rows(kw, 1), rows(kw, 2), rows(kw, 3), rows(kw, 4), pl.BlockSpec((sb, 2 * h_), lambda i: (r0 + i, 0)),
                  pl.BlockSpec((sb, GDN_CONV - 1, 3 * kw), lambda i: (i, 0, 0)), const(conv_w), const(al), const(db),
                  st_in, const(nw)],
        out_specs=[pl.BlockSpec((sb, kw), lambda i: (i, 0)), st],
        out_shape=[jax.ShapeDtypeStruct((n_s, kw), F32), jax.ShapeDtypeStruct(state.shape[1:], F32)],
        compiler_params=_cparams(("parallel",)), name="gdn_decode")(
            z3, z3, z3, z3, z4, conv_buf, conv_w, al, db, state, nw)


def trunk_layer(x, xb, pe_b, states, lw, *, n_seq, seq_len, route_tm):
    t_p = n_seq * seq_len
    n_s = x.shape[0] - t_p
    gla_s, s5_re, s5_im, gdn_s, conv_s = states
    w_in = lw['w_in']
    z1 = dense(xb, w_in[:, 0:3072].astype(BF16))
    a_lr = dense(xb, w_in[:, 3072:3088].astype(BF16))
    z3 = dense(xb, w_in[:, 3088:8208].astype(BF16))
    z4 = dense(xb, w_in[:, 8208:8224].astype(BF16))
    log_a = dense(a_lr, lw['gla_w_gate'], bias=lw['gla_b_gate'], act='log_decay')
    br_a_p, gla_pt = gla_prompt(z1, log_a, lw['gla_norm'], n_seq=n_seq, seq_len=seq_len)
    gla_p = jnp.swapaxes(gla_pt, 2, 3)
    br_a_s, gla_n = gla_decode(z1, log_a, gla_s, lw['gla_norm'], row0=t_p, layer=lw['layer'])
    br_a = jnp.concatenate([br_a_p, br_a_s.astype(BF16)], axis=0)
    tabs = s5_tables(lw['s5_lam_re'], lw['s5_lam_im'], lw['s5_log_dt'], lw['s5_b_re'], lw['s5_b_im'],
                     lw['s5_c_re'], lw['s5_c_im'])
    y_s, s5r_p, s5i_p, s5r_n, s5i_n = s5_branch(z3, n_seq, seq_len, s5_re, s5_im, tabs, lw['s5_d'])
    br_s = glu_gate(y_s, lw['s5_w_glu'].astype(BF16), lw['s5_b_glu'])
    br_c_p, gdn_p = gdn_prompt(z3, z4, lw['gdn_conv_w'], lw['gdn_a_log'], lw['gdn_dt_bias'], lw['gdn_norm'],
                               n_seq=n_seq, seq_len=seq_len)
    br_c_s, gdn_n = gdn_decode(z3, z4, gdn_s, conv_s, lw['gdn_conv_w'], lw['gdn_a_log'], lw['gdn_dt_bias'],
                               lw['gdn_norm'], row0=t_p, layer=lw['layer'])
    br_c = jnp.concatenate([br_c_p, br_c_s.astype(BF16)], axis=0)
    qkv_cols = slice(S5_WIDTH, S5_WIDTH + 2 * GDN_KW + GDN_VW)
    conv_p = jnp.stack([z3[(b + 1) * seq_len - (GDN_CONV - 1):(b + 1) * seq_len, qkv_cols] for b in range(n_seq)])
    conv_n = jnp.concatenate([conv_s[:, 1:], z3[t_p:, None, qkv_cols]], axis=1)
    merged = merge_branches(xb, w_in[:, 8224:].astype(BF16), br_a, br_s, br_c, lw['w_branch_a'].astype(BF16),
                            lw['w_branch_s'].astype(BF16), lw['w_branch_c'].astype(BF16))
    x1, x1b = out_proj_ln(merged, lw['w_out'].astype(BF16), x, lw['ln1_g'], lw['ln1_b'])
    x2, x2b = moe_layer(x1, x1b, lw, route_tm=route_tm)
    x3, x3b = ple_mix(x2, x2b, lw['ple_w_gate'].astype(BF16), pe_b, lw['ple_w_proj'].astype(BF16))
    return x3, x3b, (gla_p, s5r_p, s5i_p, gdn_p, conv_p), (gla_n, s5r_n, s5i_n, gdn_n, conv_n)


_NAMES = ('w_in', 'gla_w_gate', 'gla_b_gate', 'gla_norm', 's5_lam_re', 's5_lam_im', 's5_log_dt', 's5_b_re',
          's5_b_im', 's5_c_re', 's5_c_im', 's5_d', 's5_w_glu', 's5_b_glu', 'gdn_conv_w', 'gdn_a_log',
          'gdn_dt_bias', 'gdn_norm', 'w_branch_a', 'w_branch_s', 'w_branch_c', 'w_out', 'ln1_g', 'ln1_b',
          'ln2_g', 'ln2_b', 'moe_w_router', 'moe_b_router', 'moe_w1', 'moe_w3', 'moe_w2', 'moe_ws1', 'moe_ws3',
          'moe_ws2', 'ple_w_proj', 'ple_w_gate')


_STACKED = ('moe_w1', 'moe_w3', 'moe_w2')


def run_trunk(x_prompt, x_sample, p_prompt, p_sample, states, weights, *, route_tm):
    n_seq, seq_len, d = x_prompt.shape
    n_s = x_sample.shape[0]
    t_p = n_seq * seq_len
    depth = weights[0].shape[0]
    x = jnp.concatenate([x_prompt.reshape(t_p, d), x_sample.reshape(n_s, d)], axis=0)
    xb = x.astype(BF16)
    pe = jnp.concatenate([p_prompt.reshape(depth, t_p, -1), p_sample.reshape(depth, n_s, -1)], axis=1).astype(BF16)
    new_p, new_s = [], []
    for i in range(depth):
        lw = {n: (w if n in _STACKED else w[i]) for n, w in zip(_NAMES, weights)}
        lw['layer'] = i
        st = (states[0], states[1][i], states[2][i], states[3], states[4][i])
        x, xb, st_p, st_s = trunk_layer(x, xb, pe[i], st, lw, n_seq=n_seq, seq_len=seq_len, route_tm=route_tm)
        new_p.append(st_p)
        new_s.append(st_s)
    gla_p, s5r_p, s5i_p, gdn_p, conv_p = (jnp.stack(f) for f in zip(*new_p))
    gla_s, s5r_s, s5i_s, gdn_s, conv_s = (jnp.stack(f) for f in zip(*new_s))
    yp = x[:t_p].reshape(n_seq, seq_len, d)
    ys = x[t_p:].reshape(n_s, 1, d)
    return (yp, ys, gla_p, gla_s, s5r_p, s5r_s, s5i_p, s5i_s, gdn_p, gdn_s, conv_p, conv_s)


def kernel(x_prompt, x_sample, p_prompt, p_sample, state_gla, state_s5_re, state_s5_im, state_gdn, state_gdn_conv,
           w_in, gla_w_gate, gla_b_gate, gla_norm, s5_lam_re, s5_lam_im, s5_log_dt, s5_b_re, s5_b_im, s5_c_re,
           s5_c_im, s5_d, s5_w_glu, s5_b_glu, gdn_conv_w, gdn_a_log, gdn_dt_bias, gdn_norm, w_branch_a,
           w_branch_s, w_branch_c, w_out, ln1_g, ln1_b, ln2_g, ln2_b, moe_w_router, moe_b_router, moe_w1, moe_w3,
           moe_w2, moe_ws1, moe_ws3, moe_ws2, ple_w_proj, ple_w_gate):
    weights = (w_in, gla_w_gate, gla_b_gate, gla_norm, s5_lam_re, s5_lam_im, s5_log_dt, s5_b_re, s5_b_im, s5_c_re,
               s5_c_im, s5_d, s5_w_glu, s5_b_glu, gdn_conv_w, gdn_a_log, gdn_dt_bias, gdn_norm, w_branch_a,
               w_branch_s, w_branch_c, w_out, ln1_g, ln1_b, ln2_g, ln2_b, moe_w_router, moe_b_router, moe_w1,
               moe_w3, moe_w2, moe_ws1, moe_ws3, moe_ws2, ple_w_proj, ple_w_gate)
    states = (state_gla, state_s5_re, state_s5_im, state_gdn, state_gdn_conv)
    return run_trunk(x_prompt, x_sample, p_prompt, p_sample, states, weights, route_tm=640)
```

```python
import functools
import math

import jax
import jax.numpy as jnp
from jax import lax
from jax.experimental import pallas as pl
from jax.experimental.pallas import tpu as pltpu

F32 = jnp.float32
BF16 = jnp.bfloat16
I32 = jnp.int32

D_MODEL = 2048
DEPTH = 4
GLA_HEADS, GLA_DK, GLA_DV = 4, 128, 256
GLA_KW, GLA_VW, GLA_RANK, GLA_TAU = 512, 1024, 16, 16.0
S5_WIDTH, S5_CH, S5_GROUPS, S5_STATE = 1024, 16, 64, 64
GDN_HEADS, GDN_DK, GDN_DV = 8, 128, 128
GDN_KW, GDN_VW, GDN_CONV = 1024, 1024, 4
N_BRANCH = 3
IN_SIZES = (GLA_KW, GLA_KW, GLA_VW, GLA_VW, GLA_RANK, S5_WIDTH, GDN_KW, GDN_KW, GDN_VW, GDN_VW, GDN_HEADS,
            GDN_HEADS, N_BRANCH * D_MODEL)
CHUNK = 64
N_EXPERTS, TOP_K, N_GROUPS, TOPK_GROUPS = 64, 8, 8, 4
D_EXPERT = 512
ROUTED_SCALE = 2.5
LN_EPS = 1e-5
NORM_EPS = 1e-6
DEEPNORM_ALPHA = (2 * DEPTH) ** 0.25

LANES = 128
VMEM_LIMIT_BYTES = 56 * 1024 * 1024
S5_CS = 16
S5_TILES = S5_WIDTH // LANES
S5_TSTATE = (LANES // S5_CH) * S5_STATE
MOE_ROWS = 256
MOE_TOK_TILE = 128


def _cparams(sem):
    return pltpu.CompilerParams(dimension_semantics=sem, vmem_limit_bytes=VMEM_LIMIT_BYTES)


def _pick_tile(n, candidates):
    for c in candidates:
        if n % c == 0:
            return c
    return n


def _sigmoid(x):
    return 1.0 / (1.0 + jnp.exp(-x))


def _silu(x):
    return x * _sigmoid(x)


def _gelu_tanh(x):
    return 0.5 * x * (1.0 + jnp.tanh(math.sqrt(2.0 / math.pi) * (x + 0.044715 * (x * x * x))))


def _log_sigmoid(x):
    return jnp.minimum(x, 0.0) - jnp.log1p(jnp.exp(-jnp.abs(x)))


def _bdot(a, b):
    return jnp.dot(a.astype(BF16), b.astype(BF16), preferred_element_type=F32)


def dense(x, w, *, bias=None, act=None, out_dtype=F32, tm=None, tn=None):
    m, k = x.shape
    n = w.shape[1]
    tm = tm or _pick_tile(m, (1664, 640, 512, 256, 128, 64, 32, 16, 8))
    tn = tn or _pick_tile(n, (512, 256, 128))

    def body(x_ref, w_ref, *rest):
        o_ref = rest[-1]
        y = _bdot(x_ref[...], w_ref[...])
        if bias is not None:
            y = y + rest[0][...]
        if act == 'log_decay':
            y = _log_sigmoid(y) / GLA_TAU
        o_ref[...] = y.astype(o_ref.dtype)

    in_specs = [pl.BlockSpec((tm, k), lambda i, j: (i, 0)), pl.BlockSpec((k, tn), lambda i, j: (0, j))]
    args = [x, w]
    if bias is not None:
        in_specs.append(pl.BlockSpec((1, tn), lambda i, j: (0, j)))
        args.append(bias.reshape(1, n))
    return pl.pallas_call(
        body, grid=(m // tm, n // tn), in_specs=in_specs,
        out_specs=pl.BlockSpec((tm, tn), lambda i, j: (i, j)),
        out_shape=jax.ShapeDtypeStruct((m, n), out_dtype),
        compiler_params=_cparams(("parallel", "parallel")), name="dense")(*args)


def swiglu_hidden(x, w1, w3):
    m, k = x.shape
    n = w1.shape[1]
    tm = _pick_tile(m, (1664, 640, 512, 256, 128, 64, 32, 16, 8))
    tn = _pick_tile(n, (512, 256, 128))

    def body(x_ref, w1_ref, w3_ref, o_ref):
        xb = x_ref[...].astype(BF16)
        a = jnp.dot(xb, w1_ref[...].astype(BF16), preferred_element_type=F32)
        b = jnp.dot(xb, w3_ref[...].astype(BF16), preferred_element_type=F32)
        o_ref[...] = (_silu(a) * b).astype(o_ref.dtype)

    return pl.pallas_call(
        body, grid=(m // tm, n // tn),
        in_specs=[pl.BlockSpec((tm, k), lambda i, j: (i, 0)), pl.BlockSpec((k, tn), lambda i, j: (0, j)),
                  pl.BlockSpec((k, tn), lambda i, j: (0, j))],
        out_specs=pl.BlockSpec((tm, tn), lambda i, j: (i, j)),
        out_shape=jax.ShapeDtypeStruct((m, n), BF16),
        compiler_params=_cparams(("parallel", "parallel")), name="swiglu_hidden")(x, w1, w3)


def glu_gate(y, w, b):
    m, n = y.shape
    tm = _pick_tile(m, (1664, 640, 512, 256, 128, 64, 32, 16, 8))
    tn = _pick_tile(n, (512, 256, 128))

    def body(y_ref, yt_ref, w_ref, b_ref, o_ref):
        g = _bdot(y_ref[...], w_ref[...]) + b_ref[...]
        o_ref[...] = (yt_ref[...] * _sigmoid(g)).astype(o_ref.dtype)

    return pl.pallas_call(
        body, grid=(m // tm, n // tn),
        in_specs=[pl.BlockSpec((tm, n), lambda i, j: (i, 0)), pl.BlockSpec((tm, tn), lambda i, j: (i, j)),
                  pl.BlockSpec((n, tn), lambda i, j: (0, j)), pl.BlockSpec((1, tn), lambda i, j: (0, j))],
        out_specs=pl.BlockSpec((tm, tn), lambda i, j: (i, j)),
        out_shape=jax.ShapeDtypeStruct((m, n), BF16),
        compiler_params=_cparams(("parallel", "parallel")), name="glu_gate")(y, y, w, b.reshape(1, n))


def merge_branches(xb, w_gates, br_a, br_s, br_c, w_a, w_s, w_c):
    m, k = xb.shape
    d = w_a.shape[1]
    kb = br_a.shape[1]
    tm = _pick_tile(m, (1664, 832, 640, 512, 256, 128, 64, 32, 16, 8))
    tn = _pick_tile(d, (256, 128))
    nj = d // tn

    def body(x_ref, g0_ref, g1_ref, g2_ref, a_ref, s_ref, c_ref, wa_ref, ws_ref, wc_ref, o_ref):
        x = x_ref[...]
        acc = _sigmoid(_bdot(x, g0_ref[...])) * _bdot(a_ref[...], wa_ref[...])
        acc = acc + _sigmoid(_bdot(x, g1_ref[...])) * _bdot(s_ref[...], ws_ref[...])
        acc = acc + _sigmoid(_bdot(x, g2_ref[...])) * _bdot(c_ref[...], wc_ref[...])
        o_ref[...] = acc.astype(o_ref.dtype)

    def gate_spec(b):
        return pl.BlockSpec((k, tn), lambda i, j: (0, b * nj + j))

    act_spec = pl.BlockSpec((tm, kb), lambda i, j: (i, 0))
    w_spec = pl.BlockSpec((kb, tn), lambda i, j: (0, j))
    return pl.pallas_call(
        body, grid=(m // tm, nj),
        in_specs=[pl.BlockSpec((tm, k), lambda i, j: (i, 0)), gate_spec(0), gate_spec(1), gate_spec(2),
                  act_spec, act_spec, act_spec, w_spec, w_spec, w_spec],
        out_specs=pl.BlockSpec((tm, tn), lambda i, j: (i, j)),
        out_shape=jax.ShapeDtypeStruct((m, d), BF16),
        compiler_params=_cparams(("parallel", "parallel")), name="merge_branches")(
            xb, w_gates, w_gates, w_gates, br_a, br_s, br_c, w_a, w_s, w_c)


def _layer_norm_rows(y, g, b):
    mu = jnp.mean(y, axis=-1, keepdims=True)
    yc = y - mu
    var = jnp.mean(yc * yc, axis=-1, keepdims=True)
    return yc * lax.rsqrt(var + LN_EPS) * g + b


def out_proj_ln(merged, w_out, x, g, b):
    m, k = merged.shape
    d = w_out.shape[1]
    tm = _pick_tile(m, (416, 256, 128, 64, 32, 16))

    def body(m_ref, w_ref, x_ref, g_ref, b_ref, o_ref, ob_ref):
        y = DEEPNORM_ALPHA * x_ref[...] + _bdot(m_ref[...], w_ref[...])
        y = _layer_norm_rows(y, g_ref[...], b_ref[...])
        o_ref[...] = y
        ob_ref[...] = y.astype(BF16)

    row = pl.BlockSpec((1, d), lambda i: (0, 0))
    return pl.pallas_call(
        body, grid=(m // tm,),
        in_specs=[pl.BlockSpec((tm, k), lambda i: (i, 0)), pl.BlockSpec((k, d), lambda i: (0, 0)),
                  pl.BlockSpec((tm, d), lambda i: (i, 0)), row, row],
        out_specs=[pl.BlockSpec((tm, d), lambda i: (i, 0)), pl.BlockSpec((tm, d), lambda i: (i, 0))],
        out_shape=[jax.ShapeDtypeStruct((m, d), F32), jax.ShapeDtypeStruct((m, d), BF16)],
        compiler_params=_cparams(("parallel",)), name="out_proj_ln")(
            merged, w_out, x, g.reshape(1, d), b.reshape(1, d))


def ple_mix(x, xb, w_gate, pe, w_proj):
    m, d = x.shape
    kp = pe.shape[1]
    tm = _pick_tile(m, (1664, 640, 512, 256, 128, 64, 32, 16))
    tn = _pick_tile(d, (512, 256, 128))

    def body(xb_ref, wg_ref, pe_ref, wp_ref, x_ref, o_ref, ob_ref):
        y = x_ref[...] + _sigmoid(_bdot(xb_ref[...], wg_ref[...])) * _bdot(pe_ref[...], wp_ref[...])
        o_ref[...] = y
        ob_ref[...] = y.astype(BF16)

    return pl.pallas_call(
        body, grid=(m // tm, d // tn),
        in_specs=[pl.BlockSpec((tm, d), lambda i, j: (i, 0)), pl.BlockSpec((d, tn), lambda i, j: (0, j)),
                  pl.BlockSpec((tm, kp), lambda i, j: (i, 0)), pl.BlockSpec((kp, tn), lambda i, j: (0, j)),
                  pl.BlockSpec((tm, tn), lambda i, j: (i, j))],
        out_specs=[pl.BlockSpec((tm, tn), lambda i, j: (i, j)), pl.BlockSpec((tm, tn), lambda i, j: (i, j))],
        out_shape=[jax.ShapeDtypeStruct((m, d), F32), jax.ShapeDtypeStruct((m, d), BF16)],
        compiler_params=_cparams(("parallel", "parallel")), name="ple_mix")(xb, w_gate, pe, w_proj, x)


def s5_tables(lam_re, lam_im, log_dt, b_re, b_im, c_re, c_im):
    hp = lax.Precision.HIGHEST
    cs, nt, gl = S5_CS, S5_TILES, LANES // S5_CH
    dt = jnp.exp(log_dt)[:, None]
    mag = jnp.exp(lam_re * dt)
    ab_re, ab_im = mag * jnp.cos(lam_im * dt), mag * jnp.sin(lam_im * dt)
    den = lam_re * lam_re + lam_im * lam_im
    nr = ab_re - 1.0
    co_re = (nr * lam_re + ab_im * lam_im) / den
    co_im = (ab_im * lam_re - nr * lam_im) / den
    bb_re = co_re[..., None] * b_re - co_im[..., None] * b_im
    bb_im = co_re[..., None] * b_im + co_im[..., None] * b_re
    pr, pi = [jnp.ones_like(ab_re)], [jnp.zeros_like(ab_im)]
    for _ in range(cs):
        pr.append(pr[-1] * ab_re - pi[-1] * ab_im)
        pi.append(pr[-2] * ab_im + pi[-1] * ab_re)
    ap_re, ap_im = jnp.stack(pr), jnp.stack(pi)
    abr = ap_re[:, :, :, None] * bb_re - ap_im[:, :, :, None] * bb_im
    abi = ap_re[:, :, :, None] * bb_im + ap_im[:, :, :, None] * bb_re
    kern = (jnp.einsum('gcp,egpd->egcd', c_re, abr[:cs], precision=hp)
            - jnp.einsum('gcp,egpd->egcd', c_im, abi[:cs], precision=hp))
    same_group = jnp.eye(gl, dtype=bool)

    def block_diag(a, g_axis, h_axis):
        shape = [1] * (a.ndim + 1)
        shape[g_axis if g_axis < h_axis else g_axis + 1] = gl
        shape[h_axis] = gl
        return jnp.where(same_group.reshape(shape), jnp.expand_dims(a, h_axis), 0.0).astype(BF16)

    k5 = kern.reshape(cs, nt, gl, S5_CH, S5_CH).transpose(1, 0, 2, 4, 3)
    kcat = block_diag(k5, 2, 4).reshape(nt, cs, LANES, LANES).transpose(0, 2, 1, 3).reshape(nt, LANES, cs * LANES)
    toep = jnp.concatenate([jnp.pad(kcat[:, :, :(cs - s) * LANES], ((0, 0), (0, 0), (s * LANES, 0)))
                            for s in range(cs)], axis=1)
    def state_lanes(re, im, n_rows):
        full = jnp.concatenate([jnp.tile(re, (1, 1, 1, 1, gl)), jnp.tile(im, (1, 1, 1, 1, gl))], axis=-1)
        full = full.reshape(nt, n_rows, 2 * S5_TSTATE)
        row_g = (jnp.arange(n_rows) // S5_CH) % gl
        col_h = (jnp.arange(2 * S5_TSTATE) // S5_STATE) % gl
        return jnp.where(row_g[:, None] == col_h[None, :], full, 0.0).astype(BF16)

    er = abr[:cs][::-1].reshape(cs, nt, gl, S5_STATE, S5_CH).transpose(1, 0, 2, 4, 3)
    ei = abi[:cs][::-1].reshape(cs, nt, gl, S5_STATE, S5_CH).transpose(1, 0, 2, 4, 3)
    bend = state_lanes(er, ei, cs * LANES)
    car = c_re[None] * ap_re[:, :, None, :] - c_im[None] * ap_im[:, :, None, :]
    cai = -(c_re[None] * ap_im[:, :, None, :] + c_im[None] * ap_re[:, :, None, :])
    car = car.reshape(cs + 1, nt, gl, S5_CH, S5_STATE).transpose(1, 0, 2, 3, 4)
    cai = cai.reshape(cs + 1, nt, gl, S5_CH, S5_STATE).transpose(1, 0, 2, 3, 4)
    ccar_t = state_lanes(car, cai, (cs + 1) * LANES)

    def state_row(re, im):
        return jnp.concatenate([re.reshape(nt, 1, S5_TSTATE), im.reshape(nt, 1, S5_TSTATE)], axis=-1)

    return dict(toep=toep, bend=bend, bbar=bend[:, (cs - 1) * LANES:], c0_t=ccar_t[:, :LANES],
                ccar_t=ccar_t[:, LANES:], a1=state_row(ap_re[1], ap_im[1]), acs=state_row(ap_re[cs], ap_im[cs]))


def _chunk_rows(u_ref, nc):
    return jnp.concatenate([u_ref[pl.ds(s, nc, stride=S5_CS), :].astype(BF16) for s in range(S5_CS)], axis=1)


def s5_chunk_states(u_src, bend, *, t_p):
    nt, kc, n = bend.shape
    nc = t_p // S5_CS
    tn = 512

    def body(u_ref, b_ref, o_ref):
        o_ref[0] = jnp.dot(_chunk_rows(u_ref, nc), b_ref[0], preferred_element_type=F32)

    return pl.pallas_call(
        body, grid=(nt, n // tn),
        in_specs=[pl.BlockSpec((t_p, LANES), lambda j, n_: (0, j)),
                  pl.BlockSpec((1, kc, tn), lambda j, n_: (j, 0, n_))],
        out_specs=pl.BlockSpec((1, nc, tn), lambda j, n_: (j, 0, n_)),
        out_shape=jax.ShapeDtypeStruct((nt, nc, n), F32),
        compiler_params=_cparams(("parallel", "parallel")), name="s5_chunk_states")(u_src, bend)


def s5_carry_scan(xe, acs, n_seq):
    nt, nc, n = xe.shape
    per = nc // n_seq
    half = n // 2

    def body(x_ref, a_ref, hp_ref, hf_ref):
        ar = a_ref[0, :, :half]
        ai = a_ref[0, :, half:]

        def step(k, carry):
            hr, hi = carry
            hp_ref[0, pl.ds(k, 1), :] = jnp.concatenate([hr, hi], axis=1)
            x = x_ref[0, pl.ds(k, 1), :]
            return (ar * hr - ai * hi + x[:, :half], ar * hi + ai * hr + x[:, half:])

        zero = jnp.zeros((1, half), F32)
        hr, hi = lax.fori_loop(0, per, step, (zero, zero))
        hf_ref[0, 0] = jnp.concatenate([hr, hi], axis=1)

    return pl.pallas_call(
        body, grid=(nt, n_seq),
        in_specs=[pl.BlockSpec((1, per, n), lambda j, b: (j, b, 0)), pl.BlockSpec((1, 1, n), lambda j, b: (j, 0, 0))],
        out_specs=[pl.BlockSpec((1, per, n), lambda j, b: (j, b, 0)),
                   pl.BlockSpec((1, 1, 1, n), lambda j, b: (j, b, 0, 0))],
        out_shape=[jax.ShapeDtypeStruct((nt, nc, n), F32), jax.ShapeDtypeStruct((nt, n_seq, 1, n), F32)],
        compiler_params=_cparams(("parallel", "parallel")), name="s5_carry_scan")(xe, acs)


def s5_outputs(u_src, toep, hprev, ccar, d_skip, *, t_p):
    nt, kc, _ = toep.shape
    ns = hprev.shape[2]
    nc = t_p // S5_CS
    tn = 512
    per = tn // LANES

    def body(u_ref, t_ref, h_ref, c_ref, d_ref, o_ref):
        uc = _chunk_rows(u_ref, nc)
        hb = h_ref[0].astype(BF16)
        for n_ in range(kc // tn):
            cols = slice(n_ * tn, (n_ + 1) * tn)
            y = jnp.dot(uc, t_ref[0, :, cols], preferred_element_type=F32) + _dot_nt(hb, c_ref[0, cols, :])
            for i in range(per):
                rows = pl.ds(n_ * per + i, nc, stride=S5_CS)
                o_ref[rows, :] = _gelu_tanh(y[:, i * LANES:(i + 1) * LANES] + d_ref[...] * u_ref[rows, :])

    return pl.pallas_call(
        body, grid=(nt,),
        in_specs=[pl.BlockSpec((t_p, LANES), lambda j: (0, j)),
                  pl.BlockSpec((1, kc, kc), lambda j: (j, 0, 0)), pl.BlockSpec((1, nc, ns), lambda j: (j, 0, 0)),
                  pl.BlockSpec((1, kc, ns), lambda j: (j, 0, 0)), pl.BlockSpec((1, LANES), lambda j: (0, j))],
        out_specs=pl.BlockSpec((t_p, LANES), lambda j: (0, j)),
        out_shape=jax.ShapeDtypeStruct((t_p, nt * LANES), F32),
        compiler_params=_cparams(("parallel",)), name="s5_outputs")(
            u_src, toep, hprev, ccar, d_skip.reshape(1, nt * LANES))


def s5_decode(u_src, h_re, h_im, bbar, a1, c0, d_skip, *, row0):
    s = h_re.shape[0]
    nt = bbar.shape[0]
    w = nt * LANES
    ts = S5_TSTATE
    rb = row0 // s

    def body(u_ref, hr_ref, hi_ref, b_ref, a_ref, c_ref, d_ref, y_ref, nr_ref, ni_ref):
        uu = u_ref[...]
        x = _bdot(uu, b_ref[0])
        ar, ai = a_ref[0, :, :ts], a_ref[0, :, ts:]
        hr, hi = hr_ref[...], hi_ref[...]
        nr = ar * hr - ai * hi + x[:, :ts]
        ni = ar * hi + ai * hr + x[:, ts:]
        nr_ref[...] = nr
        ni_ref[...] = ni
        y = _dot_nt(jnp.concatenate([nr, ni], axis=1), c_ref[0]) + d_ref[...] * uu
        y_ref[...] = _gelu_tanh(y)

    col = pl.BlockSpec((s, LANES), lambda j: (0, j))
    st = pl.BlockSpec((s, ts), lambda j: (0, j))
    tab = pl.BlockSpec((1, LANES, 2 * ts), lambda j: (j, 0, 0))
    return pl.pallas_call(
        body, grid=(nt,),
        in_specs=[pl.BlockSpec((s, LANES), lambda j: (rb, j)), st, st, tab,
                  pl.BlockSpec((1, 1, 2 * ts), lambda j: (j, 0, 0)), tab,
                  pl.BlockSpec((1, LANES), lambda j: (0, j))],
        out_specs=[col, st, st],
        out_shape=[jax.ShapeDtypeStruct((s, w), F32), jax.ShapeDtypeStruct(h_re.shape, F32),
                   jax.ShapeDtypeStruct(h_im.shape, F32)],
        compiler_params=_cparams(("parallel",)), name="s5_decode")(u_src, h_re, h_im, bbar, a1, c0, d_skip.reshape(1, w))


def s5_branch(u_src, n_seq, seq_len, h_re, h_im, tabs, d_skip):
    t_p = n_seq * seq_len
    nt = S5_TILES
    xe = s5_chunk_states(u_src, tabs['bend'], t_p=t_p)
    hprev, hfin = s5_carry_scan(xe, tabs['acs'], n_seq)
    y_p = s5_outputs(u_src, tabs['toep'], hprev, tabs['ccar_t'], d_skip, t_p=t_p)
    hfin = hfin.reshape(nt, n_seq, 2, S5_TSTATE).transpose(2, 1, 0, 3).reshape(2, n_seq, S5_GROUPS, S5_STATE)
    s_rows = u_src.shape[0] - t_p
    y_s, nr, ni = s5_decode(u_src, h_re.reshape(s_rows, -1), h_im.reshape(s_rows, -1), tabs['bbar'],
                            tabs['a1'], tabs['c0_t'], d_skip, row0=t_p)
    return (jnp.concatenate([y_p, y_s], axis=0), hfin[0], hfin[1], nr.reshape(h_re.shape), ni.reshape(h_im.shape))


def moe_route(xb, w_router_t, b_router, *, tm):
    t, d = xb.shape
    ne, ng, gs = N_EXPERTS, N_GROUPS, N_EXPERTS // N_GROUPS
    neg = -jnp.inf

    def body(x_ref, w_ref, b_ref, u_ref, idx_ref, wt_ref, rank_ref, cnt_ref, carry_ref):
        @pl.when(pl.program_id(0) == 0)
        def _():
            carry_ref[...] = jnp.zeros_like(carry_ref)

        logits = lax.dot_general(w_ref[...], x_ref[...], (((1,), (1,)), ((), ())), preferred_element_type=F32)
        scores = _sigmoid(logits).reshape(ng, gs, tm)
        choice = scores + b_ref[...].reshape(ng, gs, 1)
        e_in = lax.broadcasted_iota(I32, (ng, gs, tm), 1).astype(F32)
        g_id = lax.broadcasted_iota(I32, (ng, 1, tm), 0).astype(F32)
        e_id = g_id * gs + e_in
        m1 = jnp.max(choice, axis=1, keepdims=True)
        i1 = jnp.min(jnp.where(choice == m1, e_in, float(gs)), axis=1, keepdims=True)
        m2 = jnp.max(jnp.where(e_in == i1, neg, choice), axis=1, keepdims=True)
        gscore = m1 + m2
        keep = jnp.zeros((ng, 1, tm), F32)
        for _ in range(TOPK_GROUPS):
            gm = jnp.max(gscore, axis=0, keepdims=True)
            gi = jnp.min(jnp.where(gscore == gm, g_id, float(ng)), axis=0, keepdims=True)
            hit = g_id == gi
            keep = jnp.where(hit, 1.0, keep)
            gscore = jnp.where(hit, neg, gscore)
        cand = jnp.where(keep > 0.0, choice, neg)
        member = jnp.zeros((ng, gs, tm), F32)
        picks, wts = [], []
        for _ in range(TOP_K):
            cm = jnp.max(jnp.max(cand, axis=1, keepdims=True), axis=0, keepdims=True)
            ei = jnp.min(jnp.min(jnp.where(cand == cm, e_id, float(ne)), axis=1, keepdims=True), axis=0, keepdims=True)
            sel = e_id == ei
            wts.append(jnp.sum(jnp.sum(jnp.where(sel, scores, 0.0), axis=1, keepdims=True), axis=0, keepdims=True))
            picks.append(ei)
            member = jnp.where(sel, 1.0, member)
            cand = jnp.where(sel, neg, cand)
        wsum = wts[0]
        for w in wts[1:]:
            wsum = wsum + w
        member2 = member.reshape(ne, tm)
        prefix = jnp.dot(member2.astype(BF16), u_ref[...], preferred_element_type=F32) + carry_ref[:, 0:1]
        prefix = prefix.reshape(ng, gs, tm)
        for j in range(TOP_K):
            sel = e_id == picks[j]
            rk = jnp.sum(jnp.sum(jnp.where(sel, prefix, 0.0), axis=1, keepdims=True), axis=0, keepdims=True)
            idx_ref[j:j + 1, :] = picks[j].reshape(1, tm).astype(I32)
            rank_ref[j:j + 1, :] = rk.reshape(1, tm).astype(I32)
            wt_ref[j:j + 1, :] = (wts[j] / wsum * ROUTED_SCALE).reshape(1, tm)
        carry_ref[...] = carry_ref[...] + jnp.sum(member2, axis=1, keepdims=True)
        cnt_ref[...] = carry_ref[...]

    upper = jnp.triu(jnp.ones((tm, tm), F32), 1).astype(BF16)
    tok = pl.BlockSpec((TOP_K, tm), lambda i: (0, i))
    idx, wt, rank, cnt = pl.pallas_call(
        body, grid=(t // tm,),
        in_specs=[pl.BlockSpec((tm, d), lambda i: (i, 0)), pl.BlockSpec((ne, d), lambda i: (0, 0)),
                  pl.BlockSpec((ne, 1), lambda i: (0, 0)), pl.BlockSpec((tm, tm), lambda i: (0, 0))],
        out_specs=[tok, tok, tok, pl.BlockSpec((ne, LANES), lambda i: (0, 0))],
        out_shape=[jax.ShapeDtypeStruct((TOP_K, t), I32), jax.ShapeDtypeStruct((TOP_K, t), F32),
                   jax.ShapeDtypeStruct((TOP_K, t), I32), jax.ShapeDtypeStruct((ne, LANES), F32)],
        scratch_shapes=[pltpu.VMEM((ne, LANES), F32)],
        compiler_params=_cparams(("arbitrary",)), name="moe_route")(xb, w_router_t, b_router.reshape(ne, 1), upper)
    return idx, wt, rank, cnt[:, 0]


def moe_experts(x, row_tok, block_e, n_used, w1, w3, w2, *, rows, layer):
    t, d = x.shape
    nb = row_tok.shape[0]
    f = w1.shape[3]

    def body(be_ref, nu_ref, x_hbm, tok_ref, tokn_ref, w1_ref, w3_ref, w2_ref, o_ref, buf, sem, w1b, w3b, w2b):
        i = pl.program_id(0)
        slot = lax.rem(i, 2)
        nxt = 1 - slot
        groups = 4
        per = rows // groups

        def row_copy(tref, r, sl):
            tok = tref[0, 0, r]
            return pltpu.make_async_copy(x_hbm.at[pl.ds(tok, 1), :], buf.at[sl, pl.ds(r, 1), :], sem.at[sl])

        def gather_loop(tref, sl):
            def issue(r, c):
                row_copy(tref, r, sl).start(priority=1)
                return c
            lax.fori_loop(0, rows, issue, 0, unroll=8)

        def gather_group(tref, sl, g):
            for r in range(g * per, (g + 1) * per):
                row_copy(tref, r, sl).start(priority=1)

        def block_wait(sl):
            pltpu.make_async_copy(x_hbm.at[pl.ds(0, rows), :], buf.at[sl], sem.at[sl]).wait()

        @pl.when(i == 0)
        def _():
            gather_loop(tok_ref, 0)

        e = be_ref[i]
        changed = jnp.logical_or(i == 0, e != be_ref[jnp.maximum(i - 1, 0)])

        @pl.when(changed)
        def _():
            w1b[...] = w1_ref[0, 0].astype(BF16)
            w3b[...] = w3_ref[0, 0].astype(BF16)
            w2b[...] = w2_ref[0, 0].astype(BF16)

        block_wait(slot)

        @pl.when(i < nu_ref[0])
        def _():
            xb = buf[slot].astype(BF16)
            gather_group(tokn_ref, nxt, 0)
            h1 = jnp.dot(xb, w1b[...], preferred_element_type=F32)
            gather_group(tokn_ref, nxt, 1)
            h3 = jnp.dot(xb, w3b[...], preferred_element_type=F32)
            gather_group(tokn_ref, nxt, 2)
            y = jnp.dot((_silu(h1) * h3).astype(BF16), w2b[...], preferred_element_type=F32)
            gather_group(tokn_ref, nxt, 3)
            o_ref[...] = y

        @pl.when(i >= nu_ref[0])
        def _():
            gather_loop(tokn_ref, nxt)
            o_ref[...] = jnp.zeros_like(o_ref)

        @pl.when(i == nb - 1)
        def _():
            block_wait(nxt)

    grid_spec = pltpu.PrefetchScalarGridSpec(
        num_scalar_prefetch=2, grid=(nb,),
        in_specs=[pl.BlockSpec(memory_space=pl.ANY),
                  pl.BlockSpec((1, 1, rows), lambda i, be, nu: (i, 0, 0), memory_space=pltpu.SMEM),
                  pl.BlockSpec((1, 1, rows), lambda i, be, nu: (jnp.minimum(i + 1, nb - 1), 0, 0),
                               memory_space=pltpu.SMEM),
                  pl.BlockSpec((1, 1, d, f), lambda i, be, nu: (layer, be[i], 0, 0)),
                  pl.BlockSpec((1, 1, d, f), lambda i, be, nu: (layer, be[i], 0, 0)),
                  pl.BlockSpec((1, 1, f, d), lambda i, be, nu: (layer, be[i], 0, 0))],
        out_specs=pl.BlockSpec((rows, d), lambda i, be, nu: (i, 0)),
        scratch_shapes=[pltpu.VMEM((2, rows, d), F32), pltpu.SemaphoreType.DMA((2,)),
                        pltpu.VMEM((d, f), BF16), pltpu.VMEM((d, f), BF16), pltpu.VMEM((f, d), BF16)])
    return pl.pallas_call(
        body, grid_spec=grid_spec, out_shape=jax.ShapeDtypeStruct((nb * rows, d), F32),
        compiler_params=_cparams(("arbitrary",)), name="moe_experts")(block_e, n_used, x, row_tok, row_tok, w1, w3, w2)


def moe_combine_ln(ys, dest, wts, x, shared, g, b, *, tm):
    t, d = x.shape
    nt = t // tm

    def body(ys_hbm, d_ref, dn_ref, w_ref, x_ref, s_ref, g_ref, b_ref, o_ref, ob_ref, buf, sem):
        i = pl.program_id(0)
        slot = lax.rem(i, 2)

        def gather(dref, sl):
            def issue(r, c):
                row = dref[0, 0, r]
                pltpu.make_async_copy(ys_hbm.at[pl.ds(row, 1), :], buf.at[sl, pl.ds(r, 1), :],
                                      sem.at[sl]).start(priority=1)
                return c
            lax.fori_loop(0, TOP_K * tm, issue, 0, unroll=8)

        @pl.when(i == 0)
        def _():
            gather(d_ref, 0)

        @pl.when(i + 1 < nt)
        def _():
            gather(dn_ref, 1 - slot)

        pltpu.make_async_copy(ys_hbm.at[pl.ds(0, TOP_K * tm), :], buf.at[slot], sem.at[slot]).wait()
        w = w_ref[...]
        acc = DEEPNORM_ALPHA * x_ref[...] + s_ref[...]
        for j in range(TOP_K):
            acc = acc + w[:, j:j + 1] * buf[slot, j * tm:(j + 1) * tm, :]
        y = _layer_norm_rows(acc, g_ref[...], b_ref[...])
        o_ref[...] = y
        ob_ref[...] = y.astype(BF16)

    row = pl.BlockSpec((1, d), lambda i: (0, 0))
    tile = pl.BlockSpec((tm, d), lambda i: (i, 0))
    return pl.pallas_call(
        body, grid=(nt,),
        in_specs=[pl.BlockSpec(memory_space=pl.ANY),
                  pl.BlockSpec((1, 1, TOP_K * tm), lambda i: (i, 0, 0), memory_space=pltpu.SMEM),
                  pl.BlockSpec((1, 1, TOP_K * tm), lambda i: (jnp.minimum(i + 1, nt - 1), 0, 0),
                               memory_space=pltpu.SMEM),
                  pl.BlockSpec((tm, TOP_K), lambda i: (i, 0)), tile, tile, row, row],
        out_specs=[tile, tile],
        out_shape=[jax.ShapeDtypeStruct((t, d), F32), jax.ShapeDtypeStruct((t, d), BF16)],
        scratch_shapes=[pltpu.VMEM((2, TOP_K * tm, d), F32), pltpu.SemaphoreType.DMA((2,))],
        compiler_params=_cparams(("arbitrary",)), name="moe_combine_ln")(
            ys, dest, dest, wts, x, shared, g.reshape(1, d), b.reshape(1, d))


def moe_layer(x, xb, lw, *, route_tm, rows=MOE_ROWS, tok_tile=MOE_TOK_TILE):
    t, d = x.shape
    idx, wt, rank, counts = moe_route(xb, lw['moe_w_router'].T.astype(BF16), lw['moe_b_router'], tm=route_tm)
    counts = counts.astype(I32)
    padded = (counts + rows - 1) // rows * rows
    pad_end = jnp.cumsum(padded)
    pad_start = pad_end - padded
    onehot = idx[:, :, None] == jnp.arange(N_EXPERTS, dtype=I32)[None, None, :]
    dest = jnp.sum(jnp.where(onehot, pad_start[None, None, :], 0), axis=-1) + rank
    n_rows = -(-(t * TOP_K + N_EXPERTS * (rows - 1)) // rows) * rows
    nb = n_rows // rows
    tok_id = jnp.broadcast_to(jnp.arange(t, dtype=I32)[None, :], (TOP_K, t))
    row_tok = jnp.zeros((n_rows,), I32).at[dest.reshape(-1)].set(tok_id.reshape(-1))
    blk_start = jnp.arange(nb, dtype=I32) * rows
    block_e = jnp.minimum(jnp.sum((pad_end[None, :] <= blk_start[:, None]).astype(I32), axis=1), N_EXPERTS - 1)
    n_used = (pad_end[-1] // rows).astype(I32).reshape(1)
    ys = moe_experts(x, row_tok.reshape(nb, 1, rows), block_e, n_used, lw['moe_w1'], lw['moe_w3'], lw['moe_w2'],
                     rows=rows, layer=lw['layer'])
    hs = swiglu_hidden(xb, lw['moe_ws1'], lw['moe_ws3'])
    shared = dense(hs, lw['moe_ws2'])
    dest_t = dest.reshape(TOP_K, t // tok_tile, tok_tile).transpose(1, 0, 2).reshape(t // tok_tile, 1, TOP_K * tok_tile)
    return moe_combine_ln(ys, dest_t, wt.T, x, shared, lw['ln2_g'], lw['ln2_b'], tm=tok_tile)


def _level_tables(c):
    import numpy as np
    idx = np.arange(c)
    t, r = idx[:, None], idx[None, :]
    wl, pm = [], []
    b = 1
    while b < c:
        blk, odd = t // b, (t // b) % 2 == 1
        w = np.where(odd, (r >= blk * b) & (r <= t), (r > t) & (r <= blk * b + b - 1))
        wl.append(w.astype(np.float32))
        pm.append((odd & (r // b == blk - 1)).astype(np.float32))
        b *= 2
    incl = (r <= t).astype(np.float32)
    after = (r > t).astype(np.float32)
    return wl, pm, incl, after


def _split3(x):
    hi = x.astype(BF16)
    r1 = x - hi.astype(F32)
    mid = r1.astype(BF16)
    lo = (r1 - mid.astype(F32)).astype(BF16)
    return hi, mid, lo


def _table_dot(tab, x):
    hi, mid, lo = _split3(x)
    return (jnp.dot(tab, hi, preferred_element_type=F32) + jnp.dot(tab, mid, preferred_element_type=F32)
            + jnp.dot(tab, lo, preferred_element_type=F32))


def _dot_hi(a, b):
    ah = a.astype(BF16)
    al = (a - ah.astype(F32)).astype(BF16)
    bh = b.astype(BF16)
    bl = (b - bh.astype(F32)).astype(BF16)
    return (jnp.dot(ah, bh, preferred_element_type=F32) + jnp.dot(ah, bl, preferred_element_type=F32)
            + jnp.dot(al, bh, preferred_element_type=F32))


def _dot_nt(a, b):
    return lax.dot_general(a.astype(BF16), b.astype(BF16), (((1,), (1,)), ((), ())), preferred_element_type=F32)


def _dot_tn(a, b):
    return lax.dot_general(a.astype(BF16), b.astype(BF16), (((0,), (0,)), ((), ())), preferred_element_type=F32)


def _rms_rows(o, w):
    return o * lax.rsqrt(jnp.mean(o * o, axis=-1, keepdims=True) + NORM_EPS) * w


def gla_prompt(z1, log_a, norm_w, *, n_seq, seq_len):
    c, h_, dk, dv = CHUNK, GLA_HEADS, GLA_DK, GLA_DV
    nck = seq_len // c
    wl, pm, incl, after = _level_tables(c)
    nl = len(wl)
    wcat = jnp.asarray(jnp.concatenate([jnp.asarray(w) for w in wl] + [jnp.asarray(incl), jnp.asarray(after)], axis=0),
                       BF16)
    pmask = jnp.stack([jnp.eye(c, dtype=F32)] + [jnp.asarray(p) for p in pm])
    scale = dk ** -0.5

    def body(q_ref, k_ref, v_ref, r_ref, g_ref, w_ref, p_ref, n_ref, o_ref, st_ref, s_scr):
        ci = pl.program_id(1)

        @pl.when(ci == 0)
        def _():
            s_scr[...] = jnp.zeros_like(s_scr)

        x = _table_dot(w_ref[...], g_ref[...])
        ex = jnp.exp(x)
        for h in range(h_):
            ks = slice(h * dk, (h + 1) * dk)
            vs = slice(h * dv, (h + 1) * dv)
            q = q_ref[:, ks] * scale
            k = k_ref[:, ks]
            v = v_ref[:, vs]
            scores = p_ref[0] * _dot_nt(q, k)
            for l in range(nl):
                f = ex[l * c:(l + 1) * c, ks]
                scores = scores + p_ref[l + 1] * _dot_nt(q * f, k * f)
            st = s_scr[h]
            o = _dot_nt(q * ex[nl * c:(nl + 1) * c, ks], st) + _bdot(scores, v)
            tot = x[(nl + 1) * c - 1:(nl + 1) * c, ks]
            s_scr[h] = jnp.exp(tot) * st + _dot_tn(v, k * ex[(nl + 1) * c:(nl + 2) * c, ks])
            o_ref[:, vs] = (_rms_rows(o, n_ref[...]) * _silu(r_ref[:, vs])).astype(o_ref.dtype)

        @pl.when(ci == nck - 1)
        def _():
            st_ref[0] = s_scr[...]

    def rows(width, col):
        return pl.BlockSpec((c, width), lambda b, i: (b * nck + i, col))

    return pl.pallas_call(
        body, grid=(n_seq, nck),
        in_specs=[rows(h_ * dk, 0), rows(h_ * dk, 1), rows(h_ * dv, 1), rows(h_ * dv, 2), rows(h_ * dk, 0),
                  pl.BlockSpec(wcat.shape, lambda b, i: (0, 0)), pl.BlockSpec(pmask.shape, lambda b, i: (0, 0, 0)),
                  pl.BlockSpec((1, dv), lambda b, i: (0, 0))],
        out_specs=[rows(h_ * dv, 0), pl.BlockSpec((1, h_, dv, dk), lambda b, i: (b, 0, 0, 0))],
        out_shape=[jax.ShapeDtypeStruct((n_seq * seq_len, h_ * dv), BF16),
                   jax.ShapeDtypeStruct((n_seq, h_, dv, dk), F32)],
        scratch_shapes=[pltpu.VMEM((h_, dv, dk), F32)],
        compiler_params=_cparams(("parallel", "arbitrary")), name="gla_prompt")(
            z1, z1, z1, z1, log_a, wcat, pmask, norm_w.reshape(1, dv))


def _columns(rows_list, width):
    used = sum(r.shape[0] for r in rows_list)
    stack = jnp.concatenate(list(rows_list) + [jnp.zeros((LANES - used, width), F32)], axis=0)
    return stack.T


def gla_decode(z1, log_a, state, norm_w, *, row0, layer):
    sb = 8
    n_s = state.shape[1]
    h_, dk, dv = GLA_HEADS, GLA_DK, GLA_DV
    scale = dk ** -0.5
    r0 = row0 // sb

    def body(q_ref, k_ref, v_ref, r_ref, g_ref, s_ref, n_ref, o_ref, ns_ref):
        for h in range(h_):
            ks = slice(h * dk, (h + 1) * dk)
            vs = slice(h * dv, (h + 1) * dv)
            cols = _columns([jnp.exp(g_ref[:, ks]), k_ref[:, ks], q_ref[:, ks] * scale], dk)
            v = v_ref[:, vs]
            outs = []
            for s in range(sb):
                s_new = cols[:, s:s + 1] * s_ref[0, s, h] + cols[:, sb + s:sb + s + 1] * v[s:s + 1, :]
                ns_ref[s, h] = s_new
                outs.append(jnp.sum(cols[:, 2 * sb + s:2 * sb + s + 1] * s_new, axis=0, keepdims=True))
            o = jnp.concatenate(outs, axis=0)
            o_ref[:, vs] = _rms_rows(o, n_ref[...]) * _silu(r_ref[:, vs])

    def rows(width, col):
        return pl.BlockSpec((sb, width), lambda i: (r0 + i, col))

    st_in = pl.BlockSpec((1, sb, h_, dk, dv), lambda i: (layer, i, 0, 0, 0))
    st = pl.BlockSpec((sb, h_, dk, dv), lambda i: (i, 0, 0, 0))
    return pl.pallas_call(
        body, grid=(n_s // sb,),
        in_specs=[rows(h_ * dk, 0), rows(h_ * dk, 1), rows(h_ * dv, 1), rows(h_ * dv, 2), rows(h_ * dk, 0), st_in,
                  pl.BlockSpec((1, dv), lambda i: (0, 0))],
        out_specs=[pl.BlockSpec((sb, h_ * dv), lambda i: (i, 0)), st],
        out_shape=[jax.ShapeDtypeStruct((n_s, h_ * dv), F32), jax.ShapeDtypeStruct(state.shape[1:], F32)],
        compiler_params=_cparams(("parallel",)), name="gla_decode")(
            z1, z1, z1, z1, log_a, state, norm_w.reshape(1, dv))


def _conv_silu(ext, w, c):
    acc = ext[5:5 + c] * w[0:1]
    for i in range(1, GDN_CONV):
        acc = acc + ext[5 + i:5 + i + c] * w[i:i + 1]
    return _silu(acc)


def _softplus(x):
    return jnp.maximum(x, 0.0) + jnp.log1p(jnp.exp(-jnp.abs(x)))


def _l2n(x):
    return x * lax.rsqrt(jnp.sum(x * x, axis=-1, keepdims=True) + NORM_EPS)


def gdn_prompt(z3, z4, conv_w, a_log, dt_bias, norm_w, *, n_seq, seq_len):
    c, h_, dk, dv = CHUNK, GDN_HEADS, GDN_DK, GDN_DV
    kw = h_ * dk
    nck = seq_len // c
    _, pm, incl, after = _level_tables(c)
    nl = len(pm)
    import numpy as np
    strict = (np.arange(c)[:, None] > np.arange(c)[None, :]).astype(np.float32)
    tabs = jnp.asarray(np.concatenate([incl, after, np.ones((c, c), np.float32)], axis=0), BF16)
    masks = jnp.stack([jnp.asarray(incl), jnp.asarray(strict), jnp.eye(c, dtype=F32)] + [jnp.asarray(p) for p in pm])
    strict_pad = jnp.asarray(np.concatenate([strict, np.zeros((c, LANES - c), np.float32)], axis=1))
    qscale = dk ** -0.5

    def body(q_ref, k_ref, v_ref, zg_ref, ab_ref, cw_ref, al_ref, db_ref, t_ref, m_ref, sp_ref, n_ref,
             o_ref, st_ref, s_scr, hist):
        ci = pl.program_id(1)

        @pl.when(ci == 0)
        def _():
            s_scr[...] = jnp.zeros_like(s_scr)
            hist[...] = jnp.zeros_like(hist)

        def conv(ref, j):
            cols = slice(j * kw, (j + 1) * kw)
            raw = ref[...]
            ext = jnp.concatenate([hist[:, cols], raw], axis=0)
            y = _conv_silu(ext, cw_ref[:, cols], c)
            hist[:, cols] = raw[c - 8:c]
            return y

        qc, kc, vc = conv(q_ref, 0), conv(k_ref, 1), conv(v_ref, 2)
        ab = ab_ref[...]
        g = -jnp.exp(al_ref[...]) * _softplus(ab[:, :h_] + db_ref[...])
        beta = _sigmoid(ab[:, h_:])
        sums = _table_dot(t_ref[...], jnp.concatenate([g, jnp.zeros((c, LANES - h_), F32)], axis=1))
        e_cum = jnp.exp(sums[0:c])
        e_rest = jnp.exp(sums[c:2 * c])
        e_last = jnp.exp(sums[2 * c:2 * c + 1])
        grel = jnp.concatenate([g[:, h:h + 1] * sp_ref[...] for h in range(h_)], axis=1)
        rel = _table_dot(t_ref[0:c], grel)
        m_incl, m_strict, m_eye = m_ref[0], m_ref[1], m_ref[2]
        qs, ks, vs, kbs, decs, amat, tinv = [], [], [], [], [], [], []
        for h in range(h_):
            hs = slice(h * dk, (h + 1) * dk)
            qs.append(_l2n(qc[:, hs]) * qscale)
            ks.append(_l2n(kc[:, hs]))
            vs.append(vc[:, hs])
            decs.append(m_incl * jnp.exp(m_incl * rel[:, h * LANES:h * LANES + c]))
            kbs.append(ks[h] * beta[:, h:h + 1])
            amat.append(m_strict * _dot_nt(kbs[h], ks[h]) * decs[h])
            tinv.append(m_eye - m_ref[3] * amat[h])
        for l in range(1, nl):
            tinv = [tinv[h] - _dot_hi(_dot_hi(tinv[h], m_ref[3 + l] * amat[h]), tinv[h]) for h in range(h_)]
        for h in range(h_):
            hs = slice(h * dk, (h + 1) * dk)
            q, k, v, kb, dec, t = qs[h], ks[h], vs[h], kbs[h], decs[h], tinv[h]
            bcol = beta[:, h:h + 1]
            tw = _bdot(t, jnp.concatenate([kb * e_cum[:, h:h + 1], v * bcol], axis=1))
            s_old = s_scr[h]
            both = _bdot(jnp.concatenate([q * e_cum[:, h:h + 1], tw[:, :dk]], axis=0), s_old)
            u = tw[:, dk:] - both[c:]
            o = both[:c] + _bdot(_dot_nt(q, k) * dec, u)
            s_scr[h] = e_last[:, h:h + 1] * s_old + _dot_tn(k * e_rest[:, h:h + 1], u)
            o_ref[:, hs] = (_rms_rows(o, n_ref[...]) * _silu(zg_ref[:, hs])).astype(o_ref.dtype)

        @pl.when(ci == nck - 1)
        def _():
            st_ref[0] = s_scr[...]

    def rows(width, col):
        return pl.BlockSpec((c, width), lambda b, i: (b * nck + i, col))

    def const(arr):
        nd = arr.ndim
        return pl.BlockSpec(arr.shape, lambda b, i: (0,) * nd)

    cw = conv_w
    al = a_log.reshape(1, h_)
    db = dt_bias.reshape(1, h_)
    nw = norm_w.reshape(1, dv)
    return pl.pallas_call(
        body, grid=(n_seq, nck),
        in_specs=[rows(kw, 1), rows(kw, 2), rows(kw, 3), rows(kw, 4), pl.BlockSpec((c, 2 * h_), lambda b, i: (b * nck + i, 0)),
                  const(cw), const(al), const(db), const(tabs), const(masks), const(strict_pad), const(nw)],
        out_specs=[rows(kw, 0), pl.BlockSpec((1, h_, dk, dv), lambda b, i: (b, 0, 0, 0))],
        out_shape=[jax.ShapeDtypeStruct((n_seq * seq_len, kw), BF16), jax.ShapeDtypeStruct((n_seq, h_, dk, dv), F32)],
        scratch_shapes=[pltpu.VMEM((h_, dk, dv), F32), pltpu.VMEM((8, 3 * kw), F32)],
        compiler_params=_cparams(("parallel", "arbitrary")), name="gdn_prompt")(
            z3, z3, z3, z3, z4, cw, al, db, tabs, masks, strict_pad, nw)


def gdn_decode(z3, z4, state, conv_buf, conv_w, a_log, dt_bias, norm_w, *, row0, layer):
    sb = 8
    n_s = state.shape[1]
    h_, dk, dv = GDN_HEADS, GDN_DK, GDN_DV
    kw = h_ * dk
    r0 = row0 // sb
    qscale = dk ** -0.5

    def body(q_ref, k_ref, v_ref, zg_ref, ab_ref, hb_ref, cw_ref, al_ref, db_ref, s_ref, n_ref, o_ref, ns_ref):
        def conv(ref, j):
            cols = slice(j * kw, (j + 1) * kw)
            acc = ref[...] * cw_ref[GDN_CONV - 1:GDN_CONV, cols]
            for i in range(GDN_CONV - 1):
                acc = acc + hb_ref[:, i, cols] * cw_ref[i:i + 1, cols]
            return _silu(acc)

        qc, kc, vc = conv(q_ref, 0), conv(k_ref, 1), conv(v_ref, 2)
        ab = ab_ref[...]
        eg = jnp.exp(-jnp.exp(al_ref[...]) * _softplus(ab[:, :h_] + db_ref[...]))
        beta = _sigmoid(ab[:, h_:])
        for h in range(h_):
            hs = slice(h * dk, (h + 1) * dk)
            q = _l2n(qc[:, hs]) * qscale
            k = _l2n(kc[:, hs])
            v = vc[:, hs]
            cols = _columns([k, q], dk)
            qk = jnp.sum(q * k, axis=-1, keepdims=True)
            outs = []
            for s in range(sb):
                s_old = s_ref[0, s, h]
                kcol = cols[:, s:s + 1]
                k_s = jnp.sum(kcol * s_old, axis=0, keepdims=True)
                q_s = jnp.sum(cols[:, sb + s:sb + s + 1] * s_old, axis=0, keepdims=True)
                e = eg[s:s + 1, h:h + 1]
                u = beta[s:s + 1, h:h + 1] * (v[s:s + 1, :] - e * k_s)
                ns_ref[s, h] = e * s_old + kcol * u
                outs.append(e * q_s + qk[s:s + 1, :] * u)
            o = jnp.concatenate(outs, axis=0)
            o_ref[:, hs] = _rms_rows(o, n_ref[...]) * _silu(zg_ref[:, hs])

    def rows(width, col):
        return pl.BlockSpec((sb, width), lambda i: (r0 + i, col))

    def const(arr):
        nd = arr.ndim
        return pl.BlockSpec(arr.shape, lambda i: (0,) * nd)

    st_in = pl.BlockSpec((1, sb, h_, dk, dv), lambda i: (layer, i, 0, 0, 0))
    st = pl.BlockSpec((sb, h_, dk, dv), lambda i: (i, 0, 0, 0))
    al = a_log.reshape(1, h_)
    db = dt_bias.reshape(1, h_)
    nw = norm_w.reshape(1, dv)
    return pl.pallas_call(
        body, grid=(n_s // sb,),
        in_specs=[rows(kw, 1), rows(kw, 2), rows(kw, 3), rows(kw, 4), pl.BlockSpec((sb, 2 * h_), lambda i: (r0 + i, 0)),
                  pl.BlockSpec((sb, GDN_CONV - 1, 3 * kw), lambda i: (i, 0, 0)), const(conv_w), const(al), const(db),
                  st_in, const(nw)],
        out_specs=[pl.BlockSpec((sb, kw), lambda i: (i, 0)), st],
        out_shape=[jax.ShapeDtypeStruct((n_s, kw), F32), jax.ShapeDtypeStruct(state.shape[1:], F32)],
        compiler_params=_cparams(("parallel",)), name="gdn_decode")(
            z3, z3, z3, z3, z4, conv_buf, conv_w, al, db, state, nw)


def trunk_layer(x, xb, pe_b, states, lw, *, n_seq, seq_len, route_tm):
    t_p = n_seq * seq_len
    n_s = x.shape[0] - t_p
    gla_s, s5_re, s5_im, gdn_s, conv_s = states
    w_in = lw['w_in']
    z1 = dense(xb, w_in[:, 0:3072].astype(BF16))
    a_lr = dense(xb, w_in[:, 3072:3088].astype(BF16))
    z3 = dense(xb, w_in[:, 3088:8208].astype(BF16))
    z4 = dense(xb, w_in[:, 8208:8224].astype(BF16))
    log_a = dense(a_lr, lw['gla_w_gate'], bias=lw['gla_b_gate'], act='log_decay')
    br_a_p, gla_pt = gla_prompt(z1, log_a, lw['gla_norm'], n_seq=n_seq, seq_len=seq_len)
    gla_p = jnp.swapaxes(gla_pt, 2, 3)
    br_a_s, gla_n = gla_decode(z1, log_a, gla_s, lw['gla_norm'], row0=t_p, layer=lw['layer'])
    br_a = jnp.concatenate([br_a_p, br_a_s.astype(BF16)], axis=0)
    tabs = s5_tables(lw['s5_lam_re'], lw['s5_lam_im'], lw['s5_log_dt'], lw['s5_b_re'], lw['s5_b_im'],
                     lw['s5_c_re'], lw['s5_c_im'])
    y_s, s5r_p, s5i_p, s5r_n, s5i_n = s5_branch(z3, n_seq, seq_len, s5_re, s5_im, tabs, lw['s5_d'])
    br_s = glu_gate(y_s, lw['s5_w_glu'].astype(BF16), lw['s5_b_glu'])
    br_c_p, gdn_p = gdn_prompt(z3, z4, lw['gdn_conv_w'], lw['gdn_a_log'], lw['gdn_dt_bias'], lw['gdn_norm'],
                               n_seq=n_seq, seq_len=seq_len)
    br_c_s, gdn_n = gdn_decode(z3, z4, gdn_s, conv_s, lw['gdn_conv_w'], lw['gdn_a_log'], lw['gdn_dt_bias'],
                               lw['gdn_norm'], row0=t_p, layer=lw['layer'])
    br_c = jnp.concatenate([br_c_p, br_c_s.astype(BF16)], axis=0)
    qkv_cols = slice(S5_WIDTH, S5_WIDTH + 2 * GDN_KW + GDN_VW)
    conv_p = jnp.stack([z3[(b + 1) * seq_len - (GDN_CONV - 1):(b + 1) * seq_len, qkv_cols] for b in range(n_seq)])
    conv_n = jnp.concatenate([conv_s[:, 1:], z3[t_p:, None, qkv_cols]], axis=1)
    merged = merge_branches(xb, w_in[:, 8224:].astype(BF16), br_a, br_s, br_c, lw['w_branch_a'].astype(BF16),
                            lw['w_branch_s'].astype(BF16), lw['w_branch_c'].astype(BF16))
    x1, x1b = out_proj_ln(merged, lw['w_out'].astype(BF16), x, lw['ln1_g'], lw['ln1_b'])
    x2, x2b = moe_layer(x1, x1b, lw, route_tm=route_tm)
    x3, x3b = ple_mix(x2, x2b, lw['ple_w_gate'].astype(BF16), pe_b, lw['ple_w_proj'].astype(BF16))
    return x3, x3b, (gla_p, s5r_p, s5i_p, gdn_p, conv_p), (gla_n, s5r_n, s5i_n, gdn_n, conv_n)


_NAMES = ('w_in', 'gla_w_gate', 'gla_b_gate', 'gla_norm', 's5_lam_re', 's5_lam_im', 's5_log_dt', 's5_b_re',
          's5_b_im', 's5_c_re', 's5_c_im', 's5_d', 's5_w_glu', 's5_b_glu', 'gdn_conv_w', 'gdn_a_log',
          'gdn_dt_bias', 'gdn_norm', 'w_branch_a', 'w_branch_s', 'w_branch_c', 'w_out', 'ln1_g', 'ln1_b',
          'ln2_g', 'ln2_b', 'moe_w_router', 'moe_b_router', 'moe_w1', 'moe_w3', 'moe_w2', 'moe_ws1', 'moe_ws3',
          'moe_ws2', 'ple_w_proj', 'ple_w_gate')


_STACKED = ('moe_w1', 'moe_w3', 'moe_w2')


def run_trunk(x_prompt, x_sample, p_prompt, p_sample, states, weights, *, route_tm):
    n_seq, seq_len, d = x_prompt.shape
    n_s = x_sample.shape[0]
    t_p = n_seq * seq_len
    depth = weights[0].shape[0]
    x = jnp.concatenate([x_prompt.reshape(t_p, d), x_sample.reshape(n_s, d)], axis=0)
    xb = x.astype(BF16)
    pe = jnp.concatenate([p_prompt.reshape(depth, t_p, -1), p_sample.reshape(depth, n_s, -1)], axis=1).astype(BF16)
    new_p, new_s = [], []
    for i in range(depth):
        lw = {n: (w if n in _STACKED else w[i]) for n, w in zip(_NAMES, weights)}
        lw['layer'] = i
        st = (states[0], states[1][i], states[2][i], states[3], states[4][i])
        x, xb, st_p, st_s = trunk_layer(x, xb, pe[i], st, lw, n_seq=n_seq, seq_len=seq_len, route_tm=route_tm)
        new_p.append(st_p)
        new_s.append(st_s)
    gla_p, s5r_p, s5i_p, gdn_p, conv_p = (jnp.stack(f) for f in zip(*new_p))
    gla_s, s5r_s, s5i_s, gdn_s, conv_s = (jnp.stack(f) for f in zip(*new_s))
    yp = x[:t_p].reshape(n_seq, seq_len, d)
    ys = x[t_p:].reshape(n_s, 1, d)
    return (yp, ys, gla_p, gla_s, s5r_p, s5r_s, s5i_p, s5i_s, gdn_p, gdn_s, conv_p, conv_s)


def kernel(x_prompt, x_sample, p_prompt, p_sample, state_gla, state_s5_re, state_s5_im, state_gdn, state_gdn_conv,
           w_in, gla_w_gate, gla_b_gate, gla_norm, s5_lam_re, s5_lam_im, s5_log_dt, s5_b_re, s5_b_im, s5_c_re,
           s5_c_im, s5_d, s5_w_glu, s5_b_glu, gdn_conv_w, gdn_a_log, gdn_dt_bias, gdn_norm, w_branch_a,
           w_branch_s, w_branch_c, w_out, ln1_g, ln1_b, ln2_g, ln2_b, moe_w_router, moe_b_router, moe_w1, moe_w3,
           moe_w2, moe_ws1, moe_ws3, moe_ws2, ple_w_proj, ple_w_gate):
    weights = (w_in, gla_w_gate, gla_b_gate, gla_norm, s5_lam_re, s5_lam_im, s5_log_dt, s5_b_re, s5_b_im, s5_c_re,
               s5_c_im, s5_d, s5_w_glu, s5_b_glu, gdn_conv_w, gdn_a_log, gdn_dt_bias, gdn_norm, w_branch_a,
               w_branch_s, w_branch_c, w_out, ln1_g, ln1_b, ln2_g, ln2_b, moe_w_router, moe_b_router, moe_w1,
               moe_w3, moe_w2, moe_ws1, moe_ws3, moe_ws2, ple_w_proj, ple_w_gate)
    states = (state_gla, state_s5_re, state_s5_im, state_gdn, state_gdn_conv)
    return run_trunk(x_prompt, x_sample, p_prompt, p_sample, states, weights, route_tm=640)
```

```python
import functools
import math

import jax
import jax.numpy as jnp
from jax import lax
from jax.experimental import pallas as pl
from jax.experimental.pallas import tpu as pltpu

F32 = jnp.float32
BF16 = jnp.bfloat16
I32 = jnp.int32

D_MODEL = 2048
DEPTH = 4
GLA_HEADS, GLA_DK, GLA_DV = 4, 128, 256
GLA_KW, GLA_VW, GLA_RANK, GLA_TAU = 512, 1024, 16, 16.0
S5_WIDTH, S5_CH, S5_GROUPS, S5_STATE = 1024, 16, 64, 64
GDN_HEADS, GDN_DK, GDN_DV = 8, 128, 128
GDN_KW, GDN_VW, GDN_CONV = 1024, 1024, 4
N_BRANCH = 3
IN_SIZES = (GLA_KW, GLA_KW, GLA_VW, GLA_VW, GLA_RANK, S5_WIDTH, GDN_KW, GDN_KW, GDN_VW, GDN_VW, GDN_HEADS,
            GDN_HEADS, N_BRANCH * D_MODEL)
CHUNK = 64
N_EXPERTS, TOP_K, N_GROUPS, TOPK_GROUPS = 64, 8, 8, 4
D_EXPERT = 512
ROUTED_SCALE = 2.5
LN_EPS = 1e-5
NORM_EPS = 1e-6
DEEPNORM_ALPHA = (2 * DEPTH) ** 0.25

LANES = 128
VMEM_LIMIT_BYTES = 56 * 1024 * 1024
S5_CS = 16
S5_TILES = S5_WIDTH // LANES
S5_TSTATE = (LANES // S5_CH) * S5_STATE
MOE_ROWS = 256
MOE_TOK_TILE = 128


def _cparams(sem):
    return pltpu.CompilerParams(dimension_semantics=sem, vmem_limit_bytes=VMEM_LIMIT_BYTES)


def _pick_tile(n, candidates):
    for c in candidates:
        if n % c == 0:
            return c
    return n


def _sigmoid(x):
    return 1.0 / (1.0 + jnp.exp(-x))


def _silu(x):
    return x * _sigmoid(x)


def _gelu_tanh(x):
    return 0.5 * x * (1.0 + jnp.tanh(math.sqrt(2.0 / math.pi) * (x + 0.044715 * (x * x * x))))


def _log_sigmoid(x):
    return jnp.minimum(x, 0.0) - jnp.log1p(jnp.exp(-jnp.abs(x)))


def _bdot(a, b):
    return jnp.dot(a.astype(BF16), b.astype(BF16), preferred_element_type=F32)


def dense(x, w, *, bias=None, act=None, out_dtype=F32, tm=None, tn=None):
    m, k = x.shape
    n = w.shape[1]
    tm = tm or _pick_tile(m, (1664, 640, 512, 256, 128, 64, 32, 16, 8))
    tn = tn or _pick_tile(n, (512, 256, 128))

    def body(x_ref, w_ref, *rest):
        o_ref = rest[-1]
        y = _bdot(x_ref[...], w_ref[...])
        if bias is not None:
            y = y + rest[0][...]
        if act == 'log_decay':
            y = _log_sigmoid(y) / GLA_TAU
        o_ref[...] = y.astype(o_ref.dtype)

    in_specs = [pl.BlockSpec((tm, k), lambda i, j: (i, 0)), pl.BlockSpec((k, tn), lambda i, j: (0, j))]
    args = [x, w]
    if bias is not None:
        in_specs.append(pl.BlockSpec((1, tn), lambda i, j: (0, j)))
        args.append(bias.reshape(1, n))
    return pl.pallas_call(
        body, grid=(m // tm, n // tn), in_specs=in_specs,
        out_specs=pl.BlockSpec((tm, tn), lambda i, j: (i, j)),
        out_shape=jax.ShapeDtypeStruct((m, n), out_dtype),
        compiler_params=_cparams(("parallel", "parallel")), name="dense")(*args)


def swiglu_hidden(x, w1, w3):
    m, k = x.shape
    n = w1.shape[1]
    tm = _pick_tile(m, (1664, 640, 512, 256, 128, 64, 32, 16, 8))
    tn = _pick_tile(n, (512, 256, 128))

    def body(x_ref, w1_ref, w3_ref, o_ref):
        xb = x_ref[...].astype(BF16)
        a = jnp.dot(xb, w1_ref[...].astype(BF16), preferred_element_type=F32)
        b = jnp.dot(xb, w3_ref[...].astype(BF16), preferred_element_type=F32)
        o_ref[...] = (_silu(a) * b).astype(o_ref.dtype)

    return pl.pallas_call(
        body, grid=(m // tm, n // tn),
        in_specs=[pl.BlockSpec((tm, k), lambda i, j: (i, 0)), pl.BlockSpec((k, tn), lambda i, j: (0, j)),
                  pl.BlockSpec((k, tn), lambda i, j: (0, j))],
        out_specs=pl.BlockSpec((tm, tn), lambda i, j: (i, j)),
        out_shape=jax.ShapeDtypeStruct((m, n), BF16),
        compiler_params=_cparams(("parallel", "parallel")), name="swiglu_hidden")(x, w1, w3)


def glu_gate(y, w, b):
    m, n = y.shape
    tm = _pick_tile(m, (1664, 640, 512, 256, 128, 64, 32, 16, 8))
    tn = _pick_tile(n, (512, 256, 128))

    def body(y_ref, yt_ref, w_ref, b_ref, o_ref):
        g = _bdot(y_ref[...], w_ref[...]) + b_ref[...]
        o_ref[...] = (yt_ref[...] * _sigmoid(g)).astype(o_ref.dtype)

    return pl.pallas_call(
        body, grid=(m // tm, n // tn),
        in_specs=[pl.BlockSpec((tm, n), lambda i, j: (i, 0)), pl.BlockSpec((tm, tn), lambda i, j: (i, j)),
                  pl.BlockSpec((n, tn), lambda i, j: (0, j)), pl.BlockSpec((1, tn), lambda i, j: (0, j))],
        out_specs=pl.BlockSpec((tm, tn), lambda i, j: (i, j)),
        out_shape=jax.ShapeDtypeStruct((m, n), BF16),
        compiler_params=_cparams(("parallel", "parallel")), name="glu_gate")(y, y, w, b.reshape(1, n))


def merge_branches(xb, w_gates, br_a, br_s, br_c, w_a, w_s, w_c):
    m, k = xb.shape
    d = w_a.shape[1]
    kb = br_a.shape[1]
    tm = _pick_tile(m, (1664, 832, 640, 512, 256, 128, 64, 32, 16, 8))
    tn = _pick_tile(d, (256, 128))
    nj = d // tn

    def body(x_ref, g0_ref, g1_ref, g2_ref, a_ref, s_ref, c_ref, wa_ref, ws_ref, wc_ref, o_ref):
        x = x_ref[...]
        acc = _sigmoid(_bdot(x, g0_ref[...])) * _bdot(a_ref[...], wa_ref[...])
        acc = acc + _sigmoid(_bdot(x, g1_ref[...])) * _bdot(s_ref[...], ws_ref[...])
        acc = acc + _sigmoid(_bdot(x, g2_ref[...])) * _bdot(c_ref[...], wc_ref[...])
        o_ref[...] = acc.astype(o_ref.dtype)

    def gate_spec(b):
        return pl.BlockSpec((k, tn), lambda i, j: (0, b * nj + j))

    act_spec = pl.BlockSpec((tm, kb), lambda i, j: (i, 0))
    w_spec = pl.BlockSpec((kb, tn), lambda i, j: (0, j))
    return pl.pallas_call(
        body, grid=(m // tm, nj),
        in_specs=[pl.BlockSpec((tm, k), lambda i, j: (i, 0)), gate_spec(0), gate_spec(1), gate_spec(2),
                  act_spec, act_spec, act_spec, w_spec, w_spec, w_spec],
        out_specs=pl.BlockSpec((tm, tn), lambda i, j: (i, j)),
        out_shape=jax.ShapeDtypeStruct((m, d), BF16),
        compiler_params=_cparams(("parallel", "parallel")), name="merge_branches")(
            xb, w_gates, w_gates, w_gates, br_a, br_s, br_c, w_a, w_s, w_c)


def _layer_norm_rows(y, g, b):
    mu = jnp.mean(y, axis=-1, keepdims=True)
    yc = y - mu
    var = jnp.mean(yc * yc, axis=-1, keepdims=True)
    return yc * lax.rsqrt(var + LN_EPS) * g + b


_HI16 = 0xFFFF0000


def _pack_bf16_halves(y):
    c = y.shape[1] // 2
    bits = lax.bitcast_convert_type(y.astype(BF16).astype(F32), jnp.uint32)
    return (bits[:, c:] & jnp.uint32(_HI16)) | (bits[:, :c] >> 16)


def _unpack_bf16_halves(p):
    return (lax.bitcast_convert_type(p << 16, F32), lax.bitcast_convert_type(p & jnp.uint32(_HI16), F32))


def out_proj_ln(merged, w_out, x, g, b):
    m, k = merged.shape
    d = w_out.shape[1]
    tm = _pick_tile(m, (416, 256, 128, 64, 32, 16))

    def body(m_ref, w_ref, x_ref, g_ref, b_ref, o_ref, ob_ref, op_ref):
        y = DEEPNORM_ALPHA * x_ref[...] + _bdot(m_ref[...], w_ref[...])
        y = _layer_norm_rows(y, g_ref[...], b_ref[...])
        o_ref[...] = y
        ob_ref[...] = y.astype(BF16)
        op_ref[...] = _pack_bf16_halves(y)

    row = pl.BlockSpec((1, d), lambda i: (0, 0))
    return pl.pallas_call(
        body, grid=(m // tm,),
        in_specs=[pl.BlockSpec((tm, k), lambda i: (i, 0)), pl.BlockSpec((k, d), lambda i: (0, 0)),
                  pl.BlockSpec((tm, d), lambda i: (i, 0)), row, row],
        out_specs=[pl.BlockSpec((tm, d), lambda i: (i, 0)), pl.BlockSpec((tm, d), lambda i: (i, 0)),
                   pl.BlockSpec((tm, d // 2), lambda i: (i, 0))],
        out_shape=[jax.ShapeDtypeStruct((m, d), F32), jax.ShapeDtypeStruct((m, d), BF16),
                   jax.ShapeDtypeStruct((m, d // 2), jnp.uint32)],
        compiler_params=_cparams(("parallel",)), name="out_proj_ln")(
            merged, w_out, x, g.reshape(1, d), b.reshape(1, d))


def ple_mix(x, xb, w_gate, pe, w_proj):
    m, d = x.shape
    kp = pe.shape[1]
    tm = _pick_tile(m, (1664, 640, 512, 256, 128, 64, 32, 16))
    tn = _pick_tile(d, (512, 256, 128))

    def body(xb_ref, wg_ref, pe_ref, wp_ref, x_ref, o_ref, ob_ref):
        y = x_ref[...] + _sigmoid(_bdot(xb_ref[...], wg_ref[...])) * _bdot(pe_ref[...], wp_ref[...])
        o_ref[...] = y
        ob_ref[...] = y.astype(BF16)

    return pl.pallas_call(
        body, grid=(m // tm, d // tn),
        in_specs=[pl.BlockSpec((tm, d), lambda i, j: (i, 0)), pl.BlockSpec((d, tn), lambda i, j: (0, j)),
                  pl.BlockSpec((tm, kp), lambda i, j: (i, 0)), pl.BlockSpec((kp, tn), lambda i, j: (0, j)),
                  pl.BlockSpec((tm, tn), lambda i, j: (i, j))],
        out_specs=[pl.BlockSpec((tm, tn), lambda i, j: (i, j)), pl.BlockSpec((tm, tn), lambda i, j: (i, j))],
        out_shape=[jax.ShapeDtypeStruct((m, d), F32), jax.ShapeDtypeStruct((m, d), BF16)],
        compiler_params=_cparams(("parallel", "parallel")), name="ple_mix")(xb, w_gate, pe, w_proj, x)


def s5_tables(lam_re, lam_im, log_dt, b_re, b_im, c_re, c_im):
    hp = lax.Precision.HIGHEST
    cs, nt, gl = S5_CS, S5_TILES, LANES // S5_CH
    dt = jnp.exp(log_dt)[:, None]
    mag = jnp.exp(lam_re * dt)
    ab_re, ab_im = mag * jnp.cos(lam_im * dt), mag * jnp.sin(lam_im * dt)
    den = lam_re * lam_re + lam_im * lam_im
    nr = ab_re - 1.0
    co_re = (nr * lam_re + ab_im * lam_im) / den
    co_im = (ab_im * lam_re - nr * lam_im) / den
    bb_re = co_re[..., None] * b_re - co_im[..., None] * b_im
    bb_im = co_re[..., None] * b_im + co_im[..., None] * b_re
    pr, pi = [jnp.ones_like(ab_re)], [jnp.zeros_like(ab_im)]
    for _ in range(cs):
        pr.append(pr[-1] * ab_re - pi[-1] * ab_im)
        pi.append(pr[-2] * ab_im + pi[-1] * ab_re)
    ap_re, ap_im = jnp.stack(pr), jnp.stack(pi)
    abr = ap_re[:, :, :, None] * bb_re - ap_im[:, :, :, None] * bb_im
    abi = ap_re[:, :, :, None] * bb_im + ap_im[:, :, :, None] * bb_re
    kern = (jnp.einsum('gcp,egpd->egcd', c_re, abr[:cs], precision=hp)
            - jnp.einsum('gcp,egpd->egcd', c_im, abi[:cs], precision=hp))
    same_group = jnp.eye(gl, dtype=bool)

    def block_diag(a, g_axis, h_axis):
        shape = [1] * (a.ndim + 1)
        shape[g_axis if g_axis < h_axis else g_axis + 1] = gl
        shape[h_axis] = gl
        return jnp.where(same_group.reshape(shape), jnp.expand_dims(a, h_axis), 0.0).astype(BF16)

    k5 = kern.reshape(cs, nt, gl, S5_CH, S5_CH).transpose(1, 0, 2, 4, 3)
    kcat = block_diag(k5, 2, 4).reshape(nt, cs, LANES, LANES).transpose(0, 2, 1, 3).reshape(nt, LANES, cs * LANES)
    toep = jnp.concatenate([jnp.pad(kcat[:, :, :(cs - s) * LANES], ((0, 0), (0, 0), (s * LANES, 0)))
                            for s in range(cs)], axis=1)
    def state_lanes(re, im, n_rows):
        rep = jnp.tile(jnp.eye(S5_STATE, dtype=F32), (1, gl))
        full = jnp.concatenate([jnp.einsum('...p,pq->...q', re, rep, precision=hp),
                                jnp.einsum('...p,pq->...q', im, rep, precision=hp)], axis=-1)
        full = full.reshape(nt, n_rows, 2 * S5_TSTATE)
        row_g = (jnp.arange(n_rows) // S5_CH) % gl
        col_h = (jnp.arange(2 * S5_TSTATE) // S5_STATE) % gl
        return jnp.where(row_g[:, None] == col_h[None, :], full, 0.0).astype(BF16)

    er = abr[:cs][::-1].reshape(cs, nt, gl, S5_STATE, S5_CH).transpose(1, 0, 2, 4, 3)
    ei = abi[:cs][::-1].reshape(cs, nt, gl, S5_STATE, S5_CH).transpose(1, 0, 2, 4, 3)
    bend = state_lanes(er, ei, cs * LANES)
    car = c_re[None] * ap_re[:, :, None, :] - c_im[None] * ap_im[:, :, None, :]
    cai = -(c_re[None] * ap_im[:, :, None, :] + c_im[None] * ap_re[:, :, None, :])
    car = car.reshape(cs + 1, nt, gl, S5_CH, S5_STATE).transpose(1, 0, 2, 3, 4)
    cai = cai.reshape(cs + 1, nt, gl, S5_CH, S5_STATE).transpose(1, 0, 2, 3, 4)
    ccar_t = state_lanes(car, cai, (cs + 1) * LANES)

    def state_row(re, im):
        return jnp.concatenate([re.reshape(nt, 1, S5_TSTATE), im.reshape(nt, 1, S5_TSTATE)], axis=-1)

    return dict(toep=toep, bend=bend, bbar=bend[:, (cs - 1) * LANES:], c0_t=ccar_t[:, :LANES],
                ccar_t=ccar_t[:, LANES:], a1=state_row(ap_re[1], ap_im[1]), acs=state_row(ap_re[cs], ap_im[cs]))


def _chunk_rows(u_ref, nc):
    return jnp.concatenate([u_ref[pl.ds(s, nc, stride=S5_CS), :].astype(BF16) for s in range(S5_CS)], axis=1)


def s5_chunk_states(u_src, bend, *, t_p):
    nt, kc, n = bend.shape
    nc = t_p // S5_CS
    tn = 512

    def body(u_ref, b_ref, o_ref):
        o_ref[0] = jnp.dot(_chunk_rows(u_ref, nc), b_ref[0], preferred_element_type=F32)

    return pl.pallas_call(
        body, grid=(nt, n // tn),
        in_specs=[pl.BlockSpec((t_p, LANES), lambda j, n_: (0, j)),
                  pl.BlockSpec((1, kc, tn), lambda j, n_: (j, 0, n_))],
        out_specs=pl.BlockSpec((1, nc, tn), lambda j, n_: (j, 0, n_)),
        out_shape=jax.ShapeDtypeStruct((nt, nc, n), F32),
        compiler_params=_cparams(("parallel", "parallel")), name="s5_chunk_states")(u_src, bend)


def s5_carry_scan(xe, acs, n_seq):
    nt, nc, n = xe.shape
    per = nc // n_seq
    half = n // 2

    def body(x_ref, a_ref, hp_ref, hf_ref):
        ar = a_ref[0, :, :half]
        ai = a_ref[0, :, half:]

        def step(k, carry):
            hr, hi = carry
            hp_ref[0, pl.ds(k, 1), :] = jnp.concatenate([hr, hi], axis=1)
            x = x_ref[0, pl.ds(k, 1), :]
            return (ar * hr - ai * hi + x[:, :half], ar * hi + ai * hr + x[:, half:])

        zero = jnp.zeros((1, half), F32)
        hr, hi = lax.fori_loop(0, per, step, (zero, zero))
        hf_ref[0, 0] = jnp.concatenate([hr, hi], axis=1)

    return pl.pallas_call(
        body, grid=(nt, n_seq),
        in_specs=[pl.BlockSpec((1, per, n), lambda j, b: (j, b, 0)), pl.BlockSpec((1, 1, n), lambda j, b: (j, 0, 0))],
        out_specs=[pl.BlockSpec((1, per, n), lambda j, b: (j, b, 0)),
                   pl.BlockSpec((1, 1, 1, n), lambda j, b: (j, b, 0, 0))],
        out_shape=[jax.ShapeDtypeStruct((nt, nc, n), F32), jax.ShapeDtypeStruct((nt, n_seq, 1, n), F32)],
        compiler_params=_cparams(("parallel", "parallel")), name="s5_carry_scan")(xe, acs)


def s5_outputs(u_src, toep, hprev, ccar, d_skip, *, t_p):
    nt, kc, _ = toep.shape
    ns = hprev.shape[2]
    nc = t_p // S5_CS
    tn = 512
    per = tn // LANES

    def body(u_ref, t_ref, h_ref, c_ref, d_ref, o_ref):
        uc = _chunk_rows(u_ref, nc)
        hb = h_ref[0].astype(BF16)
        for n_ in range(kc // tn):
            cols = slice(n_ * tn, (n_ + 1) * tn)
            y = jnp.dot(uc, t_ref[0, :, cols], preferred_element_type=F32) + _dot_nt(hb, c_ref[0, cols, :])
            for i in range(per):
                rows = pl.ds(n_ * per + i, nc, stride=S5_CS)
                o_ref[rows, :] = _gelu_tanh(y[:, i * LANES:(i + 1) * LANES] + d_ref[...] * u_ref[rows, :])

    return pl.pallas_call(
        body, grid=(nt,),
        in_specs=[pl.BlockSpec((t_p, LANES), lambda j: (0, j)),
                  pl.BlockSpec((1, kc, kc), lambda j: (j, 0, 0)), pl.BlockSpec((1, nc, ns), lambda j: (j, 0, 0)),
                  pl.BlockSpec((1, kc, ns), lambda j: (j, 0, 0)), pl.BlockSpec((1, LANES), lambda j: (0, j))],
        out_specs=pl.BlockSpec((t_p, LANES), lambda j: (0, j)),
        out_shape=jax.ShapeDtypeStruct((t_p, nt * LANES), F32),
        compiler_params=_cparams(("parallel",)), name="s5_outputs")(
            u_src, toep, hprev, ccar, d_skip.reshape(1, nt * LANES))


def s5_decode(u_src, h_re, h_im, bbar, a1, c0, d_skip, *, row0):
    s = h_re.shape[0]
    nt = bbar.shape[0]
    w = nt * LANES
    ts = S5_TSTATE
    rb = row0 // s

    def body(u_ref, hr_ref, hi_ref, b_ref, a_ref, c_ref, d_ref, y_ref, nr_ref, ni_ref):
        uu = u_ref[...]
        x = _bdot(uu, b_ref[0])
        ar, ai = a_ref[0, :, :ts], a_ref[0, :, ts:]
        hr, hi = hr_ref[...], hi_ref[...]
        nr = ar * hr - ai * hi + x[:, :ts]
        ni = ar * hi + ai * hr + x[:, ts:]
        nr_ref[...] = nr
        ni_ref[...] = ni
        y = _dot_nt(jnp.concatenate([nr, ni], axis=1), c_ref[0]) + d_ref[...] * uu
        y_ref[...] = _gelu_tanh(y)

    col = pl.BlockSpec((s, LANES), lambda j: (0, j))
    st = pl.BlockSpec((s, ts), lambda j: (0, j))
    tab = pl.BlockSpec((1, LANES, 2 * ts), lambda j: (j, 0, 0))
    return pl.pallas_call(
        body, grid=(nt,),
        in_specs=[pl.BlockSpec((s, LANES), lambda j: (rb, j)), st, st, tab,
                  pl.BlockSpec((1, 1, 2 * ts), lambda j: (j, 0, 0)), tab,
                  pl.BlockSpec((1, LANES), lambda j: (0, j))],
        out_specs=[col, st, st],
        out_shape=[jax.ShapeDtypeStruct((s, w), F32), jax.ShapeDtypeStruct(h_re.shape, F32),
                   jax.ShapeDtypeStruct(h_im.shape, F32)],
        compiler_params=_cparams(("parallel",)), name="s5_decode")(u_src, h_re, h_im, bbar, a1, c0, d_skip.reshape(1, w))


def s5_branch(u_src, n_seq, seq_len, h_re, h_im, tabs, d_skip):
    t_p = n_seq * seq_len
    nt = S5_TILES
    xe = s5_chunk_states(u_src, tabs['bend'], t_p=t_p)
    hprev, hfin = s5_carry_scan(xe, tabs['acs'], n_seq)
    y_p = s5_outputs(u_src, tabs['toep'], hprev, tabs['ccar_t'], d_skip, t_p=t_p)
    hfin = hfin.reshape(nt, n_seq, 2, S5_TSTATE).transpose(2, 1, 0, 3).reshape(2, n_seq, S5_GROUPS, S5_STATE)
    s_rows = u_src.shape[0] - t_p
    y_s, nr, ni = s5_decode(u_src, h_re.reshape(s_rows, -1), h_im.reshape(s_rows, -1), tabs['bbar'],
                            tabs['a1'], tabs['c0_t'], d_skip, row0=t_p)
    return (jnp.concatenate([y_p, y_s], axis=0), hfin[0], hfin[1], nr.reshape(h_re.shape), ni.reshape(h_im.shape))


def moe_route(xb, w_router_t, b_router, *, tm):
    t, d = xb.shape
    ne, ng, gs = N_EXPERTS, N_GROUPS, N_EXPERTS // N_GROUPS
    neg = -jnp.inf

    def body(x_ref, w_ref, b_ref, u_ref, idx_ref, wt_ref, rank_ref, cnt_ref, carry_ref):
        @pl.when(pl.program_id(0) == 0)
        def _():
            carry_ref[...] = jnp.zeros_like(carry_ref)

        logits = lax.dot_general(w_ref[...], x_ref[...], (((1,), (1,)), ((), ())), preferred_element_type=F32)
        scores = _sigmoid(logits).reshape(ng, gs, tm)
        choice = scores + b_ref[...].reshape(ng, gs, 1)
        e_in = lax.broadcasted_iota(I32, (ng, gs, tm), 1).astype(F32)
        g_id = lax.broadcasted_iota(I32, (ng, 1, tm), 0).astype(F32)
        e_id = g_id * gs + e_in
        m1 = jnp.max(choice, axis=1, keepdims=True)
        i1 = jnp.min(jnp.where(choice == m1, e_in, float(gs)), axis=1, keepdims=True)
        m2 = jnp.max(jnp.where(e_in == i1, neg, choice), axis=1, keepdims=True)
        gscore = m1 + m2
        keep = jnp.zeros((ng, 1, tm), F32)
        for _ in range(TOPK_GROUPS):
            gm = jnp.max(gscore, axis=0, keepdims=True)
            gi = jnp.min(jnp.where(gscore == gm, g_id, float(ng)), axis=0, keepdims=True)
            hit = g_id == gi
            keep = jnp.where(hit, 1.0, keep)
            gscore = jnp.where(hit, neg, gscore)
        cand = jnp.where(keep > 0.0, choice, neg)
        member = jnp.zeros((ng, gs, tm), F32)
        picks, wts = [], []
        for _ in range(TOP_K):
            cm = jnp.max(jnp.max(cand, axis=1, keepdims=True), axis=0, keepdims=True)
            ei = jnp.min(jnp.min(jnp.where(cand == cm, e_id, float(ne)), axis=1, keepdims=True), axis=0, keepdims=True)
            sel = e_id == ei
            wts.append(jnp.sum(jnp.sum(jnp.where(sel, scores, 0.0), axis=1, keepdims=True), axis=0, keepdims=True))
            picks.append(ei)
            member = jnp.where(sel, 1.0, member)
            cand = jnp.where(sel, neg, cand)
        wsum = wts[0]
        for w in wts[1:]:
            wsum = wsum + w
        member2 = member.reshape(ne, tm)
        prefix = jnp.dot(member2.astype(BF16), u_ref[...], preferred_element_type=F32) + carry_ref[:, 0:1]
        prefix = prefix.reshape(ng, gs, tm)
        for j in range(TOP_K):
            sel = e_id == picks[j]
            rk = jnp.sum(jnp.sum(jnp.where(sel, prefix, 0.0), axis=1, keepdims=True), axis=0, keepdims=True)
            idx_ref[j:j + 1, :] = picks[j].reshape(1, tm).astype(I32)
            rank_ref[j:j + 1, :] = rk.reshape(1, tm).astype(I32)
            wt_ref[j:j + 1, :] = (wts[j] / wsum * ROUTED_SCALE).reshape(1, tm)
        carry_ref[...] = carry_ref[...] + jnp.sum(member2, axis=1, keepdims=True)
        cnt_ref[...] = carry_ref[...]

    upper = jnp.triu(jnp.ones((tm, tm), F32), 1).astype(BF16)
    tok = pl.BlockSpec((TOP_K, tm), lambda i: (0, i))
    idx, wt, rank, cnt = pl.pallas_call(
        body, grid=(t // tm,),
        in_specs=[pl.BlockSpec((tm, d), lambda i: (i, 0)), pl.BlockSpec((ne, d), lambda i: (0, 0)),
                  pl.BlockSpec((ne, 1), lambda i: (0, 0)), pl.BlockSpec((tm, tm), lambda i: (0, 0))],
        out_specs=[tok, tok, tok, pl.BlockSpec((ne, LANES), lambda i: (0, 0))],
        out_shape=[jax.ShapeDtypeStruct((TOP_K, t), I32), jax.ShapeDtypeStruct((TOP_K, t), F32),
                   jax.ShapeDtypeStruct((TOP_K, t), I32), jax.ShapeDtypeStruct((ne, LANES), F32)],
        scratch_shapes=[pltpu.VMEM((ne, LANES), F32)],
        compiler_params=_cparams(("arbitrary",)), name="moe_route")(xb, w_router_t, b_router.reshape(ne, 1), upper)
    return idx, wt, rank, cnt[:, 0]


def moe_experts(x, row_tok, block_e, n_used, w1, w3, w2, *, rows, layer):
    t, dh = x.shape
    d = 2 * dh
    nb = row_tok.shape[0]
    f = w1.shape[3]

    def body(be_ref, nu_ref, x_hbm, tok_ref, tokn_ref, w1_ref, w3_ref, w2_ref, o_ref, buf, sem, w1b, w3b, w2b):
        i = pl.program_id(0)
        slot = lax.rem(i, 2)
        nxt = 1 - slot
        groups = 4
        per = rows // groups

        def row_copy(tref, r, sl):
            tok = tref[0, 0, r]
            return pltpu.make_async_copy(x_hbm.at[pl.ds(tok, 1), :], buf.at[sl, pl.ds(r, 1), :], sem.at[sl])

        def gather_loop(tref, sl):
            def issue(r, c):
                row_copy(tref, r, sl).start(priority=1)
                return c
            lax.fori_loop(0, rows, issue, 0, unroll=8)

        def gather_group(tref, sl, g):
            for r in range(g * per, (g + 1) * per):
                row_copy(tref, r, sl).start(priority=1)

        def block_wait(sl):
            pltpu.make_async_copy(x_hbm.at[pl.ds(0, rows), :], buf.at[sl], sem.at[sl]).wait()

        @pl.when(i == 0)
        def _():
            gather_loop(tok_ref, 0)

        e = be_ref[i]
        changed = jnp.logical_or(i == 0, e != be_ref[jnp.maximum(i - 1, 0)])

        @pl.when(changed)
        def _():
            w1b[...] = w1_ref[0, 0].astype(BF16)
            w3b[...] = w3_ref[0, 0].astype(BF16)
            w2b[...] = w2_ref[0, 0].astype(BF16)

        block_wait(slot)

        @pl.when(i < nu_ref[0])
        def _():
            lo, hi = _unpack_bf16_halves(buf[slot])
            lo, hi = lo.astype(BF16), hi.astype(BF16)
            gather_group(tokn_ref, nxt, 0)
            h1 = (jnp.dot(lo, w1b[:dh], preferred_element_type=F32)
                  + jnp.dot(hi, w1b[dh:], preferred_element_type=F32))
            gather_group(tokn_ref, nxt, 1)
            h3 = (jnp.dot(lo, w3b[:dh], preferred_element_type=F32)
                  + jnp.dot(hi, w3b[dh:], preferred_element_type=F32))
            gather_group(tokn_ref, nxt, 2)
            y = jnp.dot((_silu(h1) * h3).astype(BF16), w2b[...], preferred_element_type=F32)
            gather_group(tokn_ref, nxt, 3)
            o_ref[...] = _pack_bf16_halves(y)

        @pl.when(i >= nu_ref[0])
        def _():
            gather_loop(tokn_ref, nxt)
            o_ref[...] = jnp.zeros_like(o_ref)

        @pl.when(i == nb - 1)
        def _():
            block_wait(nxt)

    grid_spec = pltpu.PrefetchScalarGridSpec(
        num_scalar_prefetch=2, grid=(nb,),
        in_specs=[pl.BlockSpec(memory_space=pl.ANY),
                  pl.BlockSpec((1, 1, rows), lambda i, be, nu: (i, 0, 0), memory_space=pltpu.SMEM),
                  pl.BlockSpec((1, 1, rows), lambda i, be, nu: (jnp.minimum(i + 1, nb - 1), 0, 0),
                               memory_space=pltpu.SMEM),
                  pl.BlockSpec((1, 1, d, f), lambda i, be, nu: (layer, be[i], 0, 0)),
                  pl.BlockSpec((1, 1, d, f), lambda i, be, nu: (layer, be[i], 0, 0)),
                  pl.BlockSpec((1, 1, f, d), lambda i, be, nu: (layer, be[i], 0, 0))],
        out_specs=pl.BlockSpec((rows, dh), lambda i, be, nu: (i, 0)),
        scratch_shapes=[pltpu.VMEM((2, rows, dh), jnp.uint32), pltpu.SemaphoreType.DMA((2,)),
                        pltpu.VMEM((d, f), BF16), pltpu.VMEM((d, f), BF16), pltpu.VMEM((f, d), BF16)])
    return pl.pallas_call(
        body, grid_spec=grid_spec, out_shape=jax.ShapeDtypeStruct((nb * rows, dh), jnp.uint32),
        compiler_params=_cparams(("arbitrary",)), name="moe_experts")(block_e, n_used, x, row_tok, row_tok, w1, w3, w2)


def moe_combine_ln(ys, dest, wts, x, shared, g, b, *, tm):
    t, d = x.shape
    nt = t // tm

    def body(ys_hbm, d_ref, dn_ref, w_ref, x_ref, s_ref, g_ref, b_ref, o_ref, ob_ref, buf, sem):
        i = pl.program_id(0)
        slot = lax.rem(i, 2)

        def gather(dref, sl):
            def issue(r, c):
                row = dref[0, 0, r]
                pltpu.make_async_copy(ys_hbm.at[pl.ds(row, 1), :], buf.at[sl, pl.ds(r, 1), :],
                                      sem.at[sl]).start(priority=1)
                return c
            lax.fori_loop(0, TOP_K * tm, issue, 0, unroll=8)

        @pl.when(i == 0)
        def _():
            gather(d_ref, 0)

        @pl.when(i + 1 < nt)
        def _():
            gather(dn_ref, 1 - slot)

        pltpu.make_async_copy(ys_hbm.at[pl.ds(0, TOP_K * tm), :], buf.at[slot], sem.at[slot]).wait()
        w = w_ref[...]
        acc_lo = jnp.zeros((tm, d // 2), F32)
        acc_hi = jnp.zeros((tm, d // 2), F32)
        for j in range(TOP_K):
            lo, hi = _unpack_bf16_halves(buf[slot, j * tm:(j + 1) * tm, :])
            acc_lo = acc_lo + w[:, j:j + 1] * lo
            acc_hi = acc_hi + w[:, j:j + 1] * hi
        acc = DEEPNORM_ALPHA * x_ref[...] + s_ref[...] + jnp.concatenate([acc_lo, acc_hi], axis=1)
        y = _layer_norm_rows(acc, g_ref[...], b_ref[...])
        o_ref[...] = y
        ob_ref[...] = y.astype(BF16)

    row = pl.BlockSpec((1, d), lambda i: (0, 0))
    tile = pl.BlockSpec((tm, d), lambda i: (i, 0))
    return pl.pallas_call(
        body, grid=(nt,),
        in_specs=[pl.BlockSpec(memory_space=pl.ANY),
                  pl.BlockSpec((1, 1, TOP_K * tm), lambda i: (i, 0, 0), memory_space=pltpu.SMEM),
                  pl.BlockSpec((1, 1, TOP_K * tm), lambda i: (jnp.minimum(i + 1, nt - 1), 0, 0),
                               memory_space=pltpu.SMEM),
                  pl.BlockSpec((tm, TOP_K), lambda i: (i, 0)), tile, tile, row, row],
        out_specs=[tile, tile],
        out_shape=[jax.ShapeDtypeStruct((t, d), F32), jax.ShapeDtypeStruct((t, d), BF16)],
        scratch_shapes=[pltpu.VMEM((2, TOP_K * tm, d // 2), jnp.uint32), pltpu.SemaphoreType.DMA((2,))],
        compiler_params=_cparams(("arbitrary",)), name="moe_combine_ln")(
            ys, dest, dest, wts, x, shared, g.reshape(1, d), b.reshape(1, d))


def moe_layer(x, xb, xp, lw, *, route_tm, rows=MOE_ROWS, tok_tile=MOE_TOK_TILE):
    t, d = x.shape
    idx, wt, rank, counts = moe_route(xb, lw['moe_w_router'].T.astype(BF16), lw['moe_b_router'], tm=route_tm)
    counts = counts.astype(I32)
    padded = (counts + rows - 1) // rows * rows
    pad_end = jnp.cumsum(padded)
    pad_start = pad_end - padded
    onehot = idx[:, :, None] == jnp.arange(N_EXPERTS, dtype=I32)[None, None, :]
    dest = jnp.sum(jnp.where(onehot, pad_start[None, None, :], 0), axis=-1) + rank
    n_rows = -(-(t * TOP_K + N_EXPERTS * (rows - 1)) // rows) * rows
    nb = n_rows // rows
    tok_id = jnp.broadcast_to(jnp.arange(t, dtype=I32)[None, :], (TOP_K, t))
    row_tok = jnp.zeros((n_rows,), I32).at[dest.reshape(-1)].set(tok_id.reshape(-1))
    blk_start = jnp.arange(nb, dtype=I32) * rows
    block_e = jnp.minimum(jnp.sum((pad_end[None, :] <= blk_start[:, None]).astype(I32), axis=1), N_EXPERTS - 1)
    n_used = (pad_end[-1] // rows).astype(I32).reshape(1)
    ys = moe_experts(xp, row_tok.reshape(nb, 1, rows), block_e, n_used, lw['moe_w1'], lw['moe_w3'], lw['moe_w2'],
                     rows=rows, layer=lw['layer'])
    hs = swiglu_hidden(xb, lw['moe_ws1'], lw['moe_ws3'])
    shared = dense(hs, lw['moe_ws2'])
    dest_t = dest.reshape(TOP_K, t // tok_tile, tok_tile).transpose(1, 0, 2).reshape(t // tok_tile, 1, TOP_K * tok_tile)
    return moe_combine_ln(ys, dest_t, wt.T, x, shared, lw['ln2_g'], lw['ln2_b'], tm=tok_tile)


def _level_tables(c):
    import numpy as np
    idx = np.arange(c)
    t, r = idx[:, None], idx[None, :]
    wl, pm = [], []
    b = 1
    while b < c:
        blk, odd = t // b, (t // b) % 2 == 1
        w = np.where(odd, (r >= blk * b) & (r <= t), (r > t) & (r <= blk * b + b - 1))
        wl.append(w.astype(np.float32))
        pm.append((odd & (r // b == blk - 1)).astype(np.float32))
        b *= 2
    incl = (r <= t).astype(np.float32)
    after = (r > t).astype(np.float32)
    return wl, pm, incl, after


def _split3(x):
    hi = x.astype(BF16)
    r1 = x - hi.astype(F32)
    mid = r1.astype(BF16)
    lo = (r1 - mid.astype(F32)).astype(BF16)
    return hi, mid, lo


def _table_dot(tab, x):
    hi, mid, lo = _split3(x)
    return (jnp.dot(tab, hi, preferred_element_type=F32) + jnp.dot(tab, mid, preferred_element_type=F32)
            + jnp.dot(tab, lo, preferred_element_type=F32))


def _dot_hi(a, b):
    ah = a.astype(BF16)
    al = (a - ah.astype(F32)).astype(BF16)
    bh = b.astype(BF16)
    bl = (b - bh.astype(F32)).astype(BF16)
    return (jnp.dot(ah, bh, preferred_element_type=F32) + jnp.dot(ah, bl, preferred_element_type=F32)
            + jnp.dot(al, bh, preferred_element_type=F32))


def _dot_nt(a, b):
    return lax.dot_general(a.astype(BF16), b.astype(BF16), (((1,), (1,)), ((), ())), preferred_element_type=F32)


def _dot_tn(a, b):
    return lax.dot_general(a.astype(BF16), b.astype(BF16), (((0,), (0,)), ((), ())), preferred_element_type=F32)


def _rms_rows(o, w):
    return o * lax.rsqrt(jnp.mean(o * o, axis=-1, keepdims=True) + NORM_EPS) * w


def gla_prompt(z1, log_a, norm_w, *, n_seq, seq_len):
    c, h_, dk, dv = CHUNK, GLA_HEADS, GLA_DK, GLA_DV
    nck = seq_len // c
    wl, pm, incl, after = _level_tables(c)
    nl = len(wl)
    wcat = jnp.asarray(jnp.concatenate([jnp.asarray(w) for w in wl] + [jnp.asarray(incl), jnp.asarray(after)], axis=0),
                       BF16)
    pmask = jnp.stack([jnp.eye(c, dtype=F32)] + [jnp.asarray(p) for p in pm])
    scale = dk ** -0.5

    def body(q_ref, k_ref, v_ref, r_ref, g_ref, w_ref, p_ref, n_ref, o_ref, st_ref, s_scr):
        ci = pl.program_id(1)

        @pl.when(ci == 0)
        def _():
            s_scr[...] = jnp.zeros_like(s_scr)

        x = _table_dot(w_ref[...], g_ref[...])
        ex = jnp.exp(x)
        for h in range(h_):
            ks = slice(h * dk, (h + 1) * dk)
            vs = slice(h * dv, (h + 1) * dv)
            q = q_ref[:, ks] * scale
            k = k_ref[:, ks]
            v = v_ref[:, vs]
            scores = p_ref[0] * _dot_nt(q, k)
            for l in range(nl):
                f = ex[l * c:(l + 1) * c, ks]
                scores = scores + p_ref[l + 1] * _dot_nt(q * f, k * f)
            st = s_scr[h]
            o = _dot_nt(q * ex[nl * c:(nl + 1) * c, ks], st) + _bdot(scores, v)
            tot = x[(nl + 1) * c - 1:(nl + 1) * c, ks]
            s_scr[h] = jnp.exp(tot) * st + _dot_tn(v, k * ex[(nl + 1) * c:(nl + 2) * c, ks])
            o_ref[:, vs] = (_rms_rows(o, n_ref[...]) * _silu(r_ref[:, vs])).astype(o_ref.dtype)

        @pl.when(ci == nck - 1)
        def _():
            st_ref[0] = s_scr[...]

    def rows(width, col):
        return pl.BlockSpec((c, width), lambda b, i: (b * nck + i, col))

    return pl.pallas_call(
        body, grid=(n_seq, nck),
        in_specs=[rows(h_ * dk, 0), rows(h_ * dk, 1), rows(h_ * dv, 1), rows(h_ * dv, 2), rows(h_ * dk, 0),
                  pl.BlockSpec(wcat.shape, lambda b, i: (0, 0)), pl.BlockSpec(pmask.shape, lambda b, i: (0, 0, 0)),
                  pl.BlockSpec((1, dv), lambda b, i: (0, 0))],
        out_specs=[rows(h_ * dv, 0), pl.BlockSpec((1, h_, dv, dk), lambda b, i: (b, 0, 0, 0))],
        out_shape=[jax.ShapeDtypeStruct((n_seq * seq_len, h_ * dv), BF16),
                   jax.ShapeDtypeStruct((n_seq, h_, dv, dk), F32)],
        scratch_shapes=[pltpu.VMEM((h_, dv, dk), F32)],
        compiler_params=_cparams(("parallel", "arbitrary")), name="gla_prompt")(
            z1, z1, z1, z1, log_a, wcat, pmask, norm_w.reshape(1, dv))


def _columns(rows_list, width):
    used = sum(r.shape[0] for r in rows_list)
    stack = jnp.concatenate(list(rows_list) + [jnp.zeros((LANES - used, width), F32)], axis=0)
    return stack.T


def gla_decode(z1, log_a, state, norm_w, *, row0, layer):
    sb = 8
    n_s = state.shape[1]
    h_, dk, dv = GLA_HEADS, GLA_DK, GLA_DV
    scale = dk ** -0.5
    r0 = row0 // sb

    def body(q_ref, k_ref, v_ref, r_ref, g_ref, s_ref, n_ref, o_ref, ns_ref):
        for h in range(h_):
            ks = slice(h * dk, (h + 1) * dk)
            vs = slice(h * dv, (h + 1) * dv)
            cols = _columns([jnp.exp(g_ref[:, ks]), k_ref[:, ks], q_ref[:, ks] * scale], dk)
            v = v_ref[:, vs]
            outs = []
            for s in range(sb):
                s_new = cols[:, s:s + 1] * s_ref[0, s, h] + cols[:, sb + s:sb + s + 1] * v[s:s + 1, :]
                ns_ref[s, h] = s_new
                outs.append(jnp.sum(cols[:, 2 * sb + s:2 * sb + s + 1] * s_new, axis=0, keepdims=True))
            o = jnp.concatenate(outs, axis=0)
            o_ref[:, vs] = _rms_rows(o, n_ref[...]) * _silu(r_ref[:, vs])

    def rows(width, col):
        return pl.BlockSpec((sb, width), lambda i: (r0 + i, col))

    st_in = pl.BlockSpec((1, sb, h_, dk, dv), lambda i: (layer, i, 0, 0, 0))
    st = pl.BlockSpec((sb, h_, dk, dv), lambda i: (i, 0, 0, 0))
    return pl.pallas_call(
        body, grid=(n_s // sb,),
        in_specs=[rows(h_ * dk, 0), rows(h_ * dk, 1), rows(h_ * dv, 1), rows(h_ * dv, 2), rows(h_ * dk, 0), st_in,
                  pl.BlockSpec((1, dv), lambda i: (0, 0))],
        out_specs=[pl.BlockSpec((sb, h_ * dv), lambda i: (i, 0)), st],
        out_shape=[jax.ShapeDtypeStruct((n_s, h_ * dv), F32), jax.ShapeDtypeStruct(state.shape[1:], F32)],
        compiler_params=_cparams(("parallel",)), name="gla_decode")(
            z1, z1, z1, z1, log_a, state, norm_w.reshape(1, dv))


def _conv_silu(ext, w, c):
    acc = ext[5:5 + c] * w[0:1]
    for i in range(1, GDN_CONV):
        acc = acc + ext[5 + i:5 + i + c] * w[i:i + 1]
    return _silu(acc)


def _softplus(x):
    return jnp.maximum(x, 0.0) + jnp.log1p(jnp.exp(-jnp.abs(x)))


def _l2n(x):
    return x * lax.rsqrt(jnp.sum(x * x, axis=-1, keepdims=True) + NORM_EPS)


def gdn_prompt(z3, z4, conv_w, a_log, dt_bias, norm_w, *, n_seq, seq_len):
    c, h_, dk, dv = CHUNK, GDN_HEADS, GDN_DK, GDN_DV
    kw = h_ * dk
    nck = seq_len // c
    _, pm, incl, after = _level_tables(c)
    nl = len(pm)
    import numpy as np
    strict = (np.arange(c)[:, None] > np.arange(c)[None, :]).astype(np.float32)
    tabs = jnp.asarray(np.concatenate([incl, after, np.ones((c, c), np.float32)], axis=0), BF16)
    masks = jnp.stack([jnp.asarray(incl), jnp.asarray(strict), jnp.eye(c, dtype=F32)] + [jnp.asarray(p) for p in pm])
    strict_pad = jnp.asarray(np.concatenate([strict, np.zeros((c, LANES - c), np.float32)], axis=1))
    qscale = dk ** -0.5

    def body(q_ref, k_ref, v_ref, zg_ref, ab_ref, cw_ref, al_ref, db_ref, t_ref, m_ref, sp_ref, n_ref,
             o_ref, st_ref, s_scr, hist):
        ci = pl.program_id(1)

        @pl.when(ci == 0)
        def _():
            s_scr[...] = jnp.zeros_like(s_scr)
            hist[...] = jnp.zeros_like(hist)

        def conv(ref, j):
            cols = slice(j * kw, (j + 1) * kw)
            raw = ref[...]
            ext = jnp.concatenate([hist[:, cols], raw], axis=0)
            y = _conv_silu(ext, cw_ref[:, cols], c)
            hist[:, cols] = raw[c - 8:c]
            return y

        qc, kc, vc = conv(q_ref, 0), conv(k_ref, 1), conv(v_ref, 2)
        ab = ab_ref[...]
        g = -jnp.exp(al_ref[...]) * _softplus(ab[:, :h_] + db_ref[...])
        beta = _sigmoid(ab[:, h_:])
        sums = _table_dot(t_ref[...], jnp.concatenate([g, jnp.zeros((c, LANES - h_), F32)], axis=1))
        e_cum = jnp.exp(sums[0:c])
        e_rest = jnp.exp(sums[c:2 * c])
        e_last = jnp.exp(sums[2 * c:2 * c + 1])
        grel = jnp.concatenate([g[:, h:h + 1] * sp_ref[...] for h in range(h_)], axis=1)
        rel = _table_dot(t_ref[0:c], grel)
        m_incl, m_strict, m_eye = m_ref[0], m_ref[1], m_ref[2]
        qs, ks, vs, kbs, decs, amat, tinv = [], [], [], [], [], [], []
        for h in range(h_):
            hs = slice(h * dk, (h + 1) * dk)
            qs.append(_l2n(qc[:, hs]) * qscale)
            ks.append(_l2n(kc[:, hs]))
            vs.append(vc[:, hs])
            decs.append(m_incl * jnp.exp(m_incl * rel[:, h * LANES:h * LANES + c]))
            kbs.append(ks[h] * beta[:, h:h + 1])
            amat.append(m_strict * _dot_nt(kbs[h], ks[h]) * decs[h])
            tinv.append(m_eye - m_ref[3] * amat[h])
        for l in range(1, nl):
            tinv = [tinv[h] - _dot_hi(_dot_hi(tinv[h], m_ref[3 + l] * amat[h]), tinv[h]) for h in range(h_)]
        for h in range(h_):
            hs = slice(h * dk, (h + 1) * dk)
            q, k, v, kb, dec, t = qs[h], ks[h], vs[h], kbs[h], decs[h], tinv[h]
            bcol = beta[:, h:h + 1]
            tw = _bdot(t, jnp.concatenate([kb * e_cum[:, h:h + 1], v * bcol], axis=1))
            s_old = s_scr[h]
            both = _bdot(jnp.concatenate([q * e_cum[:, h:h + 1], tw[:, :dk]], axis=0), s_old)
            u = tw[:, dk:] - both[c:]
            o = both[:c] + _bdot(_dot_nt(q, k) * dec, u)
            s_scr[h] = e_last[:, h:h + 1] * s_old + _dot_tn(k * e_rest[:, h:h + 1], u)
            o_ref[:, hs] = (_rms_rows(o, n_ref[...]) * _silu(zg_ref[:, hs])).astype(o_ref.dtype)

        @pl.when(ci == nck - 1)
        def _():
            st_ref[0] = s_scr[...]

    def rows(width, col):
        return pl.BlockSpec((c, width), lambda b, i: (b * nck + i, col))

    def const(arr):
        nd = arr.ndim
        return pl.BlockSpec(arr.shape, lambda b, i: (0,) * nd)

    cw = conv_w
    al = a_log.reshape(1, h_)
    db = dt_bias.reshape(1, h_)
    nw = norm_w.reshape(1, dv)
    return pl.pallas_call(
        body, grid=(n_seq, nck),
        in_specs=[rows(kw, 1), rows(kw, 2), rows(kw, 3), rows(kw, 4), pl.BlockSpec((c, 2 * h_), lambda b, i: (b * nck + i, 0)),
                  const(cw), const(al), const(db), const(tabs), const(masks), const(strict_pad), const(nw)],
        out_specs=[rows(kw, 0), pl.BlockSpec((1, h_, dk, dv), lambda b, i: (b, 0, 0, 0))],
        out_shape=[jax.ShapeDtypeStruct((n_seq * seq_len, kw), BF16), jax.ShapeDtypeStruct((n_seq, h_, dk, dv), F32)],
        scratch_shapes=[pltpu.VMEM((h_, dk, dv), F32), pltpu.VMEM((8, 3 * kw), F32)],
        compiler_params=_cparams(("parallel", "arbitrary")), name="gdn_prompt")(
            z3, z3, z3, z3, z4, cw, al, db, tabs, masks, strict_pad, nw)


def gdn_decode(z3, z4, state, conv_buf, conv_w, a_log, dt_bias, norm_w, *, row0, layer):
    sb = 8
    n_s = state.shape[1]
    h_, dk, dv = GDN_HEADS, GDN_DK, GDN_DV
    kw = h_ * dk
    r0 = row0 // sb
    qscale = dk ** -0.5

    def body(q_ref, k_ref, v_ref, zg_ref, ab_ref, hb_ref, cw_ref, al_ref, db_ref, s_ref, n_ref, o_ref, ns_ref):
        def conv(ref, j):
            cols = slice(j * kw, (j + 1) * kw)
            acc = ref[...] * cw_ref[GDN_CONV - 1:GDN_CONV, cols]
            for i in range(GDN_CONV - 1):
                acc = acc + hb_ref[:, i, cols] * cw_ref[i:i + 1, cols]
            return _silu(acc)

        qc, kc, vc = conv(q_ref, 0), conv(k_ref, 1), conv(v_ref, 2)
        ab = ab_ref[...]
        eg = jnp.exp(-jnp.exp(al_ref[...]) * _softplus(ab[:, :h_] + db_ref[...]))
        beta = _sigmoid(ab[:, h_:])
        for h in range(h_):
            hs = slice(h * dk, (h + 1) * dk)
            q = _l2n(qc[:, hs]) * qscale
            k = _l2n(kc[:, hs])
            v = vc[:, hs]
            cols = _columns([k, q], dk)
            qk = jnp.sum(q * k, axis=-1, keepdims=True)
            outs = []
            for s in range(sb):
                s_old = s_ref[0, s, h]
                kcol = cols[:, s:s + 1]
                k_s = jnp.sum(kcol * s_old, axis=0, keepdims=True)
                q_s = jnp.sum(cols[:, sb + s:sb + s + 1] * s_old, axis=0, keepdims=True)
                e = eg[s:s + 1, h:h + 1]
                u = beta[s:s + 1, h:h + 1] * (v[s:s + 1, :] - e * k_s)
                ns_ref[s, h] = e * s_old + kcol * u
                outs.append(e * q_s + qk[s:s + 1, :] * u)
            o = jnp.concatenate(outs, axis=0)
            o_ref[:, hs] = _rms_rows(o, n_ref[...]) * _silu(zg_ref[:, hs])

    def rows(width, col):
        return pl.BlockSpec((sb, width), lambda i: (r0 + i, col))

    def const(arr):
        nd = arr.ndim
        return pl.BlockSpec(arr.shape, lambda i: (0,) * nd)

    st_in = pl.BlockSpec((1, sb, h_, dk, dv), lambda i: (layer, i, 0, 0, 0))
    st = pl.BlockSpec((sb, h_, dk, dv), lambda i: (i, 0, 0, 0))
    al = a_log.reshape(1, h_)
    db = dt_bias.reshape(1, h_)
    nw = norm_w.reshape(1, dv)
    return pl.pallas_call(
        body, grid=(n_s // sb,),
        in_specs=[rows(kw, 1), rows(kw, 2), rows(kw, 3), rows(kw, 4), pl.BlockSpec((sb, 2 * h_), lambda i: (r0 + i, 0)),
                  pl.BlockSpec((sb, GDN_CONV - 1, 3 * kw), lambda i: (i, 0, 0)), const(conv_w), const(al), const(db),
                  st_in, const(nw)],
        out_specs=[pl.BlockSpec((sb, kw), lambda i: (i, 0)), st],
        out_shape=[jax.ShapeDtypeStruct((n_s, kw), F32), jax.ShapeDtypeStruct(state.shape[1:], F32)],
        compiler_params=_cparams(("parallel",)), name="gdn_decode")(
            z3, z3, z3, z3, z4, conv_buf, conv_w, al, db, state, nw)


def trunk_layer(x, xb, pe_b, states, lw, *, n_seq, seq_len, route_tm):
    t_p = n_seq * seq_len
    n_s = x.shape[0] - t_p
    gla_s, s5_re, s5_im, gdn_s, conv_s = states
    w_in = lw['w_in']
    z1 = dense(xb, w_in[:, 0:3072].astype(BF16))
    a_lr = dense(xb, w_in[:, 3072:3088].astype(BF16))
    z3 = dense(xb, w_in[:, 3088:8208].astype(BF16))
    z4 = dense(xb, w_in[:, 8208:8224].astype(BF16))
    log_a = dense(a_lr, lw['gla_w_gate'], bias=lw['gla_b_gate'], act='log_decay')
    br_a_p, gla_pt = gla_prompt(z1, log_a, lw['gla_norm'], n_seq=n_seq, seq_len=seq_len)
    gla_p = jnp.swapaxes(gla_pt, 2, 3)
    br_a_s, gla_n = gla_decode(z1, log_a, gla_s, lw['gla_norm'], row0=t_p, layer=lw['layer'])
    br_a = jnp.concatenate([br_a_p, br_a_s.astype(BF16)], axis=0)
    tabs = s5_tables(lw['s5_lam_re'], lw['s5_lam_im'], lw['s5_log_dt'], lw['s5_b_re'], lw['s5_b_im'],
                     lw['s5_c_re'], lw['s5_c_im'])
    y_s, s5r_p, s5i_p, s5r_n, s5i_n = s5_branch(z3, n_seq, seq_len, s5_re, s5_im, tabs, lw['s5_d'])
    br_s = glu_gate(y_s, lw['s5_w_glu'].astype(BF16), lw['s5_b_glu'])
    br_c_p, gdn_p = gdn_prompt(z3, z4, lw['gdn_conv_w'], lw['gdn_a_log'], lw['gdn_dt_bias'], lw['gdn_norm'],
                               n_seq=n_seq, seq_len=seq_len)
    br_c_s, gdn_n = gdn_decode(z3, z4, gdn_s, conv_s, lw['gdn_conv_w'], lw['gdn_a_log'], lw['gdn_dt_bias'],
                               lw['gdn_norm'], row0=t_p, layer=lw['layer'])
    br_c = jnp.concatenate([br_c_p, br_c_s.astype(BF16)], axis=0)
    qkv_cols = slice(S5_WIDTH, S5_WIDTH + 2 * GDN_KW + GDN_VW)
    conv_p = jnp.stack([z3[(b + 1) * seq_len - (GDN_CONV - 1):(b + 1) * seq_len, qkv_cols] for b in range(n_seq)])
    conv_n = jnp.concatenate([conv_s[:, 1:], z3[t_p:, None, qkv_cols]], axis=1)
    merged = merge_branches(xb, w_in[:, 8224:].astype(BF16), br_a, br_s, br_c, lw['w_branch_a'].astype(BF16),
                            lw['w_branch_s'].astype(BF16), lw['w_branch_c'].astype(BF16))
    x1, x1b, x1p = out_proj_ln(merged, lw['w_out'].astype(BF16), x, lw['ln1_g'], lw['ln1_b'])
    x2, x2b = moe_layer(x1, x1b, x1p, lw, route_tm=route_tm)
    x3, x3b = ple_mix(x2, x2b, lw['ple_w_gate'].astype(BF16), pe_b, lw['ple_w_proj'].astype(BF16))
    return x3, x3b, (gla_p, s5r_p, s5i_p, gdn_p, conv_p), (gla_n, s5r_n, s5i_n, gdn_n, conv_n)


_NAMES = ('w_in', 'gla_w_gate', 'gla_b_gate', 'gla_norm', 's5_lam_re', 's5_lam_im', 's5_log_dt', 's5_b_re',
          's5_b_im', 's5_c_re', 's5_c_im', 's5_d', 's5_w_glu', 's5_b_glu', 'gdn_conv_w', 'gdn_a_log',
          'gdn_dt_bias', 'gdn_norm', 'w_branch_a', 'w_branch_s', 'w_branch_c', 'w_out', 'ln1_g', 'ln1_b',
          'ln2_g', 'ln2_b', 'moe_w_router', 'moe_b_router', 'moe_w1', 'moe_w3', 'moe_w2', 'moe_ws1', 'moe_ws3',
          'moe_ws2', 'ple_w_proj', 'ple_w_gate')


_STACKED = ('moe_w1', 'moe_w3', 'moe_w2')


def run_trunk(x_prompt, x_sample, p_prompt, p_sample, states, weights, *, route_tm):
    n_seq, seq_len, d = x_prompt.shape
    n_s = x_sample.shape[0]
    t_p = n_seq * seq_len
    depth = weights[0].shape[0]
    x = jnp.concatenate([x_prompt.reshape(t_p, d), x_sample.reshape(n_s, d)], axis=0)
    xb = x.astype(BF16)
    pe = jnp.concatenate([p_prompt.reshape(depth, t_p, -1), p_sample.reshape(depth, n_s, -1)], axis=1).astype(BF16)
    new_p, new_s = [], []
    for i in range(depth):
        lw = {n: (w if n in _STACKED else w[i]) for n, w in zip(_NAMES, weights)}
        lw['layer'] = i
        st = (states[0], states[1][i], states[2][i], states[3], states[4][i])
        x, xb, st_p, st_s = trunk_layer(x, xb, pe[i], st, lw, n_seq=n_seq, seq_len=seq_len, route_tm=route_tm)
        new_p.append(st_p)
        new_s.append(st_s)
    gla_p, s5r_p, s5i_p, gdn_p, conv_p = (jnp.stack(f) for f in zip(*new_p))
    gla_s, s5r_s, s5i_s, gdn_s, conv_s = (jnp.stack(f) for f in zip(*new_s))
    yp = x[:t_p].reshape(n_seq, seq_len, d)
    ys = x[t_p:].reshape(n_s, 1, d)
    return (yp, ys, gla_p, gla_s, s5r_p, s5r_s, s5i_p, s5i_s, gdn_p, gdn_s, conv_p, conv_s)


def kernel(x_prompt, x_sample, p_prompt, p_sample, state_gla, state_s5_re, state_s5_im, state_gdn, state_gdn_conv,
           w_in, gla_w_gate, gla_b_gate, gla_norm, s5_lam_re, s5_lam_im, s5_log_dt, s5_b_re, s5_b_im, s5_c_re,
           s5_c_im, s5_d, s5_w_glu, s5_b_glu, gdn_conv_w, gdn_a_log, gdn_dt_bias, gdn_norm, w_branch_a,
           w_branch_s, w_branch_c, w_out, ln1_g, ln1_b, ln2_g, ln2_b, moe_w_router, moe_b_router, moe_w1, moe_w3,
           moe_w2, moe_ws1, moe_ws3, moe_ws2, ple_w_proj, ple_w_gate):
    weights = (w_in, gla_w_gate, gla_b_gate, gla_norm, s5_lam_re, s5_lam_im, s5_log_dt, s5_b_re, s5_b_im, s5_c_re,
               s5_c_im, s5_d, s5_w_glu, s5_b_glu, gdn_conv_w, gdn_a_log, gdn_dt_bias, gdn_norm, w_branch_a,
               w_branch_s, w_branch_c, w_out, ln1_g, ln1_b, ln2_g, ln2_b, moe_w_router, moe_b_router, moe_w1,
               moe_w3, moe_w2, moe_ws1, moe_ws3, moe_ws2, ple_w_proj, ple_w_gate)
    states = (state_gla, state_s5_re, state_s5_im, state_gdn, state_gdn_conv)
    return run_trunk(x_prompt, x_sample, p_prompt, p_sample, states, weights, route_tm=640)
```

```python
import functools
import math

import jax
import jax.numpy as jnp
from jax import lax
from jax.experimental import pallas as pl
from jax.experimental.pallas import tpu as pltpu

F32 = jnp.float32
BF16 = jnp.bfloat16
I32 = jnp.int32

D_MODEL = 2048
DEPTH = 4
GLA_HEADS, GLA_DK, GLA_DV = 4, 128, 256
GLA_KW, GLA_VW, GLA_RANK, GLA_TAU = 512, 1024, 16, 16.0
S5_WIDTH, S5_CH, S5_GROUPS, S5_STATE = 1024, 16, 64, 64
GDN_HEADS, GDN_DK, GDN_DV = 8, 128, 128
GDN_KW, GDN_VW, GDN_CONV = 1024, 1024, 4
N_BRANCH = 3
IN_SIZES = (GLA_KW, GLA_KW, GLA_VW, GLA_VW, GLA_RANK, S5_WIDTH, GDN_KW, GDN_KW, GDN_VW, GDN_VW, GDN_HEADS,
            GDN_HEADS, N_BRANCH * D_MODEL)
CHUNK = 64
N_EXPERTS, TOP_K, N_GROUPS, TOPK_GROUPS = 64, 8, 8, 4
D_EXPERT = 512
ROUTED_SCALE = 2.5
LN_EPS = 1e-5
NORM_EPS = 1e-6
DEEPNORM_ALPHA = (2 * DEPTH) ** 0.25

LANES = 128
VMEM_LIMIT_BYTES = 56 * 1024 * 1024
S5_CS = 16
S5_TILES = S5_WIDTH // LANES
S5_TSTATE = (LANES // S5_CH) * S5_STATE
MOE_ROWS = 256
MOE_TOK_TILE = 128


def _cparams(sem):
    return pltpu.CompilerParams(dimension_semantics=sem, vmem_limit_bytes=VMEM_LIMIT_BYTES)


def _pick_tile(n, candidates):
    for c in candidates:
        if n % c == 0:
            return c
    return n


def _sigmoid(x):
    return 1.0 / (1.0 + jnp.exp(-x))


def _silu(x):
    return x * _sigmoid(x)


def _gelu_tanh(x):
    return 0.5 * x * (1.0 + jnp.tanh(math.sqrt(2.0 / math.pi) * (x + 0.044715 * (x * x * x))))


def _log_sigmoid(x):
    return jnp.minimum(x, 0.0) - jnp.log1p(jnp.exp(-jnp.abs(x)))


def _bdot(a, b):
    return jnp.dot(a.astype(BF16), b.astype(BF16), preferred_element_type=F32)


def dense(x, w, *, bias=None, act=None, out_dtype=F32, tm=None, tn=None):
    m, k = x.shape
    n = w.shape[1]
    tm = tm or _pick_tile(m, (1664, 640, 512, 256, 128, 64, 32, 16, 8))
    tn = tn or _pick_tile(n, (512, 256, 128))

    def body(x_ref, w_ref, *rest):
        o_ref = rest[-1]
        y = _bdot(x_ref[...], w_ref[...])
        if bias is not None:
            y = y + rest[0][...]
        if act == 'log_decay':
            y = _log_sigmoid(y) / GLA_TAU
        o_ref[...] = y.astype(o_ref.dtype)

    in_specs = [pl.BlockSpec((tm, k), lambda i, j: (i, 0)), pl.BlockSpec((k, tn), lambda i, j: (0, j))]
    args = [x, w]
    if bias is not None:
        in_specs.append(pl.BlockSpec((1, tn), lambda i, j: (0, j)))
        args.append(bias.reshape(1, n))
    return pl.pallas_call(
        body, grid=(m // tm, n // tn), in_specs=in_specs,
        out_specs=pl.BlockSpec((tm, tn), lambda i, j: (i, j)),
        out_shape=jax.ShapeDtypeStruct((m, n), out_dtype),
        compiler_params=_cparams(("parallel", "parallel")), name="dense")(*args)


def swiglu_hidden(x, w1, w3):
    m, k = x.shape
    n = w1.shape[1]
    tm = _pick_tile(m, (1664, 640, 512, 256, 128, 64, 32, 16, 8))
    tn = _pick_tile(n, (512, 256, 128))

    def body(x_ref, w1_ref, w3_ref, o_ref):
        xb = x_ref[...].astype(BF16)
        a = jnp.dot(xb, w1_ref[...].astype(BF16), preferred_element_type=F32)
        b = jnp.dot(xb, w3_ref[...].astype(BF16), preferred_element_type=F32)
        o_ref[...] = (_silu(a) * b).astype(o_ref.dtype)

    return pl.pallas_call(
        body, grid=(m // tm, n // tn),
        in_specs=[pl.BlockSpec((tm, k), lambda i, j: (i, 0)), pl.BlockSpec((k, tn), lambda i, j: (0, j)),
                  pl.BlockSpec((k, tn), lambda i, j: (0, j))],
        out_specs=pl.BlockSpec((tm, tn), lambda i, j: (i, j)),
        out_shape=jax.ShapeDtypeStruct((m, n), BF16),
        compiler_params=_cparams(("parallel", "parallel")), name="swiglu_hidden")(x, w1, w3)


def glu_gate(y, w, b):
    m, n = y.shape
    tm = _pick_tile(m, (1664, 640, 512, 256, 128, 64, 32, 16, 8))
    tn = _pick_tile(n, (512, 256, 128))

    def body(y_ref, yt_ref, w_ref, b_ref, o_ref):
        g = _bdot(y_ref[...], w_ref[...]) + b_ref[...]
        o_ref[...] = (yt_ref[...] * _sigmoid(g)).astype(o_ref.dtype)

    return pl.pallas_call(
        body, grid=(m // tm, n // tn),
        in_specs=[pl.BlockSpec((tm, n), lambda i, j: (i, 0)), pl.BlockSpec((tm, tn), lambda i, j: (i, j)),
                  pl.BlockSpec((n, tn), lambda i, j: (0, j)), pl.BlockSpec((1, tn), lambda i, j: (0, j))],
        out_specs=pl.BlockSpec((tm, tn), lambda i, j: (i, j)),
        out_shape=jax.ShapeDtypeStruct((m, n), BF16),
        compiler_params=_cparams(("parallel", "parallel")), name="glu_gate")(y, y, w, b.reshape(1, n))


def merge_branches(xb, w_gates, br_a, br_s, br_c, w_a, w_s, w_c):
    m, k = xb.shape
    d = w_a.shape[1]
    kb = br_a.shape[1]
    tm = _pick_tile(m, (1664, 832, 640, 512, 256, 128, 64, 32, 16, 8))
    tn = _pick_tile(d, (256, 128))
    nj = d // tn

    def body(x_ref, g0_ref, g1_ref, g2_ref, a_ref, s_ref, c_ref, wa_ref, ws_ref, wc_ref, o_ref):
        x = x_ref[...]
        acc = _sigmoid(_bdot(x, g0_ref[...])) * _bdot(a_ref[...], wa_ref[...])
        acc = acc + _sigmoid(_bdot(x, g1_ref[...])) * _bdot(s_ref[...], ws_ref[...])
        acc = acc + _sigmoid(_bdot(x, g2_ref[...])) * _bdot(c_ref[...], wc_ref[...])
        o_ref[...] = acc.astype(o_ref.dtype)

    def gate_spec(b):
        return pl.BlockSpec((k, tn), lambda i, j: (0, b * nj + j))

    act_spec = pl.BlockSpec((tm, kb), lambda i, j: (i, 0))
    w_spec = pl.BlockSpec((kb, tn), lambda i, j: (0, j))
    return pl.pallas_call(
        body, grid=(m // tm, nj),
        in_specs=[pl.BlockSpec((tm, k), lambda i, j: (i, 0)), gate_spec(0), gate_spec(1), gate_spec(2),
                  act_spec, act_spec, act_spec, w_spec, w_spec, w_spec],
        out_specs=pl.BlockSpec((tm, tn), lambda i, j: (i, j)),
        out_shape=jax.ShapeDtypeStruct((m, d), BF16),
        compiler_params=_cparams(("parallel", "parallel")), name="merge_branches")(
            xb, w_gates, w_gates, w_gates, br_a, br_s, br_c, w_a, w_s, w_c)


def _layer_norm_rows(y, g, b):
    mu = jnp.mean(y, axis=-1, keepdims=True)
    yc = y - mu
    var = jnp.mean(yc * yc, axis=-1, keepdims=True)
    return yc * lax.rsqrt(var + LN_EPS) * g + b


_HI16 = 0xFFFF0000


def _pack_bf16_halves(y):
    c = y.shape[1] // 2
    bits = lax.bitcast_convert_type(y.astype(BF16).astype(F32), jnp.uint32)
    return (bits[:, c:] & jnp.uint32(_HI16)) | (bits[:, :c] >> 16)


def _unpack_bf16_halves(p):
    return (lax.bitcast_convert_type(p << 16, F32), lax.bitcast_convert_type(p & jnp.uint32(_HI16), F32))


def out_proj_ln(merged, w_out, x, g, b):
    m, k = merged.shape
    d = w_out.shape[1]
    tm = _pick_tile(m, (416, 256, 128, 64, 32, 16))

    def body(m_ref, w_ref, x_ref, g_ref, b_ref, o_ref, ob_ref, op_ref):
        y = DEEPNORM_ALPHA * x_ref[...] + _bdot(m_ref[...], w_ref[...])
        y = _layer_norm_rows(y, g_ref[...], b_ref[...])
        o_ref[...] = y
        ob_ref[...] = y.astype(BF16)
        op_ref[...] = _pack_bf16_halves(y)

    row = pl.BlockSpec((1, d), lambda i: (0, 0))
    return pl.pallas_call(
        body, grid=(m // tm,),
        in_specs=[pl.BlockSpec((tm, k), lambda i: (i, 0)), pl.BlockSpec((k, d), lambda i: (0, 0)),
                  pl.BlockSpec((tm, d), lambda i: (i, 0)), row, row],
        out_specs=[pl.BlockSpec((tm, d), lambda i: (i, 0)), pl.BlockSpec((tm, d), lambda i: (i, 0)),
                   pl.BlockSpec((tm, d // 2), lambda i: (i, 0))],
        out_shape=[jax.ShapeDtypeStruct((m, d), F32), jax.ShapeDtypeStruct((m, d), BF16),
                   jax.ShapeDtypeStruct((m, d // 2), jnp.uint32)],
        compiler_params=_cparams(("parallel",)), name="out_proj_ln")(
            merged, w_out, x, g.reshape(1, d), b.reshape(1, d))


def ple_mix(x, xb, w_gate, pe, w_proj):
    m, d = x.shape
    kp = pe.shape[1]
    tm = _pick_tile(m, (1664, 640, 512, 256, 128, 64, 32, 16))
    tn = _pick_tile(d, (512, 256, 128))

    def body(xb_ref, wg_ref, pe_ref, wp_ref, x_ref, o_ref, ob_ref):
        y = x_ref[...] + _sigmoid(_bdot(xb_ref[...], wg_ref[...])) * _bdot(pe_ref[...], wp_ref[...])
        o_ref[...] = y
        ob_ref[...] = y.astype(BF16)

    return pl.pallas_call(
        body, grid=(m // tm, d // tn),
        in_specs=[pl.BlockSpec((tm, d), lambda i, j: (i, 0)), pl.BlockSpec((d, tn), lambda i, j: (0, j)),
                  pl.BlockSpec((tm, kp), lambda i, j: (i, 0)), pl.BlockSpec((kp, tn), lambda i, j: (0, j)),
                  pl.BlockSpec((tm, tn), lambda i, j: (i, j))],
        out_specs=[pl.BlockSpec((tm, tn), lambda i, j: (i, j)), pl.BlockSpec((tm, tn), lambda i, j: (i, j))],
        out_shape=[jax.ShapeDtypeStruct((m, d), F32), jax.ShapeDtypeStruct((m, d), BF16)],
        compiler_params=_cparams(("parallel", "parallel")), name="ple_mix")(xb, w_gate, pe, w_proj, x)


def s5_tables(lam_re, lam_im, log_dt, b_re, b_im, c_re, c_im):
    hp = lax.Precision.HIGHEST
    cs, nt, gl = S5_CS, S5_TILES, LANES // S5_CH
    dt = jnp.exp(log_dt)[:, None]
    mag = jnp.exp(lam_re * dt)
    ab_re, ab_im = mag * jnp.cos(lam_im * dt), mag * jnp.sin(lam_im * dt)
    den = lam_re * lam_re + lam_im * lam_im
    nr = ab_re - 1.0
    co_re = (nr * lam_re + ab_im * lam_im) / den
    co_im = (ab_im * lam_re - nr * lam_im) / den
    bb_re = co_re[..., None] * b_re - co_im[..., None] * b_im
    bb_im = co_re[..., None] * b_im + co_im[..., None] * b_re
    pr, pi = [jnp.ones_like(ab_re)], [jnp.zeros_like(ab_im)]
    for _ in range(cs):
        pr.append(pr[-1] * ab_re - pi[-1] * ab_im)
        pi.append(pr[-2] * ab_im + pi[-1] * ab_re)
    ap_re, ap_im = jnp.stack(pr), jnp.stack(pi)
    abr = ap_re[:, :, :, None] * bb_re - ap_im[:, :, :, None] * bb_im
    abi = ap_re[:, :, :, None] * bb_im + ap_im[:, :, :, None] * bb_re
    kern = (jnp.einsum('gcp,egpd->egcd', c_re, abr[:cs], precision=hp)
            - jnp.einsum('gcp,egpd->egcd', c_im, abi[:cs], precision=hp))
    same_group = jnp.eye(gl, dtype=bool)

    def block_diag(a, g_axis, h_axis):
        shape = [1] * (a.ndim + 1)
        shape[g_axis if g_axis < h_axis else g_axis + 1] = gl
        shape[h_axis] = gl
        return jnp.where(same_group.reshape(shape), jnp.expand_dims(a, h_axis), 0.0).astype(BF16)

    k5 = kern.reshape(cs, nt, gl, S5_CH, S5_CH).transpose(1, 0, 2, 4, 3)
    kcat = block_diag(k5, 2, 4).reshape(nt, cs, LANES, LANES).transpose(0, 2, 1, 3).reshape(nt, LANES, cs * LANES)
    toep = jnp.concatenate([jnp.pad(kcat[:, :, :(cs - s) * LANES], ((0, 0), (0, 0), (s * LANES, 0)))
                            for s in range(cs)], axis=1)
    def state_lanes(re, im, n_rows):
        rep = jnp.tile(jnp.eye(S5_STATE, dtype=F32), (1, gl))
        full = jnp.concatenate([jnp.einsum('...p,pq->...q', re, rep, precision=hp),
                                jnp.einsum('...p,pq->...q', im, rep, precision=hp)], axis=-1)
        full = full.reshape(nt, n_rows, 2 * S5_TSTATE)
        row_g = (jnp.arange(n_rows) // S5_CH) % gl
        col_h = (jnp.arange(2 * S5_TSTATE) // S5_STATE) % gl
        return jnp.where(row_g[:, None] == col_h[None, :], full, 0.0).astype(BF16)

    er = abr[:cs][::-1].reshape(cs, nt, gl, S5_STATE, S5_CH).transpose(1, 0, 2, 4, 3)
    ei = abi[:cs][::-1].reshape(cs, nt, gl, S5_STATE, S5_CH).transpose(1, 0, 2, 4, 3)
    bend = state_lanes(er, ei, cs * LANES)
    car = c_re[None] * ap_re[:, :, None, :] - c_im[None] * ap_im[:, :, None, :]
    cai = -(c_re[None] * ap_im[:, :, None, :] + c_im[None] * ap_re[:, :, None, :])
    car = car.reshape(cs + 1, nt, gl, S5_CH, S5_STATE).transpose(1, 0, 2, 3, 4)
    cai = cai.reshape(cs + 1, nt, gl, S5_CH, S5_STATE).transpose(1, 0, 2, 3, 4)
    ccar_t = state_lanes(car, cai, (cs + 1) * LANES)

    def state_row(re, im):
        return jnp.concatenate([re.reshape(nt, 1, S5_TSTATE), im.reshape(nt, 1, S5_TSTATE)], axis=-1)

    return dict(toep=toep, bend=bend, bbar=bend[:, (cs - 1) * LANES:], c0_t=ccar_t[:, :LANES],
                ccar_t=ccar_t[:, LANES:], a1=state_row(ap_re[1], ap_im[1]), acs=state_row(ap_re[cs], ap_im[cs]))


def _chunk_rows(u_ref, nc):
    return jnp.concatenate([u_ref[pl.ds(s, nc, stride=S5_CS), :].astype(BF16) for s in range(S5_CS)], axis=1)


def s5_chunk_states(u_src, bend, *, t_p):
    nt, kc, n = bend.shape
    nc = t_p // S5_CS
    tn = 512

    def body(u_ref, b_ref, o_ref):
        o_ref[0] = jnp.dot(_chunk_rows(u_ref, nc), b_ref[0], preferred_element_type=F32)

    return pl.pallas_call(
        body, grid=(nt, n // tn),
        in_specs=[pl.BlockSpec((t_p, LANES), lambda j, n_: (0, j)),
                  pl.BlockSpec((1, kc, tn), lambda j, n_: (j, 0, n_))],
        out_specs=pl.BlockSpec((1, nc, tn), lambda j, n_: (j, 0, n_)),
        out_shape=jax.ShapeDtypeStruct((nt, nc, n), F32),
        compiler_params=_cparams(("parallel", "parallel")), name="s5_chunk_states")(u_src, bend)


def s5_carry_scan(xe, acs, n_seq):
    nt, nc, n = xe.shape
    per = nc // n_seq
    half = n // 2

    def body(x_ref, a_ref, hp_ref, hf_ref):
        ar = a_ref[0, :, :half]
        ai = a_ref[0, :, half:]

        def step(k, carry):
            hr, hi = carry
            hp_ref[0, pl.ds(k, 1), :] = jnp.concatenate([hr, hi], axis=1)
            x = x_ref[0, pl.ds(k, 1), :]
            return (ar * hr - ai * hi + x[:, :half], ar * hi + ai * hr + x[:, half:])

        zero = jnp.zeros((1, half), F32)
        hr, hi = lax.fori_loop(0, per, step, (zero, zero))
        hf_ref[0, 0] = jnp.concatenate([hr, hi], axis=1)

    return pl.pallas_call(
        body, grid=(nt, n_seq),
        in_specs=[pl.BlockSpec((1, per, n), lambda j, b: (j, b, 0)), pl.BlockSpec((1, 1, n), lambda j, b: (j, 0, 0))],
        out_specs=[pl.BlockSpec((1, per, n), lambda j, b: (j, b, 0)),
                   pl.BlockSpec((1, 1, 1, n), lambda j, b: (j, b, 0, 0))],
        out_shape=[jax.ShapeDtypeStruct((nt, nc, n), F32), jax.ShapeDtypeStruct((nt, n_seq, 1, n), F32)],
        compiler_params=_cparams(("parallel", "parallel")), name="s5_carry_scan")(xe, acs)


def s5_outputs(u_src, toep, hprev, ccar, d_skip, *, t_p):
    nt, kc, _ = toep.shape
    ns = hprev.shape[2]
    nc = t_p // S5_CS
    tn = 512
    per = tn // LANES

    def body(u_ref, t_ref, h_ref, c_ref, d_ref, o_ref):
        uc = _chunk_rows(u_ref, nc)
        hb = h_ref[0].astype(BF16)
        for n_ in range(kc // tn):
            cols = slice(n_ * tn, (n_ + 1) * tn)
            y = jnp.dot(uc, t_ref[0, :, cols], preferred_element_type=F32) + _dot_nt(hb, c_ref[0, cols, :])
            for i in range(per):
                rows = pl.ds(n_ * per + i, nc, stride=S5_CS)
                o_ref[rows, :] = _gelu_tanh(y[:, i * LANES:(i + 1) * LANES] + d_ref[...] * u_ref[rows, :])

    return pl.pallas_call(
        body, grid=(nt,),
        in_specs=[pl.BlockSpec((t_p, LANES), lambda j: (0, j)),
                  pl.BlockSpec((1, kc, kc), lambda j: (j, 0, 0)), pl.BlockSpec((1, nc, ns), lambda j: (j, 0, 0)),
                  pl.BlockSpec((1, kc, ns), lambda j: (j, 0, 0)), pl.BlockSpec((1, LANES), lambda j: (0, j))],
        out_specs=pl.BlockSpec((t_p, LANES), lambda j: (0, j)),
        out_shape=jax.ShapeDtypeStruct((t_p, nt * LANES), F32),
        compiler_params=_cparams(("parallel",)), name="s5_outputs")(
            u_src, toep, hprev, ccar, d_skip.reshape(1, nt * LANES))


def s5_decode(u_src, h_re, h_im, bbar, a1, c0, d_skip, *, row0):
    s = h_re.shape[0]
    nt = bbar.shape[0]
    w = nt * LANES
    ts = S5_TSTATE
    rb = row0 // s

    def body(u_ref, hr_ref, hi_ref, b_ref, a_ref, c_ref, d_ref, y_ref, nr_ref, ni_ref):
        uu = u_ref[...]
        x = _bdot(uu, b_ref[0])
        ar, ai = a_ref[0, :, :ts], a_ref[0, :, ts:]
        hr, hi = hr_ref[...], hi_ref[...]
        nr = ar * hr - ai * hi + x[:, :ts]
        ni = ar * hi + ai * hr + x[:, ts:]
        nr_ref[...] = nr
        ni_ref[...] = ni
        y = _dot_nt(jnp.concatenate([nr, ni], axis=1), c_ref[0]) + d_ref[...] * uu
        y_ref[...] = _gelu_tanh(y)

    col = pl.BlockSpec((s, LANES), lambda j: (0, j))
    st = pl.BlockSpec((s, ts), lambda j: (0, j))
    tab = pl.BlockSpec((1, LANES, 2 * ts), lambda j: (j, 0, 0))
    return pl.pallas_call(
        body, grid=(nt,),
        in_specs=[pl.BlockSpec((s, LANES), lambda j: (rb, j)), st, st, tab,
                  pl.BlockSpec((1, 1, 2 * ts), lambda j: (j, 0, 0)), tab,
                  pl.BlockSpec((1, LANES), lambda j: (0, j))],
        out_specs=[col, st, st],
        out_shape=[jax.ShapeDtypeStruct((s, w), F32), jax.ShapeDtypeStruct(h_re.shape, F32),
                   jax.ShapeDtypeStruct(h_im.shape, F32)],
        compiler_params=_cparams(("parallel",)), name="s5_decode")(u_src, h_re, h_im, bbar, a1, c0, d_skip.reshape(1, w))


def s5_branch(u_src, n_seq, seq_len, h_re, h_im, tabs, d_skip):
    t_p = n_seq * seq_len
    nt = S5_TILES
    xe = s5_chunk_states(u_src, tabs['bend'], t_p=t_p)
    hprev, hfin = s5_carry_scan(xe, tabs['acs'], n_seq)
    y_p = s5_outputs(u_src, tabs['toep'], hprev, tabs['ccar_t'], d_skip, t_p=t_p)
    hfin = hfin.reshape(nt, n_seq, 2, S5_TSTATE).transpose(2, 1, 0, 3).reshape(2, n_seq, S5_GROUPS, S5_STATE)
    s_rows = u_src.shape[0] - t_p
    y_s, nr, ni = s5_decode(u_src, h_re.reshape(s_rows, -1), h_im.reshape(s_rows, -1), tabs['bbar'],
                            tabs['a1'], tabs['c0_t'], d_skip, row0=t_p)
    return (jnp.concatenate([y_p, y_s], axis=0), hfin[0], hfin[1], nr.reshape(h_re.shape), ni.reshape(h_im.shape))


def moe_route(xb, w_router_t, b_router, *, tm):
    t, d = xb.shape
    ne, ng, gs = N_EXPERTS, N_GROUPS, N_EXPERTS // N_GROUPS
    neg = -jnp.inf

    def body(x_ref, w_ref, b_ref, u_ref, idx_ref, wt_ref, rank_ref, cnt_ref, carry_ref):
        @pl.when(pl.program_id(0) == 0)
        def _():
            carry_ref[...] = jnp.zeros_like(carry_ref)

        logits = lax.dot_general(w_ref[...], x_ref[...], (((1,), (1,)), ((), ())), preferred_element_type=F32)
        scores = _sigmoid(logits).reshape(ng, gs, tm)
        choice = scores + b_ref[...].reshape(ng, gs, 1)
        e_in = lax.broadcasted_iota(I32, (ng, gs, tm), 1).astype(F32)
        g_id = lax.broadcasted_iota(I32, (ng, 1, tm), 0).astype(F32)
        e_id = g_id * gs + e_in
        m1 = jnp.max(choice, axis=1, keepdims=True)
        i1 = jnp.min(jnp.where(choice == m1, e_in, float(gs)), axis=1, keepdims=True)
        m2 = jnp.max(jnp.where(e_in == i1, neg, choice), axis=1, keepdims=True)
        gscore = m1 + m2
        keep = jnp.zeros((ng, 1, tm), F32)
        for _ in range(TOPK_GROUPS):
            gm = jnp.max(gscore, axis=0, keepdims=True)
            gi = jnp.min(jnp.where(gscore == gm, g_id, float(ng)), axis=0, keepdims=True)
            hit = g_id == gi
            keep = jnp.where(hit, 1.0, keep)
            gscore = jnp.where(hit, neg, gscore)
        cand = jnp.where(keep > 0.0, choice, neg)
        member = jnp.zeros((ng, gs, tm), F32)
        picks, wts = [], []
        for _ in range(TOP_K):
            cm = jnp.max(jnp.max(cand, axis=1, keepdims=True), axis=0, keepdims=True)
            ei = jnp.min(jnp.min(jnp.where(cand == cm, e_id, float(ne)), axis=1, keepdims=True), axis=0, keepdims=True)
            sel = e_id == ei
            wts.append(jnp.sum(jnp.sum(jnp.where(sel, scores, 0.0), axis=1, keepdims=True), axis=0, keepdims=True))
            picks.append(ei)
            member = jnp.where(sel, 1.0, member)
            cand = jnp.where(sel, neg, cand)
        wsum = wts[0]
        for w in wts[1:]:
            wsum = wsum + w
        member2 = member.reshape(ne, tm)
        prefix = jnp.dot(member2.astype(BF16), u_ref[...], preferred_element_type=F32) + carry_ref[:, 0:1]
        prefix = prefix.reshape(ng, gs, tm)
        for j in range(TOP_K):
            sel = e_id == picks[j]
            rk = jnp.sum(jnp.sum(jnp.where(sel, prefix, 0.0), axis=1, keepdims=True), axis=0, keepdims=True)
            idx_ref[j:j + 1, :] = picks[j].reshape(1, tm).astype(I32)
            rank_ref[j:j + 1, :] = rk.reshape(1, tm).astype(I32)
            wt_ref[j:j + 1, :] = (wts[j] / wsum * ROUTED_SCALE).reshape(1, tm)
        carry_ref[...] = carry_ref[...] + jnp.sum(member2, axis=1, keepdims=True)
        cnt_ref[...] = carry_ref[...]

    upper = jnp.triu(jnp.ones((tm, tm), F32), 1).astype(BF16)
    tok = pl.BlockSpec((TOP_K, tm), lambda i: (0, i))
    idx, wt, rank, cnt = pl.pallas_call(
        body, grid=(t // tm,),
        in_specs=[pl.BlockSpec((tm, d), lambda i: (i, 0)), pl.BlockSpec((ne, d), lambda i: (0, 0)),
                  pl.BlockSpec((ne, 1), lambda i: (0, 0)), pl.BlockSpec((tm, tm), lambda i: (0, 0))],
        out_specs=[tok, tok, tok, pl.BlockSpec((ne, LANES), lambda i: (0, 0))],
        out_shape=[jax.ShapeDtypeStruct((TOP_K, t), I32), jax.ShapeDtypeStruct((TOP_K, t), F32),
                   jax.ShapeDtypeStruct((TOP_K, t), I32), jax.ShapeDtypeStruct((ne, LANES), F32)],
        scratch_shapes=[pltpu.VMEM((ne, LANES), F32)],
        compiler_params=_cparams(("arbitrary",)), name="moe_route")(xb, w_router_t, b_router.reshape(ne, 1), upper)
    return idx, wt, rank, cnt[:, 0]


def moe_experts(x, row_tok, block_e, n_used, w1, w3, w2, *, rows, layer):
    t, dh = x.shape
    d = 2 * dh
    nb = row_tok.shape[0]
    f = w1.shape[3]

    be = block_e
    first = jnp.concatenate([jnp.ones((1,), I32), (be[1:] != be[:-1]).astype(I32)])
    seg = jnp.cumsum(first) - 1
    seg_end = jnp.sum((seg[None, :] <= seg[:, None]).astype(I32), axis=1)
    has_next = (seg_end < nb).astype(I32)
    next_e = be[jnp.minimum(seg_end, nb - 1)]
    seg_info = jnp.stack([first, seg % 2, has_next, next_e]).astype(I32)

    def body(be_ref, nu_ref, sg_ref, x_hbm, tok_ref, tokn_ref, w1_hbm, w3_hbm, w2_hbm, o_ref, buf, sem,
             w1b, w3b, w2b, wf1, wf3, wf2, semw):
        i = pl.program_id(0)
        slot = lax.rem(i, 2)
        nxt = 1 - slot
        groups = 4
        per = rows // groups

        def weight_copies(ex, ws):
            return (pltpu.make_async_copy(w1_hbm.at[layer, ex], wf1.at[ws], semw.at[ws]),
                    pltpu.make_async_copy(w3_hbm.at[layer, ex], wf3.at[ws], semw.at[ws]),
                    pltpu.make_async_copy(w2_hbm.at[layer, ex], wf2.at[ws], semw.at[ws]))

        def row_copy(tref, r, sl):
            tok = tref[0, 0, r]
            return pltpu.make_async_copy(x_hbm.at[pl.ds(tok, 1), :], buf.at[sl, pl.ds(r, 1), :], sem.at[sl])

        def gather_loop(tref, sl):
            def issue(r, c):
                row_copy(tref, r, sl).start(priority=1)
                return c
            lax.fori_loop(0, rows, issue, 0, unroll=8)

        def gather_group(tref, sl, g):
            for r in range(g * per, (g + 1) * per):
                row_copy(tref, r, sl).start(priority=1)

        def block_wait(sl):
            pltpu.make_async_copy(x_hbm.at[pl.ds(0, rows), :], buf.at[sl], sem.at[sl]).wait()

        @pl.when(i == 0)
        def _():
            gather_loop(tok_ref, 0)
            for c in weight_copies(be_ref[0], 0):
                c.start()

        ws = sg_ref[1, i]

        @pl.when(sg_ref[0, i] == 1)
        def _():
            for c in weight_copies(be_ref[i], ws):
                c.wait()
            w1b[...] = wf1[ws].astype(BF16)
            w3b[...] = wf3[ws].astype(BF16)
            w2b[...] = wf2[ws].astype(BF16)

            @pl.when(sg_ref[2, i] == 1)
            def _():
                for c in weight_copies(sg_ref[3, i], 1 - ws):
                    c.start()

        block_wait(slot)

        @pl.when(i < nu_ref[0])
        def _():
            lo, hi = _unpack_bf16_halves(buf[slot])
            lo, hi = lo.astype(BF16), hi.astype(BF16)
            gather_group(tokn_ref, nxt, 0)
            h1 = (jnp.dot(lo, w1b[:dh], preferred_element_type=F32)
                  + jnp.dot(hi, w1b[dh:], preferred_element_type=F32))
            gather_group(tokn_ref, nxt, 1)
            h3 = (jnp.dot(lo, w3b[:dh], preferred_element_type=F32)
                  + jnp.dot(hi, w3b[dh:], preferred_element_type=F32))
            gather_group(tokn_ref, nxt, 2)
            y = jnp.dot((_silu(h1) * h3).astype(BF16), w2b[...], preferred_element_type=F32)
            gather_group(tokn_ref, nxt, 3)
            o_ref[...] = _pack_bf16_halves(y)

        @pl.when(i >= nu_ref[0])
        def _():
            gather_loop(tokn_ref, nxt)
            o_ref[...] = jnp.zeros_like(o_ref)

        @pl.when(i == nb - 1)
        def _():
            block_wait(nxt)

    hbm = pl.BlockSpec(memory_space=pl.ANY)
    grid_spec = pltpu.PrefetchScalarGridSpec(
        num_scalar_prefetch=3, grid=(nb,),
        in_specs=[hbm,
                  pl.BlockSpec((1, 1, rows), lambda i, be, nu, sg: (i, 0, 0), memory_space=pltpu.SMEM),
                  pl.BlockSpec((1, 1, rows), lambda i, be, nu, sg: (jnp.minimum(i + 1, nb - 1), 0, 0),
                               memory_space=pltpu.SMEM),
                  hbm, hbm, hbm],
        out_specs=pl.BlockSpec((rows, dh), lambda i, be, nu, sg: (i, 0)),
        scratch_shapes=[pltpu.VMEM((2, rows, dh), jnp.uint32), pltpu.SemaphoreType.DMA((2,)),
                        pltpu.VMEM((d, f), BF16), pltpu.VMEM((d, f), BF16), pltpu.VMEM((f, d), BF16),
                        pltpu.VMEM((2, d, f), F32), pltpu.VMEM((2, d, f), F32), pltpu.VMEM((2, f, d), F32),
                        pltpu.SemaphoreType.DMA((2,))])
    return pl.pallas_call(
        body, grid_spec=grid_spec, out_shape=jax.ShapeDtypeStruct((nb * rows, dh), jnp.uint32),
        compiler_params=_cparams(("arbitrary",)), name="moe_experts")(
            block_e, n_used, seg_info, x, row_tok, row_tok, w1, w3, w2)


def moe_combine_ln(ys, dest, wts, x, shared, g, b, *, tm):
    t, d = x.shape
    nt = t // tm

    def body(ys_hbm, d_ref, dn_ref, w_ref, x_ref, s_ref, g_ref, b_ref, o_ref, ob_ref, buf, sem):
        i = pl.program_id(0)
        slot = lax.rem(i, 2)

        def gather(dref, sl):
            def issue(r, c):
                row = dref[0, 0, r]
                pltpu.make_async_copy(ys_hbm.at[pl.ds(row, 1), :], buf.at[sl, pl.ds(r, 1), :],
                                      sem.at[sl]).start(priority=1)
                return c
            lax.fori_loop(0, TOP_K * tm, issue, 0, unroll=8)

        @pl.when(i == 0)
        def _():
            gather(d_ref, 0)

        @pl.when(i + 1 < nt)
        def _():
            gather(dn_ref, 1 - slot)

        pltpu.make_async_copy(ys_hbm.at[pl.ds(0, TOP_K * tm), :], buf.at[slot], sem.at[slot]).wait()
        w = w_ref[...]
        acc_lo = jnp.zeros((tm, d // 2), F32)
        acc_hi = jnp.zeros((tm, d // 2), F32)
        for j in range(TOP_K):
            lo, hi = _unpack_bf16_halves(buf[slot, j * tm:(j + 1) * tm, :])
            acc_lo = acc_lo + w[:, j:j + 1] * lo
            acc_hi = acc_hi + w[:, j:j + 1] * hi
        acc = DEEPNORM_ALPHA * x_ref[...] + s_ref[...] + jnp.concatenate([acc_lo, acc_hi], axis=1)
        y = _layer_norm_rows(acc, g_ref[...], b_ref[...])
        o_ref[...] = y
        ob_ref[...] = y.astype(BF16)

    row = pl.BlockSpec((1, d), lambda i: (0, 0))
    tile = pl.BlockSpec((tm, d), lambda i: (i, 0))
    return pl.pallas_call(
        body, grid=(nt,),
        in_specs=[pl.BlockSpec(memory_space=pl.ANY),
                  pl.BlockSpec((1, 1, TOP_K * tm), lambda i: (i, 0, 0), memory_space=pltpu.SMEM),
                  pl.BlockSpec((1, 1, TOP_K * tm), lambda i: (jnp.minimum(i + 1, nt - 1), 0, 0),
                               memory_space=pltpu.SMEM),
                  pl.BlockSpec((tm, TOP_K), lambda i: (i, 0)), tile, tile, row, row],
        out_specs=[tile, tile],
        out_shape=[jax.ShapeDtypeStruct((t, d), F32), jax.ShapeDtypeStruct((t, d), BF16)],
        scratch_shapes=[pltpu.VMEM((2, TOP_K * tm, d // 2), jnp.uint32), pltpu.SemaphoreType.DMA((2,))],
        compiler_params=_cparams(("arbitrary",)), name="moe_combine_ln")(
            ys, dest, dest, wts, x, shared, g.reshape(1, d), b.reshape(1, d))


def moe_layer(x, xb, xp, lw, *, route_tm, rows=MOE_ROWS, tok_tile=MOE_TOK_TILE):
    t, d = x.shape
    idx, wt, rank, counts = moe_route(xb, lw['moe_w_router'].T.astype(BF16), lw['moe_b_router'], tm=route_tm)
    counts = counts.astype(I32)
    padded = (counts + rows - 1) // rows * rows
    pad_end = jnp.cumsum(padded)
    pad_start = pad_end - padded
    onehot = idx[:, :, None] == jnp.arange(N_EXPERTS, dtype=I32)[None, None, :]
    dest = jnp.sum(jnp.where(onehot, pad_start[None, None, :], 0), axis=-1) + rank
    n_rows = -(-(t * TOP_K + N_EXPERTS * (rows - 1)) // rows) * rows
    nb = n_rows // rows
    tok_id = jnp.broadcast_to(jnp.arange(t, dtype=I32)[None, :], (TOP_K, t))
    row_tok = jnp.zeros((n_rows,), I32).at[dest.reshape(-1)].set(tok_id.reshape(-1))
    blk_start = jnp.arange(nb, dtype=I32) * rows
    block_e = jnp.minimum(jnp.sum((pad_end[None, :] <= blk_start[:, None]).astype(I32), axis=1), N_EXPERTS - 1)
    n_used = (pad_end[-1] // rows).astype(I32).reshape(1)
    ys = moe_experts(xp, row_tok.reshape(nb, 1, rows), block_e, n_used, lw['moe_w1'], lw['moe_w3'], lw['moe_w2'],
                     rows=rows, layer=lw['layer'])
    hs = swiglu_hidden(xb, lw['moe_ws1'], lw['moe_ws3'])
    shared = dense(hs, lw['moe_ws2'])
    dest_t = dest.reshape(TOP_K, t // tok_tile, tok_tile).transpose(1, 0, 2).reshape(t // tok_tile, 1, TOP_K * tok_tile)
    return moe_combine_ln(ys, dest_t, wt.T, x, shared, lw['ln2_g'], lw['ln2_b'], tm=tok_tile)


def _level_tables(c):
    import numpy as np
    idx = np.arange(c)
    t, r = idx[:, None], idx[None, :]
    wl, pm = [], []
    b = 1
    while b < c:
        blk, odd = t // b, (t // b) % 2 == 1
        w = np.where(odd, (r >= blk * b) & (r <= t), (r > t) & (r <= blk * b + b - 1))
        wl.append(w.astype(np.float32))
        pm.append((odd & (r // b == blk - 1)).astype(np.float32))
        b *= 2
    incl = (r <= t).astype(np.float32)
    after = (r > t).astype(np.float32)
    return wl, pm, incl, after


def _split3(x):
    hi = x.astype(BF16)
    r1 = x - hi.astype(F32)
    mid = r1.astype(BF16)
    lo = (r1 - mid.astype(F32)).astype(BF16)
    return hi, mid, lo


def _table_dot(tab, x):
    hi, mid, lo = _split3(x)
    return (jnp.dot(tab, hi, preferred_element_type=F32) + jnp.dot(tab, mid, preferred_element_type=F32)
            + jnp.dot(tab, lo, preferred_element_type=F32))


def _dot_hi(a, b):
    ah = a.astype(BF16)
    al = (a - ah.astype(F32)).astype(BF16)
    bh = b.astype(BF16)
    bl = (b - bh.astype(F32)).astype(BF16)
    return (jnp.dot(ah, bh, preferred_element_type=F32) + jnp.dot(ah, bl, preferred_element_type=F32)
            + jnp.dot(al, bh, preferred_element_type=F32))


def _dot_nt(a, b):
    return lax.dot_general(a.astype(BF16), b.astype(BF16), (((1,), (1,)), ((), ())), preferred_element_type=F32)


def _dot_tn(a, b):
    return lax.dot_general(a.astype(BF16), b.astype(BF16), (((0,), (0,)), ((), ())), preferred_element_type=F32)


def _rms_rows(o, w):
    return o * lax.rsqrt(jnp.mean(o * o, axis=-1, keepdims=True) + NORM_EPS) * w


def gla_prompt(z1, log_a, norm_w, *, n_seq, seq_len):
    c, h_, dk, dv = CHUNK, GLA_HEADS, GLA_DK, GLA_DV
    nck = seq_len // c
    wl, pm, incl, after = _level_tables(c)
    nl = len(wl)
    wcat = jnp.asarray(jnp.concatenate([jnp.asarray(w) for w in wl] + [jnp.asarray(incl), jnp.asarray(after)], axis=0),
                       BF16)
    pmask = jnp.stack([jnp.eye(c, dtype=F32)] + [jnp.asarray(p) for p in pm])
    scale = dk ** -0.5

    def body(q_ref, k_ref, v_ref, r_ref, g_ref, w_ref, p_ref, n_ref, o_ref, st_ref, s_scr):
        ci = pl.program_id(1)

        @pl.when(ci == 0)
        def _():
            s_scr[...] = jnp.zeros_like(s_scr)

        x = _table_dot(w_ref[...], g_ref[...])
        ex = jnp.exp(x)
        for h in range(h_):
            ks = slice(h * dk, (h + 1) * dk)
            vs = slice(h * dv, (h + 1) * dv)
            q = q_ref[:, ks] * scale
            k = k_ref[:, ks]
            v = v_ref[:, vs]
            scores = p_ref[0] * _dot_nt(q, k)
            for l in range(nl):
                f = ex[l * c:(l + 1) * c, ks]
                scores = scores + p_ref[l + 1] * _dot_nt(q * f, k * f)
            st = s_scr[h]
            o = _dot_nt(q * ex[nl * c:(nl + 1) * c, ks], st) + _bdot(scores, v)
            tot = x[(nl + 1) * c - 1:(nl + 1) * c, ks]
            s_scr[h] = jnp.exp(tot) * st + _dot_tn(v, k * ex[(nl + 1) * c:(nl + 2) * c, ks])
            o_ref[:, vs] = (_rms_rows(o, n_ref[...]) * _silu(r_ref[:, vs])).astype(o_ref.dtype)

        @pl.when(ci == nck - 1)
        def _():
            st_ref[0] = s_scr[...]

    def rows(width, col):
        return pl.BlockSpec((c, width), lambda b, i: (b * nck + i, col))

    return pl.pallas_call(
        body, grid=(n_seq, nck),
        in_specs=[rows(h_ * dk, 0), rows(h_ * dk, 1), rows(h_ * dv, 1), rows(h_ * dv, 2), rows(h_ * dk, 0),
                  pl.BlockSpec(wcat.shape, lambda b, i: (0, 0)), pl.BlockSpec(pmask.shape, lambda b, i: (0, 0, 0)),
                  pl.BlockSpec((1, dv), lambda b, i: (0, 0))],
        out_specs=[rows(h_ * dv, 0), pl.BlockSpec((1, h_, dv, dk), lambda b, i: (b, 0, 0, 0))],
        out_shape=[jax.ShapeDtypeStruct((n_seq * seq_len, h_ * dv), BF16),
                   jax.ShapeDtypeStruct((n_seq, h_, dv, dk), F32)],
        scratch_shapes=[pltpu.VMEM((h_, dv, dk), F32)],
        compiler_params=_cparams(("parallel", "arbitrary")), name="gla_prompt")(
            z1, z1, z1, z1, log_a, wcat, pmask, norm_w.reshape(1, dv))


def _columns(rows_list, width):
    used = sum(r.shape[0] for r in rows_list)
    stack = jnp.concatenate(list(rows_list) + [jnp.zeros((LANES - used, width), F32)], axis=0)
    return stack.T


def gla_decode(z1, log_a, state, norm_w, *, row0, layer):
    sb = 8
    n_s = state.shape[1]
    h_, dk, dv = GLA_HEADS, GLA_DK, GLA_DV
    scale = dk ** -0.5
    r0 = row0 // sb

    def body(q_ref, k_ref, v_ref, r_ref, g_ref, s_ref, n_ref, o_ref, ns_ref):
        for h in range(h_):
            ks = slice(h * dk, (h + 1) * dk)
            vs = slice(h * dv, (h + 1) * dv)
            cols = _columns([jnp.exp(g_ref[:, ks]), k_ref[:, ks], q_ref[:, ks] * scale], dk)
            v = v_ref[:, vs]
            outs = []
            for s in range(sb):
                s_new = cols[:, s:s + 1] * s_ref[0, s, h] + cols[:, sb + s:sb + s + 1] * v[s:s + 1, :]
                ns_ref[s, h] = s_new
                outs.append(jnp.sum(cols[:, 2 * sb + s:2 * sb + s + 1] * s_new, axis=0, keepdims=True))
            o = jnp.concatenate(outs, axis=0)
            o_ref[:, vs] = _rms_rows(o, n_ref[...]) * _silu(r_ref[:, vs])

    def rows(width, col):
        return pl.BlockSpec((sb, width), lambda i: (r0 + i, col))

    st_in = pl.BlockSpec((1, sb, h_, dk, dv), lambda i: (layer, i, 0, 0, 0))
    st = pl.BlockSpec((sb, h_, dk, dv), lambda i: (i, 0, 0, 0))
    return pl.pallas_call(
        body, grid=(n_s // sb,),
        in_specs=[rows(h_ * dk, 0), rows(h_ * dk, 1), rows(h_ * dv, 1), rows(h_ * dv, 2), rows(h_ * dk, 0), st_in,
                  pl.BlockSpec((1, dv), lambda i: (0, 0))],
        out_specs=[pl.BlockSpec((sb, h_ * dv), lambda i: (i, 0)), st],
        out_shape=[jax.ShapeDtypeStruct((n_s, h_ * dv), F32), jax.ShapeDtypeStruct(state.shape[1:], F32)],
        compiler_params=_cparams(("parallel",)), name="gla_decode")(
            z1, z1, z1, z1, log_a, state, norm_w.reshape(1, dv))


def _conv_silu(ext, w, c):
    acc = ext[5:5 + c] * w[0:1]
    for i in range(1, GDN_CONV):
        acc = acc + ext[5 + i:5 + i + c] * w[i:i + 1]
    return _silu(acc)


def _softplus(x):
    return jnp.maximum(x, 0.0) + jnp.log1p(jnp.exp(-jnp.abs(x)))


def _l2n(x):
    return x * lax.rsqrt(jnp.sum(x * x, axis=-1, keepdims=True) + NORM_EPS)


def gdn_prompt(z3, z4, conv_w, a_log, dt_bias, norm_w, *, n_seq, seq_len):
    c, h_, dk, dv = CHUNK, GDN_HEADS, GDN_DK, GDN_DV
    kw = h_ * dk
    nck = seq_len // c
    _, pm, incl, after = _level_tables(c)
    nl = len(pm)
    import numpy as np
    strict = (np.arange(c)[:, None] > np.arange(c)[None, :]).astype(np.float32)
    tabs = jnp.asarray(np.concatenate([incl, after, np.ones((c, c), np.float32)], axis=0), BF16)
    masks = jnp.stack([jnp.asarray(incl), jnp.asarray(strict), jnp.eye(c, dtype=F32)] + [jnp.asarray(p) for p in pm])
    strict_pad = jnp.asarray(np.concatenate([strict, np.zeros((c, LANES - c), np.float32)], axis=1))
    qscale = dk ** -0.5

    def body(q_ref, k_ref, v_ref, zg_ref, ab_ref, cw_ref, al_ref, db_ref, t_ref, m_ref, sp_ref, n_ref,
             o_ref, st_ref, s_scr, hist):
        ci = pl.program_id(1)

        @pl.when(ci == 0)
        def _():
            s_scr[...] = jnp.zeros_like(s_scr)
            hist[...] = jnp.zeros_like(hist)

        def conv(ref, j):
            cols = slice(j * kw, (j + 1) * kw)
            raw = ref[...]
            ext = jnp.concatenate([hist[:, cols], raw], axis=0)
            y = _conv_silu(ext, cw_ref[:, cols], c)
            hist[:, cols] = raw[c - 8:c]
            return y

        qc, kc, vc = conv(q_ref, 0), conv(k_ref, 1), conv(v_ref, 2)
        ab = ab_ref[...]
        g = -jnp.exp(al_ref[...]) * _softplus(ab[:, :h_] + db_ref[...])
        beta = _sigmoid(ab[:, h_:])
        sums = _table_dot(t_ref[...], jnp.concatenate([g, jnp.zeros((c, LANES - h_), F32)], axis=1))
        e_cum = jnp.exp(sums[0:c])
        e_rest = jnp.exp(sums[c:2 * c])
        e_last = jnp.exp(sums[2 * c:2 * c + 1])
        grel = jnp.concatenate([g[:, h:h + 1] * sp_ref[...] for h in range(h_)], axis=1)
        rel = _table_dot(t_ref[0:c], grel)
        m_incl, m_strict, m_eye = m_ref[0], m_ref[1], m_ref[2]
        qs, ks, vs, kbs, decs, amat, tinv = [], [], [], [], [], [], []
        for h in range(h_):
            hs = slice(h * dk, (h + 1) * dk)
            qs.append(_l2n(qc[:, hs]) * qscale)
            ks.append(_l2n(kc[:, hs]))
            vs.append(vc[:, hs])
            decs.append(m_incl * jnp.exp(m_incl * rel[:, h * LANES:h * LANES + c]))
            kbs.append(ks[h] * beta[:, h:h + 1])
            amat.append(m_strict * _dot_nt(kbs[h], ks[h]) * decs[h])
            tinv.append(m_eye - m_ref[3] * amat[h])
        for l in range(1, nl):
            tinv = [tinv[h] - _dot_hi(_dot_hi(tinv[h], m_ref[3 + l] * amat[h]), tinv[h]) for h in range(h_)]
        for h in range(h_):
            hs = slice(h * dk, (h + 1) * dk)
            q, k, v, kb, dec, t = qs[h], ks[h], vs[h], kbs[h], decs[h], tinv[h]
            bcol = beta[:, h:h + 1]
            tw = _bdot(t, jnp.concatenate([kb * e_cum[:, h:h + 1], v * bcol], axis=1))
            s_old = s_scr[h]
            both = _bdot(jnp.concatenate([q * e_cum[:, h:h + 1], tw[:, :dk]], axis=0), s_old)
            u = tw[:, dk:] - both[c:]
            o = both[:c] + _bdot(_dot_nt(q, k) * dec, u)
            s_scr[h] = e_last[:, h:h + 1] * s_old + _dot_tn(k * e_rest[:, h:h + 1], u)
            o_ref[:, hs] = (_rms_rows(o, n_ref[...]) * _silu(zg_ref[:, hs])).astype(o_ref.dtype)

        @pl.when(ci == nck - 1)
        def _():
            st_ref[0] = s_scr[...]

    def rows(width, col):
        return pl.BlockSpec((c, width), lambda b, i: (b * nck + i, col))

    def const(arr):
        nd = arr.ndim
        return pl.BlockSpec(arr.shape, lambda b, i: (0,) * nd)

    cw = conv_w
    al = a_log.reshape(1, h_)
    db = dt_bias.reshape(1, h_)
    nw = norm_w.reshape(1, dv)
    return pl.pallas_call(
        body, grid=(n_seq, nck),
        in_specs=[rows(kw, 1), rows(kw, 2), rows(kw, 3), rows(kw, 4), pl.BlockSpec((c, 2 * h_), lambda b, i: (b * nck + i, 0)),
                  const(cw), const(al), const(db), const(tabs), const(masks), const(strict_pad), const(nw)],
        out_specs=[rows(kw, 0), pl.BlockSpec((1, h_, dk, dv), lambda b, i: (b, 0, 0, 0))],
        out_shape=[jax.ShapeDtypeStruct((n_seq * seq_len, kw), BF16), jax.ShapeDtypeStruct((n_seq, h_, dk, dv), F32)],
        scratch_shapes=[pltpu.VMEM((h_, dk, dv), F32), pltpu.VMEM((8, 3 * kw), F32)],
        compiler_params=_cparams(("parallel", "arbitrary")), name="gdn_prompt")(
            z3, z3, z3, z3, z4, cw, al, db, tabs, masks, strict_pad, nw)


def gdn_decode(z3, z4, state, conv_buf, conv_w, a_log, dt_bias, norm_w, *, row0, layer):
    sb = 8
    n_s = state.shape[1]
    h_, dk, dv = GDN_HEADS, GDN_DK, GDN_DV
    kw = h_ * dk
    r0 = row0 // sb
    qscale = dk ** -0.5

    def body(q_ref, k_ref, v_ref, zg_ref, ab_ref, hb_ref, cw_ref, al_ref, db_ref, s_ref, n_ref, o_ref, ns_ref):
        def conv(ref, j):
            cols = slice(j * kw, (j + 1) * kw)
            acc = ref[...] * cw_ref[GDN_CONV - 1:GDN_CONV, cols]
            for i in range(GDN_CONV - 1):
                acc = acc + hb_ref[:, i, cols] * cw_ref[i:i + 1, cols]
            return _silu(acc)

        qc, kc, vc = conv(q_ref, 0), conv(k_ref, 1), conv(v_ref, 2)
        ab = ab_ref[...]
        eg = jnp.exp(-jnp.exp(al_ref[...]) * _softplus(ab[:, :h_] + db_ref[...]))
        beta = _sigmoid(ab[:, h_:])
        for h in range(h_):
            hs = slice(h * dk, (h + 1) * dk)
            q = _l2n(qc[:, hs]) * qscale
            k = _l2n(kc[:, hs])
            v = vc[:, hs]
            cols = _columns([k, q], dk)
            qk = jnp.sum(q * k, axis=-1, keepdims=True)
            outs = []
            for s in range(sb):
                s_old = s_ref[0, s, h]
                kcol = cols[:, s:s + 1]
                k_s = jnp.sum(kcol * s_old, axis=0, keepdims=True)
                q_s = jnp.sum(cols[:, sb + s:sb + s + 1] * s_old, axis=0, keepdims=True)
                e = eg[s:s + 1, h:h + 1]
                u = beta[s:s + 1, h:h + 1] * (v[s:s + 1, :] - e * k_s)
                ns_ref[s, h] = e * s_old + kcol * u
                outs.append(e * q_s + qk[s:s + 1, :] * u)
            o = jnp.concatenate(outs, axis=0)
            o_ref[:, hs] = _rms_rows(o, n_ref[...]) * _silu(zg_ref[:, hs])

    def rows(width, col):
        return pl.BlockSpec((sb, width), lambda i: (r0 + i, col))

    def const(arr):
        nd = arr.ndim
        return pl.BlockSpec(arr.shape, lambda i: (0,) * nd)

    st_in = pl.BlockSpec((1, sb, h_, dk, dv), lambda i: (layer, i, 0, 0, 0))
    st = pl.BlockSpec((sb, h_, dk, dv), lambda i: (i, 0, 0, 0))
    al = a_log.reshape(1, h_)
    db = dt_bias.reshape(1, h_)
    nw = norm_w.reshape(1, dv)
    return pl.pallas_call(
        body, grid=(n_s // sb,),
        in_specs=[rows(kw, 1), rows(kw, 2), rows(kw, 3), rows(kw, 4), pl.BlockSpec((sb, 2 * h_), lambda i: (r0 + i, 0)),
                  pl.BlockSpec((sb, GDN_CONV - 1, 3 * kw), lambda i: (i, 0, 0)), const(conv_w), const(al), const(db),
                  st_in, const(nw)],
        out_specs=[pl.BlockSpec((sb, kw), lambda i: (i, 0)), st],
        out_shape=[jax.ShapeDtypeStruct((n_s, kw), F32), jax.ShapeDtypeStruct(state.shape[1:], F32)],
        compiler_params=_cparams(("parallel",)), name="gdn_decode")(
            z3, z3, z3, z3, z4, conv_buf, conv_w, al, db, state, nw)


def trunk_layer(x, xb, pe_b, states, lw, *, n_seq, seq_len, route_tm):
    t_p = n_seq * seq_len
    n_s = x.shape[0] - t_p
    gla_s, s5_re, s5_im, gdn_s, conv_s = states
    w_in = lw['w_in']
    z1 = dense(xb, w_in[:, 0:3072].astype(BF16))
    a_lr = dense(xb, w_in[:, 3072:3088].astype(BF16))
    z3 = dense(xb, w_in[:, 3088:8208].astype(BF16))
    z4 = dense(xb, w_in[:, 8208:8224].astype(BF16))
    log_a = dense(a_lr, lw['gla_w_gate'], bias=lw['gla_b_gate'], act='log_decay')
    br_a_p, gla_pt = gla_prompt(z1, log_a, lw['gla_norm'], n_seq=n_seq, seq_len=seq_len)
    gla_p = jnp.swapaxes(gla_pt, 2, 3)
    br_a_s, gla_n = gla_decode(z1, log_a, gla_s, lw['gla_norm'], row0=t_p, layer=lw['layer'])
    br_a = jnp.concatenate([br_a_p, br_a_s.astype(BF16)], axis=0)
    tabs = s5_tables(lw['s5_lam_re'], lw['s5_lam_im'], lw['s5_log_dt'], lw['s5_b_re'], lw['s5_b_im'],
                     lw['s5_c_re'], lw['s5_c_im'])
    y_s, s5r_p, s5i_p, s5r_n, s5i_n = s5_branch(z3, n_seq, seq_len, s5_re, s5_im, tabs, lw['s5_d'])
    br_s = glu_gate(y_s, lw['s5_w_glu'].astype(BF16), lw['s5_b_glu'])
    br_c_p, gdn_p = gdn_prompt(z3, z4, lw['gdn_conv_w'], lw['gdn_a_log'], lw['gdn_dt_bias'], lw['gdn_norm'],
                               n_seq=n_seq, seq_len=seq_len)
    br_c_s, gdn_n = gdn_decode(z3, z4, gdn_s, conv_s, lw['gdn_conv_w'], lw['gdn_a_log'], lw['gdn_dt_bias'],
                               lw['gdn_norm'], row0=t_p, layer=lw['layer'])
    br_c = jnp.concatenate([br_c_p, br_c_s.astype(BF16)], axis=0)
    qkv_cols = slice(S5_WIDTH, S5_WIDTH + 2 * GDN_KW + GDN_VW)
    conv_p = jnp.stack([z3[(b + 1) * seq_len - (GDN_CONV - 1):(b + 1) * seq_len, qkv_cols] for b in range(n_seq)])
    conv_n = jnp.concatenate([conv_s[:, 1:], z3[t_p:, None, qkv_cols]], axis=1)
    merged = merge_branches(xb, w_in[:, 8224:].astype(BF16), br_a, br_s, br_c, lw['w_branch_a'].astype(BF16),
                            lw['w_branch_s'].astype(BF16), lw['w_branch_c'].astype(BF16))
    x1, x1b, x1p = out_proj_ln(merged, lw['w_out'].astype(BF16), x, lw['ln1_g'], lw['ln1_b'])
    x2, x2b = moe_layer(x1, x1b, x1p, lw, route_tm=route_tm)
    x3, x3b = ple_mix(x2, x2b, lw['ple_w_gate'].astype(BF16), pe_b, lw['ple_w_proj'].astype(BF16))
    return x3, x3b, (gla_p, s5r_p, s5i_p, gdn_p, conv_p), (gla_n, s5r_n, s5i_n, gdn_n, conv_n)


_NAMES = ('w_in', 'gla_w_gate', 'gla_b_gate', 'gla_norm', 's5_lam_re', 's5_lam_im', 's5_log_dt', 's5_b_re',
          's5_b_im', 's5_c_re', 's5_c_im', 's5_d', 's5_w_glu', 's5_b_glu', 'gdn_conv_w', 'gdn_a_log',
          'gdn_dt_bias', 'gdn_norm', 'w_branch_a', 'w_branch_s', 'w_branch_c', 'w_out', 'ln1_g', 'ln1_b',
          'ln2_g', 'ln2_b', 'moe_w_router', 'moe_b_router', 'moe_w1', 'moe_w3', 'moe_w2', 'moe_ws1', 'moe_ws3',
          'moe_ws2', 'ple_w_proj', 'ple_w_gate')


_STACKED = ('moe_w1', 'moe_w3', 'moe_w2')


def run_trunk(x_prompt, x_sample, p_prompt, p_sample, states, weights, *, route_tm):
    n_seq, seq_len, d = x_prompt.shape
    n_s = x_sample.shape[0]
    t_p = n_seq * seq_len
    depth = weights[0].shape[0]
    x = jnp.concatenate([x_prompt.reshape(t_p, d), x_sample.reshape(n_s, d)], axis=0)
    xb = x.astype(BF16)
    pe = jnp.concatenate([p_prompt.reshape(depth, t_p, -1), p_sample.reshape(depth, n_s, -1)], axis=1).astype(BF16)
    new_p, new_s = [], []
    for i in range(depth):
        lw = {n: (w if n in _STACKED else w[i]) for n, w in zip(_NAMES, weights)}
        lw['layer'] = i
        st = (states[0], states[1][i], states[2][i], states[3], states[4][i])
        x, xb, st_p, st_s = trunk_layer(x, xb, pe[i], st, lw, n_seq=n_seq, seq_len=seq_len, route_tm=route_tm)
        new_p.append(st_p)
        new_s.append(st_s)
    gla_p, s5r_p, s5i_p, gdn_p, conv_p = (jnp.stack(f) for f in zip(*new_p))
    gla_s, s5r_s, s5i_s, gdn_s, conv_s = (jnp.stack(f) for f in zip(*new_s))
    yp = x[:t_p].reshape(n_seq, seq_len, d)
    ys = x[t_p:].reshape(n_s, 1, d)
    return (yp, ys, gla_p, gla_s, s5r_p, s5r_s, s5i_p, s5i_s, gdn_p, gdn_s, conv_p, conv_s)


def kernel(x_prompt, x_sample, p_prompt, p_sample, state_gla, state_s5_re, state_s5_im, state_gdn, state_gdn_conv,
           w_in, gla_w_gate, gla_b_gate, gla_norm, s5_lam_re, s5_lam_im, s5_log_dt, s5_b_re, s5_b_im, s5_c_re,
           s5_c_im, s5_d, s5_w_glu, s5_b_glu, gdn_conv_w, gdn_a_log, gdn_dt_bias, gdn_norm, w_branch_a,
           w_branch_s, w_branch_c, w_out, ln1_g, ln1_b, ln2_g, ln2_b, moe_w_router, moe_b_router, moe_w1, moe_w3,
           moe_w2, moe_ws1, moe_ws3, moe_ws2, ple_w_proj, ple_w_gate):
    weights = (w_in, gla_w_gate, gla_b_gate, gla_norm, s5_lam_re, s5_lam_im, s5_log_dt, s5_b_re, s5_b_im, s5_c_re,
               s5_c_im, s5_d, s5_w_glu, s5_b_glu, gdn_conv_w, gdn_a_log, gdn_dt_bias, gdn_norm, w_branch_a,
               w_branch_s, w_branch_c, w_out, ln1_g, ln1_b, ln2_g, ln2_b, moe_w_router, moe_b_router, moe_w1,
               moe_w3, moe_w2, moe_ws1, moe_ws3, moe_ws2, ple_w_proj, ple_w_gate)
    states = (state_gla, state_s5_re, state_s5_im, state_gdn, state_gdn_conv)
    return run_trunk(x_prompt, x_sample, p_prompt, p_sample, states, weights, route_tm=640)
```

```python
import functools
import math

import jax
import jax.numpy as jnp
from jax import lax
from jax.experimental import pallas as pl
from jax.experimental.pallas import tpu as pltpu

F32 = jnp.float32
BF16 = jnp.bfloat16
I32 = jnp.int32

D_MODEL = 2048
DEPTH = 4
GLA_HEADS, GLA_DK, GLA_DV = 4, 128, 256
GLA_KW, GLA_VW, GLA_RANK, GLA_TAU = 512, 1024, 16, 16.0
S5_WIDTH, S5_CH, S5_GROUPS, S5_STATE = 1024, 16, 64, 64
GDN_HEADS, GDN_DK, GDN_DV = 8, 128, 128
GDN_KW, GDN_VW, GDN_CONV = 1024, 1024, 4
N_BRANCH = 3
IN_SIZES = (GLA_KW, GLA_KW, GLA_VW, GLA_VW, GLA_RANK, S5_WIDTH, GDN_KW, GDN_KW, GDN_VW, GDN_VW, GDN_HEADS,
            GDN_HEADS, N_BRANCH * D_MODEL)
CHUNK = 64
N_EXPERTS, TOP_K, N_GROUPS, TOPK_GROUPS = 64, 8, 8, 4
D_EXPERT = 512
ROUTED_SCALE = 2.5
LN_EPS = 1e-5
NORM_EPS = 1e-6
DEEPNORM_ALPHA = (2 * DEPTH) ** 0.25

LANES = 128
VMEM_LIMIT_BYTES = 56 * 1024 * 1024
S5_CS = 16
S5_TILES = S5_WIDTH // LANES
S5_TSTATE = (LANES // S5_CH) * S5_STATE
MOE_ROWS = 512
MOE_TOK_TILE = 128


def _cparams(sem):
    return pltpu.CompilerParams(dimension_semantics=sem, vmem_limit_bytes=VMEM_LIMIT_BYTES)


def _pick_tile(n, candidates):
    for c in candidates:
        if n % c == 0:
            return c
    return n


def _sigmoid(x):
    return 1.0 / (1.0 + jnp.exp(-x))


def _silu(x):
    return x * _sigmoid(x)


def _gelu_tanh(x):
    return 0.5 * x * (1.0 + jnp.tanh(math.sqrt(2.0 / math.pi) * (x + 0.044715 * (x * x * x))))


def _log_sigmoid(x):
    return jnp.minimum(x, 0.0) - jnp.log1p(jnp.exp(-jnp.abs(x)))


def _bdot(a, b):
    return jnp.dot(a.astype(BF16), b.astype(BF16), preferred_element_type=F32)


def dense(x, w, *, bias=None, act=None, out_dtype=F32, tm=None, tn=None):
    m, k = x.shape
    n = w.shape[1]
    tm = tm or _pick_tile(m, (1664, 640, 512, 256, 128, 64, 32, 16, 8))
    tn = tn or _pick_tile(n, (512, 256, 128))

    def body(x_ref, w_ref, *rest):
        o_ref = rest[-1]
        y = _bdot(x_ref[...], w_ref[...])
        if bias is not None:
            y = y + rest[0][...]
        if act == 'log_decay':
            y = _log_sigmoid(y) / GLA_TAU
        o_ref[...] = y.astype(o_ref.dtype)

    in_specs = [pl.BlockSpec((tm, k), lambda i, j: (i, 0)), pl.BlockSpec((k, tn), lambda i, j: (0, j))]
    args = [x, w]
    if bias is not None:
        in_specs.append(pl.BlockSpec((1, tn), lambda i, j: (0, j)))
        args.append(bias.reshape(1, n))
    return pl.pallas_call(
        body, grid=(m // tm, n // tn), in_specs=in_specs,
        out_specs=pl.BlockSpec((tm, tn), lambda i, j: (i, j)),
        out_shape=jax.ShapeDtypeStruct((m, n), out_dtype),
        compiler_params=_cparams(("parallel", "parallel")), name="dense")(*args)


def swiglu_hidden(x, w1, w3):
    m, k = x.shape
    n = w1.shape[1]
    tm = _pick_tile(m, (1664, 640, 512, 256, 128, 64, 32, 16, 8))
    tn = _pick_tile(n, (512, 256, 128))

    def body(x_ref, w1_ref, w3_ref, o_ref):
        xb = x_ref[...].astype(BF16)
        a = jnp.dot(xb, w1_ref[...].astype(BF16), preferred_element_type=F32)
        b = jnp.dot(xb, w3_ref[...].astype(BF16), preferred_element_type=F32)
        o_ref[...] = (_silu(a) * b).astype(o_ref.dtype)

    return pl.pallas_call(
        body, grid=(m // tm, n // tn),
        in_specs=[pl.BlockSpec((tm, k), lambda i, j: (i, 0)), pl.BlockSpec((k, tn), lambda i, j: (0, j)),
                  pl.BlockSpec((k, tn), lambda i, j: (0, j))],
        out_specs=pl.BlockSpec((tm, tn), lambda i, j: (i, j)),
        out_shape=jax.ShapeDtypeStruct((m, n), BF16),
        compiler_params=_cparams(("parallel", "parallel")), name="swiglu_hidden")(x, w1, w3)


def glu_gate(y, w, b):
    m, n = y.shape
    tm = _pick_tile(m, (1664, 640, 512, 256, 128, 64, 32, 16, 8))
    tn = _pick_tile(n, (512, 256, 128))

    def body(y_ref, yt_ref, w_ref, b_ref, o_ref):
        g = _bdot(y_ref[...], w_ref[...]) + b_ref[...]
        o_ref[...] = (yt_ref[...] * _sigmoid(g)).astype(o_ref.dtype)

    return pl.pallas_call(
        body, grid=(m // tm, n // tn),
        in_specs=[pl.BlockSpec((tm, n), lambda i, j: (i, 0)), pl.BlockSpec((tm, tn), lambda i, j: (i, j)),
                  pl.BlockSpec((n, tn), lambda i, j: (0, j)), pl.BlockSpec((1, tn), lambda i, j: (0, j))],
        out_specs=pl.BlockSpec((tm, tn), lambda i, j: (i, j)),
        out_shape=jax.ShapeDtypeStruct((m, n), BF16),
        compiler_params=_cparams(("parallel", "parallel")), name="glu_gate")(y, y, w, b.reshape(1, n))


def merge_branches(xb, w_gates, br_a, br_s, br_c, w_a, w_s, w_c):
    m, k = xb.shape
    d = w_a.shape[1]
    kb = br_a.shape[1]
    tm = _pick_tile(m, (1664, 832, 640, 512, 256, 128, 64, 32, 16, 8))
    tn = _pick_tile(d, (256, 128))
    nj = d // tn

    def body(x_ref, g0_ref, g1_ref, g2_ref, a_ref, s_ref, c_ref, wa_ref, ws_ref, wc_ref, o_ref):
        x = x_ref[...]
        acc = _sigmoid(_bdot(x, g0_ref[...])) * _bdot(a_ref[...], wa_ref[...])
        acc = acc + _sigmoid(_bdot(x, g1_ref[...])) * _bdot(s_ref[...], ws_ref[...])
        acc = acc + _sigmoid(_bdot(x, g2_ref[...])) * _bdot(c_ref[...], wc_ref[...])
        o_ref[...] = acc.astype(o_ref.dtype)

    def gate_spec(b):
        return pl.BlockSpec((k, tn), lambda i, j: (0, b * nj + j))

    act_spec = pl.BlockSpec((tm, kb), lambda i, j: (i, 0))
    w_spec = pl.BlockSpec((kb, tn), lambda i, j: (0, j))
    return pl.pallas_call(
        body, grid=(m // tm, nj),
        in_specs=[pl.BlockSpec((tm, k), lambda i, j: (i, 0)), gate_spec(0), gate_spec(1), gate_spec(2),
                  act_spec, act_spec, act_spec, w_spec, w_spec, w_spec],
        out_specs=pl.BlockSpec((tm, tn), lambda i, j: (i, j)),
        out_shape=jax.ShapeDtypeStruct((m, d), BF16),
        compiler_params=_cparams(("parallel", "parallel")), name="merge_branches")(
            xb, w_gates, w_gates, w_gates, br_a, br_s, br_c, w_a, w_s, w_c)


def _layer_norm_rows(y, g, b):
    mu = jnp.mean(y, axis=-1, keepdims=True)
    yc = y - mu
    var = jnp.mean(yc * yc, axis=-1, keepdims=True)
    return yc * lax.rsqrt(var + LN_EPS) * g + b


_HI16 = 0xFFFF0000


def _pack_bf16_halves(y):
    c = y.shape[1] // 2
    bits = lax.bitcast_convert_type(y.astype(BF16).astype(F32), jnp.uint32)
    return (bits[:, c:] & jnp.uint32(_HI16)) | (bits[:, :c] >> 16)


def _unpack_bf16_halves(p):
    return (lax.bitcast_convert_type(p << 16, F32), lax.bitcast_convert_type(p & jnp.uint32(_HI16), F32))


def out_proj_ln(merged, w_out, x, g, b):
    m, k = merged.shape
    d = w_out.shape[1]
    tm = _pick_tile(m, (416, 256, 128, 64, 32, 16))

    def body(m_ref, w_ref, x_ref, g_ref, b_ref, o_ref, ob_ref, op_ref):
        y = DEEPNORM_ALPHA * x_ref[...] + _bdot(m_ref[...], w_ref[...])
        y = _layer_norm_rows(y, g_ref[...], b_ref[...])
        o_ref[...] = y
        ob_ref[...] = y.astype(BF16)
        op_ref[...] = _pack_bf16_halves(y)

    row = pl.BlockSpec((1, d), lambda i: (0, 0))
    return pl.pallas_call(
        body, grid=(m // tm,),
        in_specs=[pl.BlockSpec((tm, k), lambda i: (i, 0)), pl.BlockSpec((k, d), lambda i: (0, 0)),
                  pl.BlockSpec((tm, d), lambda i: (i, 0)), row, row],
        out_specs=[pl.BlockSpec((tm, d), lambda i: (i, 0)), pl.BlockSpec((tm, d), lambda i: (i, 0)),
                   pl.BlockSpec((tm, d // 2), lambda i: (i, 0))],
        out_shape=[jax.ShapeDtypeStruct((m, d), F32), jax.ShapeDtypeStruct((m, d), BF16),
                   jax.ShapeDtypeStruct((m, d // 2), jnp.uint32)],
        compiler_params=_cparams(("parallel",)), name="out_proj_ln")(
            merged, w_out, x, g.reshape(1, d), b.reshape(1, d))


def ple_mix(x, xb, w_gate, pe, w_proj):
    m, d = x.shape
    kp = pe.shape[1]
    tm = _pick_tile(m, (1664, 640, 512, 256, 128, 64, 32, 16))
    tn = _pick_tile(d, (512, 256, 128))

    def body(xb_ref, wg_ref, pe_ref, wp_ref, x_ref, o_ref, ob_ref):
        y = x_ref[...] + _sigmoid(_bdot(xb_ref[...], wg_ref[...])) * _bdot(pe_ref[...], wp_ref[...])
        o_ref[...] = y
        ob_ref[...] = y.astype(BF16)

    return pl.pallas_call(
        body, grid=(m // tm, d // tn),
        in_specs=[pl.BlockSpec((tm, d), lambda i, j: (i, 0)), pl.BlockSpec((d, tn), lambda i, j: (0, j)),
                  pl.BlockSpec((tm, kp), lambda i, j: (i, 0)), pl.BlockSpec((kp, tn), lambda i, j: (0, j)),
                  pl.BlockSpec((tm, tn), lambda i, j: (i, j))],
        out_specs=[pl.BlockSpec((tm, tn), lambda i, j: (i, j)), pl.BlockSpec((tm, tn), lambda i, j: (i, j))],
        out_shape=[jax.ShapeDtypeStruct((m, d), F32), jax.ShapeDtypeStruct((m, d), BF16)],
        compiler_params=_cparams(("parallel", "parallel")), name="ple_mix")(xb, w_gate, pe, w_proj, x)


def s5_tables(lam_re, lam_im, log_dt, b_re, b_im, c_re, c_im):
    hp = lax.Precision.HIGHEST
    cs, nt, gl = S5_CS, S5_TILES, LANES // S5_CH
    dt = jnp.exp(log_dt)[:, None]
    mag = jnp.exp(lam_re * dt)
    ab_re, ab_im = mag * jnp.cos(lam_im * dt), mag * jnp.sin(lam_im * dt)
    den = lam_re * lam_re + lam_im * lam_im
    nr = ab_re - 1.0
    co_re = (nr * lam_re + ab_im * lam_im) / den
    co_im = (ab_im * lam_re - nr * lam_im) / den
    bb_re = co_re[..., None] * b_re - co_im[..., None] * b_im
    bb_im = co_re[..., None] * b_im + co_im[..., None] * b_re
    pr, pi = [jnp.ones_like(ab_re)], [jnp.zeros_like(ab_im)]
    for _ in range(cs):
        pr.append(pr[-1] * ab_re - pi[-1] * ab_im)
        pi.append(pr[-2] * ab_im + pi[-1] * ab_re)
    ap_re, ap_im = jnp.stack(pr), jnp.stack(pi)
    abr = ap_re[:, :, :, None] * bb_re - ap_im[:, :, :, None] * bb_im
    abi = ap_re[:, :, :, None] * bb_im + ap_im[:, :, :, None] * bb_re
    kern = (jnp.einsum('gcp,egpd->egcd', c_re, abr[:cs], precision=hp)
            - jnp.einsum('gcp,egpd->egcd', c_im, abi[:cs], precision=hp))
    same_group = jnp.eye(gl, dtype=bool)

    def block_diag(a, g_axis, h_axis):
        shape = [1] * (a.ndim + 1)
        shape[g_axis if g_axis < h_axis else g_axis + 1] = gl
        shape[h_axis] = gl
        return jnp.where(same_group.reshape(shape), jnp.expand_dims(a, h_axis), 0.0).astype(BF16)

    k5 = kern.reshape(cs, nt, gl, S5_CH, S5_CH).transpose(1, 0, 2, 4, 3)
    kcat = block_diag(k5, 2, 4).reshape(nt, cs, LANES, LANES).transpose(0, 2, 1, 3).reshape(nt, LANES, cs * LANES)
    toep = jnp.concatenate([jnp.pad(kcat[:, :, :(cs - s) * LANES], ((0, 0), (0, 0), (s * LANES, 0)))
                            for s in range(cs)], axis=1)
    def state_lanes(re, im, n_rows):
        rep = jnp.tile(jnp.eye(S5_STATE, dtype=F32), (1, gl))
        full = jnp.concatenate([jnp.einsum('...p,pq->...q', re, rep, precision=hp),
                                jnp.einsum('...p,pq->...q', im, rep, precision=hp)], axis=-1)
        full = full.reshape(nt, n_rows, 2 * S5_TSTATE)
        row_g = (jnp.arange(n_rows) // S5_CH) % gl
        col_h = (jnp.arange(2 * S5_TSTATE) // S5_STATE) % gl
        return jnp.where(row_g[:, None] == col_h[None, :], full, 0.0).astype(BF16)

    er = abr[:cs][::-1].reshape(cs, nt, gl, S5_STATE, S5_CH).transpose(1, 0, 2, 4, 3)
    ei = abi[:cs][::-1].reshape(cs, nt, gl, S5_STATE, S5_CH).transpose(1, 0, 2, 4, 3)
    bend = state_lanes(er, ei, cs * LANES)
    car = c_re[None] * ap_re[:, :, None, :] - c_im[None] * ap_im[:, :, None, :]
    cai = -(c_re[None] * ap_im[:, :, None, :] + c_im[None] * ap_re[:, :, None, :])
    car = car.reshape(cs + 1, nt, gl, S5_CH, S5_STATE).transpose(1, 0, 2, 3, 4)
    cai = cai.reshape(cs + 1, nt, gl, S5_CH, S5_STATE).transpose(1, 0, 2, 3, 4)
    ccar_t = state_lanes(car, cai, (cs + 1) * LANES)

    def state_row(re, im):
        return jnp.concatenate([re.reshape(nt, 1, S5_TSTATE), im.reshape(nt, 1, S5_TSTATE)], axis=-1)

    return dict(toep=toep, bend=bend, bbar=bend[:, (cs - 1) * LANES:], c0_t=ccar_t[:, :LANES],
                ccar_t=ccar_t[:, LANES:], a1=state_row(ap_re[1], ap_im[1]), acs=state_row(ap_re[cs], ap_im[cs]))


def _chunk_rows(u_ref, nc):
    return jnp.concatenate([u_ref[pl.ds(s, nc, stride=S5_CS), :].astype(BF16) for s in range(S5_CS)], axis=1)


def s5_chunk_states(u_src, bend, *, t_p):
    nt, kc, n = bend.shape
    nc = t_p // S5_CS
    tn = 512

    def body(u_ref, b_ref, o_ref):
        o_ref[0] = jnp.dot(_chunk_rows(u_ref, nc), b_ref[0], preferred_element_type=F32)

    return pl.pallas_call(
        body, grid=(nt, n // tn),
        in_specs=[pl.BlockSpec((t_p, LANES), lambda j, n_: (0, j)),
                  pl.BlockSpec((1, kc, tn), lambda j, n_: (j, 0, n_))],
        out_specs=pl.BlockSpec((1, nc, tn), lambda j, n_: (j, 0, n_)),
        out_shape=jax.ShapeDtypeStruct((nt, nc, n), F32),
        compiler_params=_cparams(("parallel", "parallel")), name="s5_chunk_states")(u_src, bend)


def s5_carry_scan(xe, acs, n_seq):
    nt, nc, n = xe.shape
    per = nc // n_seq
    half = n // 2

    def body(x_ref, a_ref, hp_ref, hf_ref):
        ar = a_ref[0, :, :half]
        ai = a_ref[0, :, half:]

        def step(k, carry):
            hr, hi = carry
            hp_ref[0, pl.ds(k, 1), :] = jnp.concatenate([hr, hi], axis=1)
            x = x_ref[0, pl.ds(k, 1), :]
            return (ar * hr - ai * hi + x[:, :half], ar * hi + ai * hr + x[:, half:])

        zero = jnp.zeros((1, half), F32)
        hr, hi = lax.fori_loop(0, per, step, (zero, zero))
        hf_ref[0, 0] = jnp.concatenate([hr, hi], axis=1)

    return pl.pallas_call(
        body, grid=(nt, n_seq),
        in_specs=[pl.BlockSpec((1, per, n), lambda j, b: (j, b, 0)), pl.BlockSpec((1, 1, n), lambda j, b: (j, 0, 0))],
        out_specs=[pl.BlockSpec((1, per, n), lambda j, b: (j, b, 0)),
                   pl.BlockSpec((1, 1, 1, n), lambda j, b: (j, b, 0, 0))],
        out_shape=[jax.ShapeDtypeStruct((nt, nc, n), F32), jax.ShapeDtypeStruct((nt, n_seq, 1, n), F32)],
        compiler_params=_cparams(("parallel", "parallel")), name="s5_carry_scan")(xe, acs)


def s5_outputs(u_src, toep, hprev, ccar, d_skip, *, t_p):
    nt, kc, _ = toep.shape
    ns = hprev.shape[2]
    nc = t_p // S5_CS
    tn = 512
    per = tn // LANES

    def body(u_ref, t_ref, h_ref, c_ref, d_ref, o_ref):
        uc = _chunk_rows(u_ref, nc)
        hb = h_ref[0].astype(BF16)
        for n_ in range(kc // tn):
            cols = slice(n_ * tn, (n_ + 1) * tn)
            y = jnp.dot(uc, t_ref[0, :, cols], preferred_element_type=F32) + _dot_nt(hb, c_ref[0, cols, :])
            for i in range(per):
                rows = pl.ds(n_ * per + i, nc, stride=S5_CS)
                o_ref[rows, :] = _gelu_tanh(y[:, i * LANES:(i + 1) * LANES] + d_ref[...] * u_ref[rows, :])

    return pl.pallas_call(
        body, grid=(nt,),
        in_specs=[pl.BlockSpec((t_p, LANES), lambda j: (0, j)),
                  pl.BlockSpec((1, kc, kc), lambda j: (j, 0, 0)), pl.BlockSpec((1, nc, ns), lambda j: (j, 0, 0)),
                  pl.BlockSpec((1, kc, ns), lambda j: (j, 0, 0)), pl.BlockSpec((1, LANES), lambda j: (0, j))],
        out_specs=pl.BlockSpec((t_p, LANES), lambda j: (0, j)),
        out_shape=jax.ShapeDtypeStruct((t_p, nt * LANES), F32),
        compiler_params=_cparams(("parallel",)), name="s5_outputs")(
            u_src, toep, hprev, ccar, d_skip.reshape(1, nt * LANES))


def s5_decode(u_src, h_re, h_im, bbar, a1, c0, d_skip, *, row0):
    s = h_re.shape[0]
    nt = bbar.shape[0]
    w = nt * LANES
    ts = S5_TSTATE
    rb = row0 // s

    def body(u_ref, hr_ref, hi_ref, b_ref, a_ref, c_ref, d_ref, y_ref, nr_ref, ni_ref):
        uu = u_ref[...]
        x = _bdot(uu, b_ref[0])
        ar, ai = a_ref[0, :, :ts], a_ref[0, :, ts:]
        hr, hi = hr_ref[...], hi_ref[...]
        nr = ar * hr - ai * hi + x[:, :ts]
        ni = ar * hi + ai * hr + x[:, ts:]
        nr_ref[...] = nr
        ni_ref[...] = ni
        y = _dot_nt(jnp.concatenate([nr, ni], axis=1), c_ref[0]) + d_ref[...] * uu
        y_ref[...] = _gelu_tanh(y)

    col = pl.BlockSpec((s, LANES), lambda j: (0, j))
    st = pl.BlockSpec((s, ts), lambda j: (0, j))
    tab = pl.BlockSpec((1, LANES, 2 * ts), lambda j: (j, 0, 0))
    return pl.pallas_call(
        body, grid=(nt,),
        in_specs=[pl.BlockSpec((s, LANES), lambda j: (rb, j)), st, st, tab,
                  pl.BlockSpec((1, 1, 2 * ts), lambda j: (j, 0, 0)), tab,
                  pl.BlockSpec((1, LANES), lambda j: (0, j))],
        out_specs=[col, st, st],
        out_shape=[jax.ShapeDtypeStruct((s, w), F32), jax.ShapeDtypeStruct(h_re.shape, F32),
                   jax.ShapeDtypeStruct(h_im.shape, F32)],
        compiler_params=_cparams(("parallel",)), name="s5_decode")(u_src, h_re, h_im, bbar, a1, c0, d_skip.reshape(1, w))


def s5_branch(u_src, n_seq, seq_len, h_re, h_im, tabs, d_skip):
    t_p = n_seq * seq_len
    nt = S5_TILES
    xe = s5_chunk_states(u_src, tabs['bend'], t_p=t_p)
    hprev, hfin = s5_carry_scan(xe, tabs['acs'], n_seq)
    y_p = s5_outputs(u_src, tabs['toep'], hprev, tabs['ccar_t'], d_skip, t_p=t_p)
    hfin = hfin.reshape(nt, n_seq, 2, S5_TSTATE).transpose(2, 1, 0, 3).reshape(2, n_seq, S5_GROUPS, S5_STATE)
    s_rows = u_src.shape[0] - t_p
    y_s, nr, ni = s5_decode(u_src, h_re.reshape(s_rows, -1), h_im.reshape(s_rows, -1), tabs['bbar'],
                            tabs['a1'], tabs['c0_t'], d_skip, row0=t_p)
    return (jnp.concatenate([y_p, y_s], axis=0), hfin[0], hfin[1], nr.reshape(h_re.shape), ni.reshape(h_im.shape))


def moe_route(xb, w_router_t, b_router, *, tm):
    t, d = xb.shape
    ne, ng, gs = N_EXPERTS, N_GROUPS, N_EXPERTS // N_GROUPS
    neg = -jnp.inf

    def body(x_ref, w_ref, b_ref, u_ref, idx_ref, wt_ref, rank_ref, cnt_ref, carry_ref):
        @pl.when(pl.program_id(0) == 0)
        def _():
            carry_ref[...] = jnp.zeros_like(carry_ref)

        logits = lax.dot_general(w_ref[...], x_ref[...], (((1,), (1,)), ((), ())), preferred_element_type=F32)
        scores = _sigmoid(logits).reshape(ng, gs, tm)
        choice = scores + b_ref[...].reshape(ng, gs, 1)
        e_in = lax.broadcasted_iota(I32, (ng, gs, tm), 1).astype(F32)
        g_id = lax.broadcasted_iota(I32, (ng, 1, tm), 0).astype(F32)
        e_id = g_id * gs + e_in
        m1 = jnp.max(choice, axis=1, keepdims=True)
        i1 = jnp.min(jnp.where(choice == m1, e_in, float(gs)), axis=1, keepdims=True)
        m2 = jnp.max(jnp.where(e_in == i1, neg, choice), axis=1, keepdims=True)
        gscore = m1 + m2
        keep = jnp.zeros((ng, 1, tm), F32)
        for _ in range(TOPK_GROUPS):
            gm = jnp.max(gscore, axis=0, keepdims=True)
            gi = jnp.min(jnp.where(gscore == gm, g_id, float(ng)), axis=0, keepdims=True)
            hit = g_id == gi
            keep = jnp.where(hit, 1.0, keep)
            gscore = jnp.where(hit, neg, gscore)
        cand = jnp.where(keep > 0.0, choice, neg)
        member = jnp.zeros((ng, gs, tm), F32)
        picks, wts = [], []
        for _ in range(TOP_K):
            cm = jnp.max(jnp.max(cand, axis=1, keepdims=True), axis=0, keepdims=True)
            ei = jnp.min(jnp.min(jnp.where(cand == cm, e_id, float(ne)), axis=1, keepdims=True), axis=0, keepdims=True)
            sel = e_id == ei
            wts.append(jnp.sum(jnp.sum(jnp.where(sel, scores, 0.0), axis=1, keepdims=True), axis=0, keepdims=True))
            picks.append(ei)
            member = jnp.where(sel, 1.0, member)
            cand = jnp.where(sel, neg, cand)
        wsum = wts[0]
        for w in wts[1:]:
            wsum = wsum + w
        member2 = member.reshape(ne, tm)
        prefix = jnp.dot(member2.astype(BF16), u_ref[...], preferred_element_type=F32) + carry_ref[:, 0:1]
        prefix = prefix.reshape(ng, gs, tm)
        for j in range(TOP_K):
            sel = e_id == picks[j]
            rk = jnp.sum(jnp.sum(jnp.where(sel, prefix, 0.0), axis=1, keepdims=True), axis=0, keepdims=True)
            idx_ref[j:j + 1, :] = picks[j].reshape(1, tm).astype(I32)
            rank_ref[j:j + 1, :] = rk.reshape(1, tm).astype(I32)
            wt_ref[j:j + 1, :] = (wts[j] / wsum * ROUTED_SCALE).reshape(1, tm)
        carry_ref[...] = carry_ref[...] + jnp.sum(member2, axis=1, keepdims=True)
        cnt_ref[...] = carry_ref[...]

    upper = jnp.triu(jnp.ones((tm, tm), F32), 1).astype(BF16)
    tok = pl.BlockSpec((TOP_K, tm), lambda i: (0, i))
    idx, wt, rank, cnt = pl.pallas_call(
        body, grid=(t // tm,),
        in_specs=[pl.BlockSpec((tm, d), lambda i: (i, 0)), pl.BlockSpec((ne, d), lambda i: (0, 0)),
                  pl.BlockSpec((ne, 1), lambda i: (0, 0)), pl.BlockSpec((tm, tm), lambda i: (0, 0))],
        out_specs=[tok, tok, tok, pl.BlockSpec((ne, LANES), lambda i: (0, 0))],
        out_shape=[jax.ShapeDtypeStruct((TOP_K, t), I32), jax.ShapeDtypeStruct((TOP_K, t), F32),
                   jax.ShapeDtypeStruct((TOP_K, t), I32), jax.ShapeDtypeStruct((ne, LANES), F32)],
        scratch_shapes=[pltpu.VMEM((ne, LANES), F32)],
        compiler_params=_cparams(("arbitrary",)), name="moe_route")(xb, w_router_t, b_router.reshape(ne, 1), upper)
    return idx, wt, rank, cnt[:, 0]


def moe_experts(x, row_tok, block_e, n_used, w1, w3, w2, *, rows, layer):
    t, dh = x.shape
    d = 2 * dh
    nb = row_tok.shape[0]
    f = w1.shape[3]

    be = block_e
    first = jnp.concatenate([jnp.ones((1,), I32), (be[1:] != be[:-1]).astype(I32)])
    seg = jnp.cumsum(first) - 1
    seg_end = jnp.sum((seg[None, :] <= seg[:, None]).astype(I32), axis=1)
    has_next = (seg_end < nb).astype(I32)
    next_e = be[jnp.minimum(seg_end, nb - 1)]
    seg_info = jnp.stack([first, seg % 2, has_next, next_e]).astype(I32)

    def body(be_ref, nu_ref, sg_ref, x_hbm, tok_ref, tokn_ref, w1_hbm, w3_hbm, w2_hbm, o_ref, buf, sem,
             w1b, w3b, w2b, wf1, wf3, wf2, semw):
        i = pl.program_id(0)
        slot = lax.rem(i, 2)
        nxt = 1 - slot
        groups = 4
        per = rows // groups

        def weight_copies(ex, ws):
            return (pltpu.make_async_copy(w1_hbm.at[layer, ex], wf1.at[ws], semw.at[ws]),
                    pltpu.make_async_copy(w3_hbm.at[layer, ex], wf3.at[ws], semw.at[ws]),
                    pltpu.make_async_copy(w2_hbm.at[layer, ex], wf2.at[ws], semw.at[ws]))

        def row_copy(tref, r, sl):
            tok = tref[0, 0, r]
            return pltpu.make_async_copy(x_hbm.at[pl.ds(tok, 1), :], buf.at[sl, pl.ds(r, 1), :], sem.at[sl])

        def gather_loop(tref, sl):
            def issue(r, c):
                row_copy(tref, r, sl).start(priority=1)
                return c
            lax.fori_loop(0, rows, issue, 0, unroll=8)

        def gather_group(tref, sl, g):
            for r in range(g * per, (g + 1) * per):
                row_copy(tref, r, sl).start(priority=1)

        def block_wait(sl):
            pltpu.make_async_copy(x_hbm.at[pl.ds(0, rows), :], buf.at[sl], sem.at[sl]).wait()

        @pl.when(i == 0)
        def _():
            gather_loop(tok_ref, 0)
            for c in weight_copies(be_ref[0], 0):
                c.start()

        ws = sg_ref[1, i]

        @pl.when(sg_ref[0, i] == 1)
        def _():
            for c in weight_copies(be_ref[i], ws):
                c.wait()
            w1b[...] = wf1[ws].astype(BF16)
            w3b[...] = wf3[ws].astype(BF16)
            w2b[...] = wf2[ws].astype(BF16)

            @pl.when(sg_ref[2, i] == 1)
            def _():
                for c in weight_copies(sg_ref[3, i], 1 - ws):
                    c.start()

        block_wait(slot)

        @pl.when(i < nu_ref[0])
        def _():
            lo, hi = _unpack_bf16_halves(buf[slot])
            lo, hi = lo.astype(BF16), hi.astype(BF16)
            gather_group(tokn_ref, nxt, 0)
            h1 = (jnp.dot(lo, w1b[:dh], preferred_element_type=F32)
                  + jnp.dot(hi, w1b[dh:], preferred_element_type=F32))
            gather_group(tokn_ref, nxt, 1)
            h3 = (jnp.dot(lo, w3b[:dh], preferred_element_type=F32)
                  + jnp.dot(hi, w3b[dh:], preferred_element_type=F32))
            gather_group(tokn_ref, nxt, 2)
            hb = (_silu(h1) * h3).astype(BF16)
            gather_group(tokn_ref, nxt, 3)
            o_ref[...] = _pack_bf16_halves(jnp.dot(hb, w2b[...], preferred_element_type=F32))

        @pl.when(i >= nu_ref[0])
        def _():
            gather_loop(tokn_ref, nxt)
            o_ref[...] = jnp.zeros_like(o_ref)

        @pl.when(i == nb - 1)
        def _():
            block_wait(nxt)

    hbm = pl.BlockSpec(memory_space=pl.ANY)
    grid_spec = pltpu.PrefetchScalarGridSpec(
        num_scalar_prefetch=3, grid=(nb,),
        in_specs=[hbm,
                  pl.BlockSpec((1, 1, rows), lambda i, be, nu, sg: (i, 0, 0), memory_space=pltpu.SMEM),
                  pl.BlockSpec((1, 1, rows), lambda i, be, nu, sg: (jnp.minimum(i + 1, nb - 1), 0, 0),
                               memory_space=pltpu.SMEM),
                  hbm, hbm, hbm],
        out_specs=pl.BlockSpec((rows, dh), lambda i, be, nu, sg: (i, 0)),
        scratch_shapes=[pltpu.VMEM((2, rows, dh), jnp.uint32), pltpu.SemaphoreType.DMA((2,)),
                        pltpu.VMEM((d, f), BF16), pltpu.VMEM((d, f), BF16), pltpu.VMEM((f, d), BF16),
                        pltpu.VMEM((2, d, f), F32), pltpu.VMEM((2, d, f), F32), pltpu.VMEM((2, f, d), F32),
                        pltpu.SemaphoreType.DMA((2,))])
    return pl.pallas_call(
        body, grid_spec=grid_spec, out_shape=jax.ShapeDtypeStruct((nb * rows, dh), jnp.uint32),
        compiler_params=_cparams(("arbitrary",)), name="moe_experts")(
            block_e, n_used, seg_info, x, row_tok, row_tok, w1, w3, w2)


def moe_combine_ln(ys, dest, wts, x, shared, g, b, *, tm):
    t, d = x.shape
    nt = t // tm

    def body(ys_hbm, d_ref, dn_ref, w_ref, x_ref, s_ref, g_ref, b_ref, o_ref, ob_ref, buf, sem):
        i = pl.program_id(0)
        slot = lax.rem(i, 2)

        def gather(dref, sl):
            def issue(r, c):
                row = dref[0, 0, r]
                pltpu.make_async_copy(ys_hbm.at[pl.ds(row, 1), :], buf.at[sl, pl.ds(r, 1), :],
                                      sem.at[sl]).start(priority=1)
                return c
            lax.fori_loop(0, TOP_K * tm, issue, 0, unroll=8)

        @pl.when(i == 0)
        def _():
            gather(d_ref, 0)

        @pl.when(i + 1 < nt)
        def _():
            gather(dn_ref, 1 - slot)

        pltpu.make_async_copy(ys_hbm.at[pl.ds(0, TOP_K * tm), :], buf.at[slot], sem.at[slot]).wait()
        w = w_ref[...]
        acc_lo = jnp.zeros((tm, d // 2), F32)
        acc_hi = jnp.zeros((tm, d // 2), F32)
        for j in range(TOP_K):
            lo, hi = _unpack_bf16_halves(buf[slot, j * tm:(j + 1) * tm, :])
            acc_lo = acc_lo + w[:, j:j + 1] * lo
            acc_hi = acc_hi + w[:, j:j + 1] * hi
        acc = DEEPNORM_ALPHA * x_ref[...] + s_ref[...] + jnp.concatenate([acc_lo, acc_hi], axis=1)
        y = _layer_norm_rows(acc, g_ref[...], b_ref[...])
        o_ref[...] = y
        ob_ref[...] = y.astype(BF16)

    row = pl.BlockSpec((1, d), lambda i: (0, 0))
    tile = pl.BlockSpec((tm, d), lambda i: (i, 0))
    return pl.pallas_call(
        body, grid=(nt,),
        in_specs=[pl.BlockSpec(memory_space=pl.ANY),
                  pl.BlockSpec((1, 1, TOP_K * tm), lambda i: (i, 0, 0), memory_space=pltpu.SMEM),
                  pl.BlockSpec((1, 1, TOP_K * tm), lambda i: (jnp.minimum(i + 1, nt - 1), 0, 0),
                               memory_space=pltpu.SMEM),
                  pl.BlockSpec((tm, TOP_K), lambda i: (i, 0)), tile, tile, row, row],
        out_specs=[tile, tile],
        out_shape=[jax.ShapeDtypeStruct((t, d), F32), jax.ShapeDtypeStruct((t, d), BF16)],
        scratch_shapes=[pltpu.VMEM((2, TOP_K * tm, d // 2), jnp.uint32), pltpu.SemaphoreType.DMA((2,))],
        compiler_params=_cparams(("arbitrary",)), name="moe_combine_ln")(
            ys, dest, dest, wts, x, shared, g.reshape(1, d), b.reshape(1, d))


def moe_layer(x, xb, xp, lw, *, route_tm, rows=MOE_ROWS, tok_tile=MOE_TOK_TILE):
    t, d = x.shape
    idx, wt, rank, counts = moe_route(xb, lw['moe_w_router'].T.astype(BF16), lw['moe_b_router'], tm=route_tm)
    counts = counts.astype(I32)
    padded = (counts + rows - 1) // rows * rows
    pad_end = jnp.cumsum(padded)
    pad_start = pad_end - padded
    onehot = idx[:, :, None] == jnp.arange(N_EXPERTS, dtype=I32)[None, None, :]
    dest = jnp.sum(jnp.where(onehot, pad_start[None, None, :], 0), axis=-1) + rank
    n_rows = -(-(t * TOP_K + N_EXPERTS * (rows - 1)) // rows) * rows
    nb = n_rows // rows
    tok_id = jnp.broadcast_to(jnp.arange(t, dtype=I32)[None, :], (TOP_K, t))
    row_tok = jnp.zeros((n_rows,), I32).at[dest.reshape(-1)].set(tok_id.reshape(-1))
    blk_start = jnp.arange(nb, dtype=I32) * rows
    block_e = jnp.minimum(jnp.sum((pad_end[None, :] <= blk_start[:, None]).astype(I32), axis=1), N_EXPERTS - 1)
    n_used = (pad_end[-1] // rows).astype(I32).reshape(1)
    ys = moe_experts(xp, row_tok.reshape(nb, 1, rows), block_e, n_used, lw['moe_w1'], lw['moe_w3'], lw['moe_w2'],
                     rows=rows, layer=lw['layer'])
    hs = swiglu_hidden(xb, lw['moe_ws1'], lw['moe_ws3'])
    shared = dense(hs, lw['moe_ws2'])
    dest_t = dest.reshape(TOP_K, t // tok_tile, tok_tile).transpose(1, 0, 2).reshape(t // tok_tile, 1, TOP_K * tok_tile)
    return moe_combine_ln(ys, dest_t, wt.T, x, shared, lw['ln2_g'], lw['ln2_b'], tm=tok_tile)


def _level_tables(c):
    import numpy as np
    idx = np.arange(c)
    t, r = idx[:, None], idx[None, :]
    wl, pm = [], []
    b = 1
    while b < c:
        blk, odd = t // b, (t // b) % 2 == 1
        w = np.where(odd, (r >= blk * b) & (r <= t), (r > t) & (r <= blk * b + b - 1))
        wl.append(w.astype(np.float32))
        pm.append((odd & (r // b == blk - 1)).astype(np.float32))
        b *= 2
    incl = (r <= t).astype(np.float32)
    after = (r > t).astype(np.float32)
    return wl, pm, incl, after


def _split3(x):
    hi = x.astype(BF16)
    r1 = x - hi.astype(F32)
    mid = r1.astype(BF16)
    lo = (r1 - mid.astype(F32)).astype(BF16)
    return hi, mid, lo


def _table_dot(tab, x):
    hi, mid, lo = _split3(x)
    return (jnp.dot(tab, hi, preferred_element_type=F32) + jnp.dot(tab, mid, preferred_element_type=F32)
            + jnp.dot(tab, lo, preferred_element_type=F32))


def _dot_hi(a, b):
    ah = a.astype(BF16)
    al = (a - ah.astype(F32)).astype(BF16)
    bh = b.astype(BF16)
    bl = (b - bh.astype(F32)).astype(BF16)
    return (jnp.dot(ah, bh, preferred_element_type=F32) + jnp.dot(ah, bl, preferred_element_type=F32)
            + jnp.dot(al, bh, preferred_element_type=F32))


def _dot_nt(a, b):
    return lax.dot_general(a.astype(BF16), b.astype(BF16), (((1,), (1,)), ((), ())), preferred_element_type=F32)


def _dot_tn(a, b):
    return lax.dot_general(a.astype(BF16), b.astype(BF16), (((0,), (0,)), ((), ())), preferred_element_type=F32)


def _rms_rows(o, w):
    return o * lax.rsqrt(jnp.mean(o * o, axis=-1, keepdims=True) + NORM_EPS) * w


def gla_prompt(z1, log_a, norm_w, *, n_seq, seq_len):
    c, h_, dk, dv = CHUNK, GLA_HEADS, GLA_DK, GLA_DV
    nck = seq_len // c
    wl, pm, incl, after = _level_tables(c)
    nl = len(wl)
    wcat = jnp.asarray(jnp.concatenate([jnp.asarray(w) for w in wl] + [jnp.asarray(incl), jnp.asarray(after)], axis=0),
                       BF16)
    pmask = jnp.stack([jnp.eye(c, dtype=F32)] + [jnp.asarray(p) for p in pm])
    scale = dk ** -0.5

    def body(q_ref, k_ref, v_ref, r_ref, g_ref, w_ref, p_ref, n_ref, o_ref, st_ref, s_scr):
        ci = pl.program_id(1)

        @pl.when(ci == 0)
        def _():
            s_scr[...] = jnp.zeros_like(s_scr)

        x = _table_dot(w_ref[...], g_ref[...])
        ex = jnp.exp(x)
        for h in range(h_):
            ks = slice(h * dk, (h + 1) * dk)
            vs = slice(h * dv, (h + 1) * dv)
            q = q_ref[:, ks] * scale
            k = k_ref[:, ks]
            v = v_ref[:, vs]
            scores = p_ref[0] * _dot_nt(q, k)
            for l in range(nl):
                f = ex[l * c:(l + 1) * c, ks]
                scores = scores + p_ref[l + 1] * _dot_nt(q * f, k * f)
            st = s_scr[h]
            o = _dot_nt(q * ex[nl * c:(nl + 1) * c, ks], st) + _bdot(scores, v)
            tot = x[(nl + 1) * c - 1:(nl + 1) * c, ks]
            s_scr[h] = jnp.exp(tot) * st + _dot_tn(v, k * ex[(nl + 1) * c:(nl + 2) * c, ks])
            o_ref[:, vs] = (_rms_rows(o, n_ref[...]) * _silu(r_ref[:, vs])).astype(o_ref.dtype)

        @pl.when(ci == nck - 1)
        def _():
            st_ref[0] = s_scr[...]

    def rows(width, col):
        return pl.BlockSpec((c, width), lambda b, i: (b * nck + i, col))

    return pl.pallas_call(
        body, grid=(n_seq, nck),
        in_specs=[rows(h_ * dk, 0), rows(h_ * dk, 1), rows(h_ * dv, 1), rows(h_ * dv, 2), rows(h_ * dk, 0),
                  pl.BlockSpec(wcat.shape, lambda b, i: (0, 0)), pl.BlockSpec(pmask.shape, lambda b, i: (0, 0, 0)),
                  pl.BlockSpec((1, dv), lambda b, i: (0, 0))],
        out_specs=[rows(h_ * dv, 0), pl.BlockSpec((1, h_, dv, dk), lambda b, i: (b, 0, 0, 0))],
        out_shape=[jax.ShapeDtypeStruct((n_seq * seq_len, h_ * dv), BF16),
                   jax.ShapeDtypeStruct((n_seq, h_, dv, dk), F32)],
        scratch_shapes=[pltpu.VMEM((h_, dv, dk), F32)],
        compiler_params=_cparams(("parallel", "arbitrary")), name="gla_prompt")(
            z1, z1, z1, z1, log_a, wcat, pmask, norm_w.reshape(1, dv))


def _columns(rows_list, width):
    used = sum(r.shape[0] for r in rows_list)
    stack = jnp.concatenate(list(rows_list) + [jnp.zeros((LANES - used, width), F32)], axis=0)
    return stack.T


def gla_decode(z1, log_a, state, norm_w, *, row0, layer):
    sb = 8
    n_s = state.shape[1]
    h_, dk, dv = GLA_HEADS, GLA_DK, GLA_DV
    scale = dk ** -0.5
    r0 = row0 // sb

    def body(q_ref, k_ref, v_ref, r_ref, g_ref, s_ref, n_ref, o_ref, ns_ref):
        for h in range(h_):
            ks = slice(h * dk, (h + 1) * dk)
            vs = slice(h * dv, (h + 1) * dv)
            cols = _columns([jnp.exp(g_ref[:, ks]), k_ref[:, ks], q_ref[:, ks] * scale], dk)
            v = v_ref[:, vs]
            outs = []
            for s in range(sb):
                s_new = cols[:, s:s + 1] * s_ref[0, s, h] + cols[:, sb + s:sb + s + 1] * v[s:s + 1, :]
                ns_ref[s, h] = s_new
                outs.append(jnp.sum(cols[:, 2 * sb + s:2 * sb + s + 1] * s_new, axis=0, keepdims=True))
            o = jnp.concatenate(outs, axis=0)
            o_ref[:, vs] = _rms_rows(o, n_ref[...]) * _silu(r_ref[:, vs])

    def rows(width, col):
        return pl.BlockSpec((sb, width), lambda i: (r0 + i, col))

    st_in = pl.BlockSpec((1, sb, h_, dk, dv), lambda i: (layer, i, 0, 0, 0))
    st = pl.BlockSpec((sb, h_, dk, dv), lambda i: (i, 0, 0, 0))
    return pl.pallas_call(
        body, grid=(n_s // sb,),
        in_specs=[rows(h_ * dk, 0), rows(h_ * dk, 1), rows(h_ * dv, 1), rows(h_ * dv, 2), rows(h_ * dk, 0), st_in,
                  pl.BlockSpec((1, dv), lambda i: (0, 0))],
        out_specs=[pl.BlockSpec((sb, h_ * dv), lambda i: (i, 0)), st],
        out_shape=[jax.ShapeDtypeStruct((n_s, h_ * dv), F32), jax.ShapeDtypeStruct(state.shape[1:], F32)],
        compiler_params=_cparams(("parallel",)), name="gla_decode")(
            z1, z1, z1, z1, log_a, state, norm_w.reshape(1, dv))


def _conv_silu(ext, w, c):
    acc = ext[5:5 + c] * w[0:1]
    for i in range(1, GDN_CONV):
        acc = acc + ext[5 + i:5 + i + c] * w[i:i + 1]
    return _silu(acc)


def _softplus(x):
    return jnp.maximum(x, 0.0) + jnp.log1p(jnp.exp(-jnp.abs(x)))


def _l2n(x):
    return x * lax.rsqrt(jnp.sum(x * x, axis=-1, keepdims=True) + NORM_EPS)


def gdn_prompt(z3, z4, conv_w, a_log, dt_bias, norm_w, *, n_seq, seq_len):
    c, h_, dk, dv = CHUNK, GDN_HEADS, GDN_DK, GDN_DV
    kw = h_ * dk
    nck = seq_len // c
    _, pm, incl, after = _level_tables(c)
    nl = len(pm)
    import numpy as np
    strict = (np.arange(c)[:, None] > np.arange(c)[None, :]).astype(np.float32)
    tabs = jnp.asarray(np.concatenate([incl, after, np.ones((c, c), np.float32)], axis=0), BF16)
    masks = jnp.stack([jnp.asarray(incl), jnp.asarray(strict), jnp.eye(c, dtype=F32)] + [jnp.asarray(p) for p in pm])
    strict_pad = jnp.asarray(np.concatenate([strict, np.zeros((c, LANES - c), np.float32)], axis=1))
    qscale = dk ** -0.5

    def body(q_ref, k_ref, v_ref, zg_ref, ab_ref, cw_ref, al_ref, db_ref, t_ref, m_ref, sp_ref, n_ref,
             o_ref, st_ref, s_scr, hist):
        ci = pl.program_id(1)

        @pl.when(ci == 0)
        def _():
            s_scr[...] = jnp.zeros_like(s_scr)
            hist[...] = jnp.zeros_like(hist)

        def conv(ref, j):
            cols = slice(j * kw, (j + 1) * kw)
            raw = ref[...]
            ext = jnp.concatenate([hist[:, cols], raw], axis=0)
            y = _conv_silu(ext, cw_ref[:, cols], c)
            hist[:, cols] = raw[c - 8:c]
            return y

        qc, kc, vc = conv(q_ref, 0), conv(k_ref, 1), conv(v_ref, 2)
        ab = ab_ref[...]
        g = -jnp.exp(al_ref[...]) * _softplus(ab[:, :h_] + db_ref[...])
        beta = _sigmoid(ab[:, h_:])
        sums = _table_dot(t_ref[...], jnp.concatenate([g, jnp.zeros((c, LANES - h_), F32)], axis=1))
        e_cum = jnp.exp(sums[0:c])
        e_rest = jnp.exp(sums[c:2 * c])
        e_last = jnp.exp(sums[2 * c:2 * c + 1])
        grel = jnp.concatenate([g[:, h:h + 1] * sp_ref[...] for h in range(h_)], axis=1)
        rel = _table_dot(t_ref[0:c], grel)
        m_incl, m_strict, m_eye = m_ref[0], m_ref[1], m_ref[2]
        qs, ks, vs, kbs, decs, amat, tinv = [], [], [], [], [], [], []
        for h in range(h_):
            hs = slice(h * dk, (h + 1) * dk)
            qs.append(_l2n(qc[:, hs]) * qscale)
            ks.append(_l2n(kc[:, hs]))
            vs.append(vc[:, hs])
            decs.append(m_incl * jnp.exp(m_incl * rel[:, h * LANES:h * LANES + c]))
            kbs.append(ks[h] * beta[:, h:h + 1])
            amat.append(m_strict * _dot_nt(kbs[h], ks[h]) * decs[h])
            tinv.append(m_eye - m_ref[3] * amat[h])
        for l in range(1, nl):
            tinv = [tinv[h] - _dot_hi(_dot_hi(tinv[h], m_ref[3 + l] * amat[h]), tinv[h]) for h in range(h_)]
        for h in range(h_):
            hs = slice(h * dk, (h + 1) * dk)
            q, k, v, kb, dec, t = qs[h], ks[h], vs[h], kbs[h], decs[h], tinv[h]
            bcol = beta[:, h:h + 1]
            tw = _bdot(t, jnp.concatenate([kb * e_cum[:, h:h + 1], v * bcol], axis=1))
            s_old = s_scr[h]
            both = _bdot(jnp.concatenate([q * e_cum[:, h:h + 1], tw[:, :dk]], axis=0), s_old)
            u = tw[:, dk:] - both[c:]
            o = both[:c] + _bdot(_dot_nt(q, k) * dec, u)
            s_scr[h] = e_last[:, h:h + 1] * s_old + _dot_tn(k * e_rest[:, h:h + 1], u)
            o_ref[:, hs] = (_rms_rows(o, n_ref[...]) * _silu(zg_ref[:, hs])).astype(o_ref.dtype)

        @pl.when(ci == nck - 1)
        def _():
            st_ref[0] = s_scr[...]

    def rows(width, col):
        return pl.BlockSpec((c, width), lambda b, i: (b * nck + i, col))

    def const(arr):
        nd = arr.ndim
        return pl.BlockSpec(arr.shape, lambda b, i: (0,) * nd)

    cw = conv_w
    al = a_log.reshape(1, h_)
    db = dt_bias.reshape(1, h_)
    nw = norm_w.reshape(1, dv)
    return pl.pallas_call(
        body, grid=(n_seq, nck),
        in_specs=[rows(kw, 1), rows(kw, 2), rows(kw, 3), rows(kw, 4), pl.BlockSpec((c, 2 * h_), lambda b, i: (b * nck + i, 0)),
                  const(cw), const(al), const(db), const(tabs), const(masks), const(strict_pad), const(nw)],
        out_specs=[rows(kw, 0), pl.BlockSpec((1, h_, dk, dv), lambda b, i: (b, 0, 0, 0))],
        out_shape=[jax.ShapeDtypeStruct((n_seq * seq_len, kw), BF16), jax.ShapeDtypeStruct((n_seq, h_, dk, dv), F32)],
        scratch_shapes=[pltpu.VMEM((h_, dk, dv), F32), pltpu.VMEM((8, 3 * kw), F32)],
        compiler_params=_cparams(("parallel", "arbitrary")), name="gdn_prompt")(
            z3, z3, z3, z3, z4, cw, al, db, tabs, masks, strict_pad, nw)


def gdn_decode(z3, z4, state, conv_buf, conv_w, a_log, dt_bias, norm_w, *, row0, layer):
    sb = 8
    n_s = state.shape[1]
    h_, dk, dv = GDN_HEADS, GDN_DK, GDN_DV
    kw = h_ * dk
    r0 = row0 // sb
    qscale = dk ** -0.5

    def body(q_ref, k_ref, v_ref, zg_ref, ab_ref, hb_ref, cw_ref, al_ref, db_ref, s_ref, n_ref, o_ref, ns_ref):
        def conv(ref, j):
            cols = slice(j * kw, (j + 1) * kw)
            acc = ref[...] * cw_ref[GDN_CONV - 1:GDN_CONV, cols]
            for i in range(GDN_CONV - 1):
                acc = acc + hb_ref[:, i, cols] * cw_ref[i:i + 1, cols]
            return _silu(acc)

        qc, kc, vc = conv(q_ref, 0), conv(k_ref, 1), conv(v_ref, 2)
        ab = ab_ref[...]
        eg = jnp.exp(-jnp.exp(al_ref[...]) * _softplus(ab[:, :h_] + db_ref[...]))
        beta = _sigmoid(ab[:, h_:])
        for h in range(h_):
            hs = slice(h * dk, (h + 1) * dk)
            q = _l2n(qc[:, hs]) * qscale
            k = _l2n(kc[:, hs])
            v = vc[:, hs]
            cols = _columns([k, q], dk)
            qk = jnp.sum(q * k, axis=-1, keepdims=True)
            outs = []
            for s in range(sb):
                s_old = s_ref[0, s, h]
                kcol = cols[:, s:s + 1]
                k_s = jnp.sum(kcol * s_old, axis=0, keepdims=True)
                q_s = jnp.sum(cols[:, sb + s:sb + s + 1] * s_old, axis=0, keepdims=True)
                e = eg[s:s + 1, h:h + 1]
                u = beta[s:s + 1, h:h + 1] * (v[s:s + 1, :] - e * k_s)
                ns_ref[s, h] = e * s_old + kcol * u
                outs.append(e * q_s + qk[s:s + 1, :] * u)
            o = jnp.concatenate(outs, axis=0)
            o_ref[:, hs] = _rms_rows(o, n_ref[...]) * _silu(zg_ref[:, hs])

    def rows(width, col):
        return pl.BlockSpec((sb, width), lambda i: (r0 + i, col))

    def const(arr):
        nd = arr.ndim
        return pl.BlockSpec(arr.shape, lambda i: (0,) * nd)

    st_in = pl.BlockSpec((1, sb, h_, dk, dv), lambda i: (layer, i, 0, 0, 0))
    st = pl.BlockSpec((sb, h_, dk, dv), lambda i: (i, 0, 0, 0))
    al = a_log.reshape(1, h_)
    db = dt_bias.reshape(1, h_)
    nw = norm_w.reshape(1, dv)
    return pl.pallas_call(
        body, grid=(n_s // sb,),
        in_specs=[rows(kw, 1), rows(kw, 2), rows(kw, 3), rows(kw, 4), pl.BlockSpec((sb, 2 * h_), lambda i: (r0 + i, 0)),
                  pl.BlockSpec((sb, GDN_CONV - 1, 3 * kw), lambda i: (i, 0, 0)), const(conv_w), const(al), const(db),
                  st_in, const(nw)],
        out_specs=[pl.BlockSpec((sb, kw), lambda i: (i, 0)), st],
        out_shape=[jax.ShapeDtypeStruct((n_s, kw), F32), jax.ShapeDtypeStruct(state.shape[1:], F32)],
        compiler_params=_cparams(("parallel",)), name="gdn_decode")(
            z3, z3, z3, z3, z4, conv_buf, conv_w, al, db, state, nw)


def trunk_layer(x, xb, pe_b, states, lw, *, n_seq, seq_len, route_tm):
    t_p = n_seq * seq_len
    n_s = x.shape[0] - t_p
    gla_s, s5_re, s5_im, gdn_s, conv_s = states
    w_in = lw['w_in']
    z1 = dense(xb, w_in[:, 0:3072].astype(BF16))
    a_lr = dense(xb, w_in[:, 3072:3088].astype(BF16))
    z3 = dense(xb, w_in[:, 3088:8208].astype(BF16))
    z4 = dense(xb, w_in[:, 8208:8224].astype(BF16))
    log_a = dense(a_lr, lw['gla_w_gate'], bias=lw['gla_b_gate'], act='log_decay')
    br_a_p, gla_pt = gla_prompt(z1, log_a, lw['gla_norm'], n_seq=n_seq, seq_len=seq_len)
    gla_p = jnp.swapaxes(gla_pt, 2, 3)
    br_a_s, gla_n = gla_decode(z1, log_a, gla_s, lw['gla_norm'], row0=t_p, layer=lw['layer'])
    br_a = jnp.concatenate([br_a_p, br_a_s.astype(BF16)], axis=0)
    tabs = s5_tables(lw['s5_lam_re'], lw['s5_lam_im'], lw['s5_log_dt'], lw['s5_b_re'], lw['s5_b_im'],
                     lw['s5_c_re'], lw['s5_c_im'])
    y_s, s5r_p, s5i_p, s5r_n, s5i_n = s5_branch(z3, n_seq, seq_len, s5_re, s5_im, tabs, lw['s5_d'])
    br_s = glu_gate(y_s, lw['s5_w_glu'].astype(BF16), lw['s5_b_glu'])
    br_c_p, gdn_p = gdn_prompt(z3, z4, lw['gdn_conv_w'], lw['gdn_a_log'], lw['gdn_dt_bias'], lw['gdn_norm'],
                               n_seq=n_seq, seq_len=seq_len)
    br_c_s, gdn_n = gdn_decode(z3, z4, gdn_s, conv_s, lw['gdn_conv_w'], lw['gdn_a_log'], lw['gdn_dt_bias'],
                               lw['gdn_norm'], row0=t_p, layer=lw['layer'])
    br_c = jnp.concatenate([br_c_p, br_c_s.astype(BF16)], axis=0)
    qkv_cols = slice(S5_WIDTH, S5_WIDTH + 2 * GDN_KW + GDN_VW)
    conv_p = jnp.stack([z3[(b + 1) * seq_len - (GDN_CONV - 1):(b + 1) * seq_len, qkv_cols] for b in range(n_seq)])
    conv_n = jnp.concatenate([conv_s[:, 1:], z3[t_p:, None, qkv_cols]], axis=1)
    merged = merge_branches(xb, w_in[:, 8224:].astype(BF16), br_a, br_s, br_c, lw['w_branch_a'].astype(BF16),
                            lw['w_branch_s'].astype(BF16), lw['w_branch_c'].astype(BF16))
    x1, x1b, x1p = out_proj_ln(merged, lw['w_out'].astype(BF16), x, lw['ln1_g'], lw['ln1_b'])
    x2, x2b = moe_layer(x1, x1b, x1p, lw, route_tm=route_tm)
    x3, x3b = ple_mix(x2, x2b, lw['ple_w_gate'].astype(BF16), pe_b, lw['ple_w_proj'].astype(BF16))
    return x3, x3b, (gla_p, s5r_p, s5i_p, gdn_p, conv_p), (gla_n, s5r_n, s5i_n, gdn_n, conv_n)


_NAMES = ('w_in', 'gla_w_gate', 'gla_b_gate', 'gla_norm', 's5_lam_re', 's5_lam_im', 's5_log_dt', 's5_b_re',
          's5_b_im', 's5_c_re', 's5_c_im', 's5_d', 's5_w_glu', 's5_b_glu', 'gdn_conv_w', 'gdn_a_log',
          'gdn_dt_bias', 'gdn_norm', 'w_branch_a', 'w_branch_s', 'w_branch_c', 'w_out', 'ln1_g', 'ln1_b',
          'ln2_g', 'ln2_b', 'moe_w_router', 'moe_b_router', 'moe_w1', 'moe_w3', 'moe_w2', 'moe_ws1', 'moe_ws3',
          'moe_ws2', 'ple_w_proj', 'ple_w_gate')


_STACKED = ('moe_w1', 'moe_w3', 'moe_w2')


def run_trunk(x_prompt, x_sample, p_prompt, p_sample, states, weights, *, route_tm):
    n_seq, seq_len, d = x_prompt.shape
    n_s = x_sample.shape[0]
    t_p = n_seq * seq_len
    depth = weights[0].shape[0]
    x = jnp.concatenate([x_prompt.reshape(t_p, d), x_sample.reshape(n_s, d)], axis=0)
    xb = x.astype(BF16)
    pe = jnp.concatenate([p_prompt.reshape(depth, t_p, -1), p_sample.reshape(depth, n_s, -1)], axis=1).astype(BF16)
    new_p, new_s = [], []
    for i in range(depth):
        lw = {n: (w if n in _STACKED else w[i]) for n, w in zip(_NAMES, weights)}
        lw['layer'] = i
        st = (states[0], states[1][i], states[2][i], states[3], states[4][i])
        x, xb, st_p, st_s = trunk_layer(x, xb, pe[i], st, lw, n_seq=n_seq, seq_len=seq_len, route_tm=route_tm)
        new_p.append(st_p)
        new_s.append(st_s)
    gla_p, s5r_p, s5i_p, gdn_p, conv_p = (jnp.stack(f) for f in zip(*new_p))
    gla_s, s5r_s, s5i_s, gdn_s, conv_s = (jnp.stack(f) for f in zip(*new_s))
    yp = x[:t_p].reshape(n_seq, seq_len, d)
    ys = x[t_p:].reshape(n_s, 1, d)
    return (yp, ys, gla_p, gla_s, s5r_p, s5r_s, s5i_p, s5i_s, gdn_p, gdn_s, conv_p, conv_s)


def kernel(x_prompt, x_sample, p_prompt, p_sample, state_gla, state_s5_re, state_s5_im, state_gdn, state_gdn_conv,
           w_in, gla_w_gate, gla_b_gate, gla_norm, s5_lam_re, s5_lam_im, s5_log_dt, s5_b_re, s5_b_im, s5_c_re,
           s5_c_im, s5_d, s5_w_glu, s5_b_glu, gdn_conv_w, gdn_a_log, gdn_dt_bias, gdn_norm, w_branch_a,
           w_branch_s, w_branch_c, w_out, ln1_g, ln1_b, ln2_g, ln2_b, moe_w_router, moe_b_router, moe_w1, moe_w3,
           moe_w2, moe_ws1, moe_ws3, moe_ws2, ple_w_proj, ple_w_gate):
    weights = (w_in, gla_w_gate, gla_b_gate, gla_norm, s5_lam_re, s5_lam_im, s5_log_dt, s5_b_re, s5_b_im, s5_c_re,
               s5_c_im, s5_d, s5_w_glu, s5_b_glu, gdn_conv_w, gdn_a_log, gdn_dt_bias, gdn_norm, w_branch_a,
               w_branch_s, w_branch_c, w_out, ln1_g, ln1_b, ln2_g, ln2_b, moe_w_router, moe_b_router, moe_w1,
               moe_w3, moe_w2, moe_ws1, moe_ws3, moe_ws2, ple_w_proj, ple_w_gate)
    states = (state_gla, state_s5_re, state_s5_im, state_gdn, state_gdn_conv)
    return run_trunk(x_prompt, x_sample, p_prompt, p_sample, states, weights, route_tm=640)
```
